```python
import math
import jax, jax.numpy as jnp
from jax import lax
import numpy as np

D_MODEL = 1024
BATCH = 8
SEQ = 8192
DEPTH = 2

N_ATT = (DEPTH + 1) // 2
N_LIN = DEPTH // 2

EPS = 1e-6
HEAD_DIM = 64
A_HEADS = 8
A_KV_HEADS = 2
A_GROUP = A_HEADS // A_KV_HEADS
WINDOW = 128
BLOCK = 128
B_HEADS = 8
B_NOPE = 64
B_ROPE = 32
B_VDIM = 64
B_Q_RANK = 384
B_KV_RANK = 256
ROPE_THETA = 10000.0
Q_BLOCK = 128
C_HEADS = 4
C_DK = 64
C_DV = 128
C_GATE_RANK = 16
C_GATE_NORM = 16.0
C_CHUNK = 64
D_HEADS = 4
D_DK = 64
D_DV = 128
D_CONV = 5
D_CHUNK = 64
D_FF = 4 * D_MODEL

A_Q = A_HEADS * HEAD_DIM
A_KV = A_KV_HEADS * HEAD_DIM
IN0_SPLITS = (A_Q, A_KV, A_KV, B_Q_RANK, B_KV_RANK, B_ROPE)
IN0 = sum(IN0_SPLITS)
MIX0 = A_HEADS * HEAD_DIM + B_HEADS * B_VDIM
B_QK = B_NOPE + B_ROPE

C_QK = C_HEADS * C_DK
C_V = C_HEADS * C_DV
D_QK = D_HEADS * D_DK
D_V = D_HEADS * D_DV
CONV_CH = 2 * D_QK + D_V
IN1_SPLITS = (C_QK, C_QK, C_V, C_V, C_GATE_RANK, C_GATE_RANK,
              CONV_CH, D_V, D_HEADS, D_HEADS, D_HEADS, D_HEADS)
IN1 = sum(IN1_SPLITS)
MIX1 = C_V + D_V

kernel_name = "hybrid_bidir_swa_mla_gla_gdn"

F32 = jnp.float32


def rms_norm(x, w):
    xf = x.astype(F32)
    y = xf * lax.rsqrt(jnp.mean(xf * xf, axis=-1, keepdims=True) + EPS)
    return (y * w.astype(F32)).astype(x.dtype)


def l2_norm(x):
    xf = x.astype(F32)
    return (xf * lax.rsqrt(jnp.sum(xf * xf, axis=-1, keepdims=True) + EPS)).astype(x.dtype)


def split_cols(t, sizes):
    idx = [int(i) for i in np.cumsum(sizes)[:-1]]
    return jnp.split(t, idx, axis=-1)


def flip(t):
    return jnp.flip(t, axis=1)


def alibi_slopes(n):
    return jnp.asarray(np.array([2.0 ** (-8.0 * (h + 1) / n) for h in range(n)], dtype=np.float32))


def rope(x):
    S = x.shape[1]
    half = x.shape[-1] // 2
    inv = ROPE_THETA ** (-jnp.arange(half, dtype=F32) / half)
    ang = jnp.arange(S, dtype=F32)[:, None] * inv[None, :]
    cos, sin = jnp.cos(ang)[:, None, :], jnp.sin(ang)[:, None, :]
    x1, x2 = x[..., :half].astype(F32), x[..., half:].astype(F32)
    return jnp.concatenate([x1 * cos - x2 * sin, x2 * cos + x1 * sin], axis=-1).astype(x.dtype)


def window_attention(q, k, v, sink):
    B, S = q.shape[0], q.shape[1]
    nb = S // BLOCK
    qb = q.reshape(B, nb, BLOCK, A_KV_HEADS, A_GROUP, HEAD_DIM)
    pad = ((0, 0), (BLOCK, BLOCK), (0, 0), (0, 0))
    kp = jnp.pad(k, pad).reshape(B, nb + 2, BLOCK, A_KV_HEADS, HEAD_DIM)
    vp = jnp.pad(v, pad).reshape(B, nb + 2, BLOCK, A_KV_HEADS, HEAD_DIM)
    kw = jnp.concatenate([kp[:, :-2], kp[:, 1:-1], kp[:, 2:]], axis=2)
    vw = jnp.concatenate([vp[:, :-2], vp[:, 1:-1], vp[:, 2:]], axis=2)
    s = jnp.einsum('bnqhgd,bnshd->bnhgqs', qb, kw, preferred_element_type=F32) * (HEAD_DIM ** -0.5)
    qi = jnp.arange(BLOCK)[:, None]
    kj = jnp.arange(3 * BLOCK)[None, :]
    dist = qi + BLOCK - kj
    kpos = jnp.arange(nb)[:, None] * BLOCK + jnp.arange(3 * BLOCK)[None, :] - BLOCK
    valid = (jnp.abs(dist) <= WINDOW)[None] & ((kpos >= 0) & (kpos < S))[:, None, :]
    slopes = alibi_slopes(A_HEADS).reshape(A_KV_HEADS, A_GROUP)
    s = s - slopes[:, :, None, None] * jnp.abs(dist).astype(F32)
    s = jnp.where(valid[None, :, None, None], s, -jnp.inf)
    sink_l = sink.astype(F32).reshape(A_KV_HEADS, A_GROUP)[None, None, :, :, None, None]
    m = jnp.maximum(jnp.max(s, axis=-1, keepdims=True), sink_l)
    p = jnp.exp(s - m)
    p = p / (jnp.sum(p, axis=-1, keepdims=True) + jnp.exp(sink_l - m))
    o = jnp.einsum('bnhgqs,bnshd->bnqhgd', p.astype(v.dtype), vw)
    return o.reshape(B, S, A_HEADS * HEAD_DIM)


def latent_attention(c_q, c_kv, k_pe, q_norm_w, w_uq, kv_norm_w, w_ukv):
    B, S = c_q.shape[0], c_q.shape[1]
    q = (rms_norm(c_q, q_norm_w) @ w_uq).reshape(B, S, B_HEADS, B_QK)
    q_nope, q_pe = q[..., :B_NOPE], rope(q[..., B_NOPE:])
    kv = (rms_norm(c_kv, kv_norm_w) @ w_ukv).reshape(B, S, B_HEADS, B_NOPE + B_VDIM)
    k_nope, v = kv[..., :B_NOPE], kv[..., B_NOPE:]
    k_r = rope(k_pe[:, :, None, :])[:, :, 0]
    nb = S // Q_BLOCK
    qn_b = jnp.moveaxis(q_nope.reshape(B, nb, Q_BLOCK, B_HEADS, B_NOPE), 1, 0)
    qp_b = jnp.moveaxis(q_pe.reshape(B, nb, Q_BLOCK, B_HEADS, B_ROPE), 1, 0)
    scale = B_QK ** -0.5

    def block(args):
        qn, qp = args
        s = (jnp.einsum('bqhd,bshd->bhqs', qn, k_nope, preferred_element_type=F32)
             + jnp.einsum('bqhr,bsr->bhqs', qp, k_r, preferred_element_type=F32)) * scale
        p = jax.nn.softmax(s, axis=-1)
        return jnp.einsum('bhqs,bshd->bqhd', p.astype(v.dtype), v)

    o = lax.map(block, (qn_b, qp_b))
    return jnp.moveaxis(o, 0, 1).reshape(B, S, B_HEADS * B_VDIM)


def gla_chunked(q, k, v, log_a):
    B, S, H, DK = q.shape
    DV = v.shape[-1]
    nc = S // C_CHUNK

    def chunks(t):
        return t.reshape(B, nc, C_CHUNK, H, t.shape[-1]).astype(F32)

    qc, kc, vc = chunks(q), chunks(k), chunks(v)
    b = jnp.cumsum(chunks(log_a), axis=2)
    b_last = b[:, :, -1:]
    q_in = qc * jnp.exp(b)
    k_in = kc * jnp.exp(-b)
    k_out = kc * jnp.exp(b_last - b)
    causal_in_chunk = jnp.tril(jnp.ones((C_CHUNK, C_CHUNK), dtype=bool))
    sc = jnp.where(causal_in_chunk, jnp.einsum('bnthd,bnshd->bnhts', q_in, k_in), 0.0)
    o_intra = jnp.einsum('bnhts,bnshv->bnthv', sc, vc)
    d_state = jnp.einsum('bnshd,bnshv->nbhdv', k_out, vc)
    decay = jnp.moveaxis(jnp.exp(b_last[:, :, 0]), 1, 0)

    def step(state, xs):
        dec, ds = xs
        return dec[..., None] * state + ds, state

    _, s_prev = lax.scan(step, jnp.zeros((B, H, DK, DV), F32), (decay, d_state))
    o_inter = jnp.einsum('bnthd,nbhdv->bnthv', q_in, s_prev)
    return (o_intra + o_inter).reshape(B, S, H, DV).astype(v.dtype)


def gla_mixer(q, k, v, g, gl_f, gl_b, w_gate_f, b_gate_f, w_gate_b, b_gate_b, norm_w):
    B, S = q.shape[0], q.shape[1]
    q = q.reshape(B, S, C_HEADS, C_DK) * (C_DK ** -0.5)
    k = k.reshape(B, S, C_HEADS, C_DK)
    v = v.reshape(B, S, C_HEADS, C_DV)
    la_f = (jax.nn.log_sigmoid((gl_f @ w_gate_f + b_gate_f).astype(F32)) / C_GATE_NORM).reshape(B, S, C_HEADS, C_DK)
    la_b = (jax.nn.log_sigmoid((gl_b @ w_gate_b + b_gate_b).astype(F32)) / C_GATE_NORM).reshape(B, S, C_HEADS, C_DK)
    o = gla_chunked(q, k, v, la_f) + flip(gla_chunked(flip(q), flip(k), flip(v), flip(la_b)))
    o = rms_norm(o, norm_w) * jax.nn.silu(g.reshape(B, S, C_HEADS, C_DV))
    return o.reshape(B, S, C_V)


def centred_conv(x, w):
    return lax.conv_general_dilated(
        x, w[:, None, :].astype(x.dtype), window_strides=(1,),
        padding=[(D_CONV // 2, D_CONV // 2)],
        dimension_numbers=('NWC', 'WIO', 'NWC'), feature_group_count=x.shape[-1])


def gated_delta_chunked(q, k, v, beta, g):
    B, S, H, DK = q.shape
    DV = v.shape[-1]
    C = D_CHUNK
    nc = S // C

    def chunks(t):
        return jnp.moveaxis(t.reshape(B, nc, C, H, t.shape[-1]).astype(F32), 3, 2)

    qc, kc, vc = chunks(q), chunks(k), chunks(v)
    bt = jnp.moveaxis(beta.reshape(B, nc, C, H).astype(F32), 3, 2)
    b = jnp.cumsum(jnp.moveaxis(g.reshape(B, nc, C, H).astype(F32), 3, 2), axis=-1)
    b_last = b[..., -1:]
    incl = jnp.tril(jnp.ones((C, C), dtype=bool))
    strict = jnp.tril(jnp.ones((C, C), dtype=bool), -1)
    gamma = jnp.exp(jnp.where(incl, b[..., :, None] - b[..., None, :], -jnp.inf))
    k_beta = kc * bt[..., None]
    lower = jnp.where(strict, jnp.einsum('bnhtd,bnhsd->bnhts', k_beta, kc) * gamma, 0.0)
    t_mat = lower + jnp.eye(C, dtype=F32)
    u = lax.linalg.triangular_solve(t_mat, vc * bt[..., None], left_side=True, lower=True, unit_diagonal=True)
    w = lax.linalg.triangular_solve(t_mat, k_beta * jnp.exp(b)[..., None], left_side=True, lower=True, unit_diagonal=True)
    attn = jnp.einsum('bnhtd,bnhsd->bnhts', qc, kc) * gamma
    q_dec = qc * jnp.exp(b)[..., None]
    k_dec = kc * jnp.exp(b_last - b)[..., None]
    decay = jnp.exp(b_last[..., 0])
    xs = tuple(jnp.moveaxis(t, 1, 0) for t in (u, w, attn, q_dec, k_dec, decay))

    def step(state, xs_c):
        u_c, w_c, attn_c, q_c, k_c, dec_c = xs_c
        v_new = u_c - jnp.einsum('bhcd,bhdv->bhcv', w_c, state)
        o = jnp.einsum('bhcd,bhdv->bhcv', q_c, state) + jnp.einsum('bhts,bhsv->bhtv', attn_c, v_new)
        state = dec_c[..., None, None] * state + jnp.einsum('bhcd,bhcv->bhdv', k_c, v_new)
        return state, o

    _, o = lax.scan(step, jnp.zeros((B, H, DK, DV), F32), xs)
    o = jnp.moveaxis(jnp.moveaxis(o, 0, 1), 2, 3)
    return o.reshape(B, S, H, DV).astype(v.dtype)


def delta_mixer(qkv, z, beta_f_in, beta_b_in, a_f_in, a_b_in, conv_w,
                a_log_f, dt_bias_f, a_log_b, dt_bias_b, norm_w):
    B, S = qkv.shape[0], qkv.shape[1]
    qkv = jax.nn.silu(centred_conv(qkv, conv_w))
    q, k, v = split_cols(qkv, (D_QK, D_QK, D_V))
    q = l2_norm(q.reshape(B, S, D_HEADS, D_DK)) * (D_DK ** -0.5)
    k = l2_norm(k.reshape(B, S, D_HEADS, D_DK))
    v = v.reshape(B, S, D_HEADS, D_DV)
    beta_f = jax.nn.sigmoid(beta_f_in.astype(F32))
    beta_b = jax.nn.sigmoid(beta_b_in.astype(F32))
    g_f = -jnp.exp(a_log_f.astype(F32)) * jax.nn.softplus(a_f_in.astype(F32) + dt_bias_f.astype(F32))
    g_b = -jnp.exp(a_log_b.astype(F32)) * jax.nn.softplus(a_b_in.astype(F32) + dt_bias_b.astype(F32))
    o = (gated_delta_chunked(q, k, v, beta_f, g_f)
         + flip(gated_delta_chunked(flip(q), flip(k), flip(v), flip(beta_b), flip(g_b))))
    o = rms_norm(o, norm_w) * jax.nn.silu(z.reshape(B, S, D_HEADS, D_DV))
    return o.reshape(B, S, D_V)


def setup_inputs(seed: int = 0) -> dict:
    key = jax.random.key(seed)
    ks = iter(jax.random.split(key, 48))

    def nrm(shape, fan_in):
        return jax.random.normal(next(ks), shape, F32) * (fan_in ** -0.5)

    def gain(shape):
        return 1.0 + 0.02 * jax.random.normal(next(ks), shape, F32)

    def small(shape, scale=0.01):
        return scale * jax.random.normal(next(ks), shape, F32)

    def a_log(shape):
        return jnp.log(jax.random.uniform(next(ks), shape, F32, minval=1.0, maxval=16.0))

    def dt_bias(shape):
        dt = jnp.exp(jax.random.uniform(next(ks), shape, F32, minval=math.log(1e-3), maxval=math.log(1e-1)))
        return dt + jnp.log(-jnp.expm1(-dt))

    return {
        "x": jax.random.normal(next(ks), (BATCH, SEQ, D_MODEL), F32),
        "att_norm": gain((N_ATT, D_MODEL)),
        "att_w_in": nrm((N_ATT, D_MODEL, IN0), D_MODEL),
        "att_sink": small((N_ATT, A_HEADS), 0.5),
        "mla_q_norm": gain((N_ATT, B_Q_RANK)),
        "mla_w_uq": nrm((N_ATT, B_Q_RANK, B_HEADS * B_QK), B_Q_RANK),
        "mla_kv_norm": gain((N_ATT, B_KV_RANK)),
        "mla_w_ukv": nrm((N_ATT, B_KV_RANK, B_HEADS * (B_NOPE + B_VDIM)), B_KV_RANK),
        "att_w_out": nrm((N_ATT, MIX0, D_MODEL), MIX0),
        "lin_norm": gain((N_LIN, D_MODEL)),
        "lin_w_in": nrm((N_LIN, D_MODEL, IN1), D_MODEL),
        "gla_w_gate_f": nrm((N_LIN, C_GATE_RANK, C_QK), C_GATE_RANK),
        "gla_b_gate_f": small((N_LIN, C_QK), 0.1),
        "gla_w_gate_b": nrm((N_LIN, C_GATE_RANK, C_QK), C_GATE_RANK),
        "gla_b_gate_b": small((N_LIN, C_QK), 0.1),
        "gla_norm": gain((N_LIN, C_DV)),
        "gdn_conv": nrm((N_LIN, D_CONV, CONV_CH), D_CONV),
        "gdn_a_log_f": a_log((N_LIN, D_HEADS)),
        "gdn_dt_bias_f": dt_bias((N_LIN, D_HEADS)),
        "gdn_a_log_b": a_log((N_LIN, D_HEADS)),
        "gdn_dt_bias_b": dt_bias((N_LIN, D_HEADS)),
        "gdn_norm": gain((N_LIN, D_DV)),
        "lin_w_out": nrm((N_LIN, MIX1, D_MODEL), MIX1),
        "mlp_norm": gain((DEPTH, D_MODEL)),
        "mlp_w1": nrm((DEPTH, D_MODEL, D_FF), D_MODEL),
        "mlp_w2": nrm((DEPTH, D_FF, D_MODEL), D_FF),
        "final_norm": gain((D_MODEL,)),
    }


def reference(x, att_norm, att_w_in, att_sink, mla_q_norm, mla_w_uq, mla_kv_norm, mla_w_ukv,
              att_w_out, lin_norm, lin_w_in, gla_w_gate_f, gla_b_gate_f, gla_w_gate_b, gla_b_gate_b,
              gla_norm, gdn_conv, gdn_a_log_f, gdn_dt_bias_f, gdn_a_log_b, gdn_dt_bias_b, gdn_norm,
              lin_w_out, mlp_norm, mlp_w1, mlp_w2, final_norm):
    B, S = x.shape[0], x.shape[1]
    for layer in range(DEPTH):
        i = layer // 2
        if layer % 2 == 0:
            h = rms_norm(x, att_norm[i])
            a_q, a_k, a_v, c_q, c_kv, k_pe = split_cols(h @ att_w_in[i], IN0_SPLITS)
            o_a = window_attention(a_q.reshape(B, S, A_HEADS, HEAD_DIM),
                                   a_k.reshape(B, S, A_KV_HEADS, HEAD_DIM),
                                   a_v.reshape(B, S, A_KV_HEADS, HEAD_DIM), att_sink[i])
            o_b = latent_attention(c_q, c_kv, k_pe, mla_q_norm[i], mla_w_uq[i],
                                   mla_kv_norm[i], mla_w_ukv[i])
            x = x + jnp.concatenate([o_a, o_b], axis=-1) @ att_w_out[i]
        else:
            h = rms_norm(x, lin_norm[i])
            (c_q, c_k, c_v, c_g, gl_f, gl_b, d_qkv, d_z,
             d_beta_f, d_beta_b, d_a_f, d_a_b) = split_cols(h @ lin_w_in[i], IN1_SPLITS)
            o_c = gla_mixer(c_q, c_k, c_v, c_g, gl_f, gl_b, gla_w_gate_f[i], gla_b_gate_f[i],
                            gla_w_gate_b[i], gla_b_gate_b[i], gla_norm[i])
            o_d = delta_mixer(d_qkv, d_z, d_beta_f, d_beta_b, d_a_f, d_a_b, gdn_conv[i],
                              gdn_a_log_f[i], gdn_dt_bias_f[i], gdn_a_log_b[i], gdn_dt_bias_b[i],
                              gdn_norm[i])
            x = x + jnp.concatenate([o_c, o_d], axis=-1) @ lin_w_out[i]
        m = rms_norm(x, mlp_norm[layer])
        x = x + jnp.square(jax.nn.relu(m @ mlp_w1[layer])) @ mlp_w2[layer]
    return rms_norm(x, final_norm)
```

```python
import functools
import math

import numpy as np
import jax
import jax.numpy as jnp
from jax import lax
from jax.experimental import pallas as pl
from jax.experimental.pallas import tpu as pltpu

F32 = jnp.float32
BF16 = jnp.bfloat16
EPS = 1e-6

LANE = 128
SUBLANE = 8
VMEM_LIMIT_BYTES = 56 * 1024 * 1024

HEAD_DIM = 64
A_HEADS = 8
A_KV_HEADS = 2
A_GROUP = A_HEADS // A_KV_HEADS
WINDOW = 128
BLOCK = 128
B_HEADS = 8
B_NOPE = 64
B_ROPE = 32
B_VDIM = 64
B_Q_RANK = 384
B_KV_RANK = 256
ROPE_THETA = 10000.0
LIN_HEADS = 4
LIN_DK = 64
LIN_DV = 128
GATE_RANK = 16
GATE_NORM = 16.0
CHUNK = 64
CONV_TAPS = 5
STACK = LIN_HEADS * CHUNK
GATE_LANE0 = 2 * GATE_RANK

NT_DIMS = (((1,), (1,)), ((), ()))
TN_DIMS = (((0,), (0,)), ((), ()))


def _cparams(*sem):
    return pltpu.CompilerParams(dimension_semantics=sem, vmem_limit_bytes=VMEM_LIMIT_BYTES)


def _resident(shape):
    nd = len(shape)
    return pl.BlockSpec(shape, lambda *_: (0,) * nd, pipeline_mode=pl.Buffered(1))


def _rms(x, w):
    return x * lax.rsqrt(jnp.mean(x * x, axis=-1, keepdims=True) + EPS) * w


def _sigmoid(x):
    return 1.0 / (1.0 + jnp.exp(-x))


def _softplus(x):
    return jnp.maximum(x, 0.0) + jnp.log1p(jnp.exp(-jnp.abs(x)))


def _dot(a, b):
    return jnp.dot(a, b, preferred_element_type=F32)


def _dot_nt(a, b):
    return lax.dot_general(a, b, NT_DIMS, preferred_element_type=F32)


def _dot_tn(a, b):
    return lax.dot_general(a, b, TN_DIMS, preferred_element_type=F32)


def _dot_exact(a, b):
    return jnp.dot(a, b, preferred_element_type=F32, precision=lax.Precision.HIGHEST)


def _dot_tn_exact(a, b):
    return lax.dot_general(a, b, TN_DIMS, preferred_element_type=F32, precision=lax.Precision.HIGHEST)


def _norm_proj_kernel(x_ref, nw_ref, w_ref, *out_refs, segs):
    xn = _rms(x_ref[...], nw_ref[...]).astype(BF16)
    for (start, width), o_ref in zip(segs, out_refs):
        o_ref[...] = _dot(xn, w_ref[:, start:start + width]).astype(o_ref.dtype)


def norm_proj(x, norm_w, w, segs, dtypes, tm):
    t, k = x.shape
    n = w.shape[1]
    return pl.pallas_call(
        functools.partial(_norm_proj_kernel, segs=tuple(segs)),
        grid=(t // tm,),
        in_specs=[pl.BlockSpec((tm, k), lambda i: (i, 0)),
                  _resident((1, k)),
                  _resident((k, n))],
        out_specs=[pl.BlockSpec((tm, wd), lambda i: (i, 0)) for _, wd in segs],
        out_shape=[jax.ShapeDtypeStruct((t, wd), dt) for (_, wd), dt in zip(segs, dtypes)],
        compiler_params=_cparams("parallel"),
        name="norm_proj",
    )(x, norm_w.reshape(1, k), w)


def _win_attn_kernel(sink_ref, q_ref, kp_ref, kc_ref, kn_ref, vp_ref, vc_ref, vn_ref, o_ref, *, seq):
    n = pl.program_id(1)
    k3 = jnp.concatenate([kp_ref[...], kc_ref[...], kn_ref[...]], axis=0)
    v3 = jnp.concatenate([vp_ref[...], vc_ref[...], vn_ref[...]], axis=0)
    qi = lax.broadcasted_iota(jnp.int32, (BLOCK, 3 * BLOCK), 0)
    kj = lax.broadcasted_iota(jnp.int32, (BLOCK, 3 * BLOCK), 1)
    dist = jnp.abs(qi + BLOCK - kj)
    kpos = n * BLOCK + kj - BLOCK
    valid = (dist <= WINDOW) & (kpos >= 0) & (kpos < seq)
    distf = dist.astype(F32)
    scale = HEAD_DIM ** -0.5
    for pair in range(A_HEADS // 2):
        acc = jnp.zeros((BLOCK, LANE), F32)
        for e in range(2):
            h = 2 * pair + e
            g = h // A_GROUP
            slope = 2.0 ** (-8.0 * (h + 1) / A_HEADS)
            q = q_ref[:, h * LANE:(h + 1) * LANE]
            k = k3[:, g * LANE:(g + 1) * LANE]
            s = _dot_nt(q, k) * scale - slope * distf
            s = jnp.where(valid, s, -jnp.inf)
            sink = sink_ref[h]
            m = jnp.maximum(jnp.max(s, axis=-1, keepdims=True), sink)
            p = jnp.exp(s - m)
            den = jnp.sum(p, axis=-1, keepdims=True) + jnp.exp(sink - m)
            p = (p / den).astype(BF16)
            acc = acc + _dot(p, v3[:, (2 * g + e) * LANE:(2 * g + e + 1) * LANE])
        o_ref[:, pair * LANE:(pair + 1) * LANE] = acc.astype(o_ref.dtype)


def window_attention(q, k, v, sink, batch, seq):
    nb = seq // BLOCK
    cur = lambda b, n: (b * nb + n, 0)
    prev = lambda b, n: (b * nb + jnp.maximum(n - 1, 0), 0)
    nxt = lambda b, n: (b * nb + jnp.minimum(n + 1, nb - 1), 0)
    kw, vw = k.shape[1], v.shape[1]
    return pl.pallas_call(
        functools.partial(_win_attn_kernel, seq=seq),
        grid=(batch, nb),
        in_specs=[pl.BlockSpec(memory_space=pltpu.SMEM),
                  pl.BlockSpec((BLOCK, q.shape[1]), cur),
                  pl.BlockSpec((BLOCK, kw), prev), pl.BlockSpec((BLOCK, kw), cur), pl.BlockSpec((BLOCK, kw), nxt),
                  pl.BlockSpec((BLOCK, vw), prev), pl.BlockSpec((BLOCK, vw), cur), pl.BlockSpec((BLOCK, vw), nxt)],
        out_specs=pl.BlockSpec((BLOCK, A_HEADS * HEAD_DIM), cur),
        out_shape=jax.ShapeDtypeStruct((batch * seq, A_HEADS * HEAD_DIM), BF16),
        compiler_params=_cparams("parallel", "parallel"),
        name="window_attention",
    )(sink, q, k, k, k, v, v, v)


def _mla_q_kernel(cq_ref, nw_ref, w_ref, tab_ref, o_ref):
    xn = _rms(cq_ref[...], nw_ref[...]).astype(BF16)
    tab = tab_ref[...]
    for h in range(B_HEADS):
        cols = slice(h * LANE, (h + 1) * LANE)
        o_ref[:, cols] = (_dot(xn, w_ref[:, cols]) * tab).astype(o_ref.dtype)


def mla_q_prep(cq, norm_w, w, tab, batch, seq, tm):
    nt = seq // tm
    r = cq.shape[1]
    n = w.shape[1]
    return pl.pallas_call(
        _mla_q_kernel,
        grid=(batch, nt),
        in_specs=[pl.BlockSpec((tm, r), lambda b, i: (b * nt + i, 0)),
                  _resident((1, r)), _resident((r, n)),
                  pl.BlockSpec((tm, LANE), lambda b, i: (i, 0))],
        out_specs=pl.BlockSpec((tm, n), lambda b, i: (b * nt + i, 0)),
        out_shape=jax.ShapeDtypeStruct((batch * seq, n), BF16),
        compiler_params=_cparams("parallel", "parallel"),
        name="mla_q_prep",
    )(cq, norm_w.reshape(1, r), w, tab)


def _mla_kv_kernel(ckv_ref, kab_ref, nw_ref, wk_ref, wv_ref, ctab_ref, stab_ref, k_ref, v_ref):
    xn = _rms(ckv_ref[...], nw_ref[...]).astype(BF16)
    kr = kab_ref[:, :LANE] * ctab_ref[...] + kab_ref[:, LANE:] * stab_ref[...]
    for h in range(B_HEADS):
        cols = slice(h * LANE, (h + 1) * LANE)
        k_ref[:, cols] = (_dot(xn, wk_ref[:, cols]) + kr).astype(k_ref.dtype)
    v_ref[...] = _dot(xn, wv_ref[...]).astype(v_ref.dtype)


def mla_kv_prep(ckv, kab, norm_w, wk, wv, ctab, stab, batch, seq, tm):
    nt = seq // tm
    r = ckv.shape[1]
    row = lambda b, i: (b * nt + i, 0)
    pos = lambda b, i: (i, 0)
    return pl.pallas_call(
        _mla_kv_kernel,
        grid=(batch, nt),
        in_specs=[pl.BlockSpec((tm, r), row), pl.BlockSpec((tm, 2 * LANE), row),
                  _resident((1, r)), _resident(wk.shape), _resident(wv.shape),
                  pl.BlockSpec((tm, LANE), pos), pl.BlockSpec((tm, LANE), pos)],
        out_specs=[pl.BlockSpec((tm, wk.shape[1]), row), pl.BlockSpec((tm, wv.shape[1]), row)],
        out_shape=[jax.ShapeDtypeStruct((batch * seq, wk.shape[1]), BF16),
                   jax.ShapeDtypeStruct((batch * seq, wv.shape[1]), BF16)],
        compiler_params=_cparams("parallel", "parallel"),
        name="mla_kv_prep",
    )(ckv, kab, norm_w.reshape(1, r), wk, wv, ctab, stab)


def _mla_flash_kernel(q_ref, k_ref, v_ref, o_ref, *, tk, nk):
    tq = q_ref.shape[0]
    outs = []
    for e in range(2):
        cols = slice(e * LANE, (e + 1) * LANE)
        q = q_ref[:, cols]

        def body(j, carry, cols=cols, q=q):
            m, l, acc = carry
            rows = pl.ds(pl.multiple_of(j * tk, tk), tk)
            s = _dot_nt(q, k_ref[rows, cols])
            m_new = jnp.maximum(m, jnp.max(s, axis=-1, keepdims=True))
            alpha = jnp.exp2(m - m_new)
            p = jnp.exp2(s - m_new)
            l = alpha * l + jnp.sum(p, axis=-1, keepdims=True)
            acc = alpha * acc + _dot(p.astype(BF16), v_ref[rows, :])
            return m_new, l, acc

        init = (jnp.full((tq, 1), -jnp.inf, F32), jnp.zeros((tq, 1), F32), jnp.zeros((tq, LANE), F32))
        _, l, acc = lax.fori_loop(0, nk, body, init)
        outs.append(acc / l)
    lane = lax.broadcasted_iota(jnp.int32, (tq, LANE), 1)
    o_ref[...] = jnp.where(lane < B_VDIM, outs[0], outs[1]).astype(o_ref.dtype)


def mla_flash(q, k, v, batch, seq, tq, tk):
    nq = seq // tq
    npair = B_HEADS // 2
    return pl.pallas_call(
        functools.partial(_mla_flash_kernel, tk=tk, nk=seq // tk),
        grid=(batch, npair, nq),
        in_specs=[pl.BlockSpec((tq, 2 * LANE), lambda b, p, i: (b * nq + i, p)),
                  pl.BlockSpec((seq, 2 * LANE), lambda b, p, i: (b, p)),
                  pl.BlockSpec((seq, LANE), lambda b, p, i: (b, p))],
        out_specs=pl.BlockSpec((tq, LANE), lambda b, p, i: (b * nq + i, p)),
        out_shape=jax.ShapeDtypeStruct((batch * seq, B_HEADS * B_VDIM), BF16),
        compiler_params=_cparams("parallel", "parallel", "parallel"),
        name="mla_flash",
    )(q, k, v)


def _out_proj_kernel(x_ref, a_ref, b_ref, w_ref, o_ref):
    ka = a_ref.shape[1]
    o_ref[...] = x_ref[...] + _dot(a_ref[...], w_ref[:ka, :]) + _dot(b_ref[...], w_ref[ka:, :])


def out_proj(x, a, b, w, tm):
    t, d = x.shape
    row = lambda i: (i, 0)
    return pl.pallas_call(
        _out_proj_kernel,
        grid=(t // tm,),
        in_specs=[pl.BlockSpec((tm, d), row), pl.BlockSpec((tm, a.shape[1]), row),
                  pl.BlockSpec((tm, b.shape[1]), row), _resident(w.shape)],
        out_specs=pl.BlockSpec((tm, d), row),
        out_shape=jax.ShapeDtypeStruct((t, d), F32),
        compiler_params=_cparams("parallel"),
        name="out_proj",
    )(x, a, b, w)


def _mlp_kernel(x_ref, nw_ref, w1_ref, w2_ref, fw_ref, o_ref, *, fchunk, final_norm):
    x = x_ref[...]
    xn = _rms(x, nw_ref[...]).astype(BF16)
    acc = x
    for f in range(w1_ref.shape[1] // fchunk):
        cols = slice(f * fchunk, (f + 1) * fchunk)
        h = jnp.square(jnp.maximum(_dot(xn, w1_ref[:, cols]), 0.0)).astype(BF16)
        acc = acc + _dot(h, w2_ref[cols, :])
    if final_norm:
        acc = _rms(acc, fw_ref[...])
    o_ref[...] = acc


def mlp(x, norm_w, w1, w2, final_w, final_norm, tm, fchunk):
    t, d = x.shape
    row = lambda i: (i, 0)
    return pl.pallas_call(
        functools.partial(_mlp_kernel, fchunk=fchunk, final_norm=final_norm),
        grid=(t // tm,),
        in_specs=[pl.BlockSpec((tm, d), row), _resident((1, d)), _resident(w1.shape),
                  _resident(w2.shape), _resident((1, d))],
        out_specs=pl.BlockSpec((tm, d), row),
        out_shape=jax.ShapeDtypeStruct((t, d), F32),
        compiler_params=_cparams("parallel"),
        name="mlp",
    )(x, norm_w.reshape(1, d), w1, w2, final_w.reshape(1, d))


def _direction_masks(d):
    sgn = 1 - 2 * d
    r = lax.broadcasted_iota(jnp.int32, (STACK, STACK), 0)
    c = lax.broadcasted_iota(jnp.int32, (STACK, STACK), 1)
    same = (r & -CHUNK) == (c & -CHUNK)
    order = ((r & (CHUNK - 1)) - (c & (CHUNK - 1))) * sgn
    i = lax.broadcasted_iota(jnp.int32, (CHUNK, CHUNK), 0)
    j = lax.broadcasted_iota(jnp.int32, (CHUNK, CHUNK), 1)
    cum = (((i - j) * sgn) >= 0).astype(F32)
    return same, same & (order >= 0), same & (order > 0), cum


def _block_diag(x, same):
    return jnp.where(same, jnp.concatenate([x] * LIN_HEADS, axis=0), 0.0)


def _stack_heads(v, width):
    return jnp.concatenate([v[:, h * width:(h + 1) * width] for h in range(LIN_HEADS)], axis=0)


def _scan_rows(b, n, d, nblk):
    return b * nblk + n + d * (nblk - 1 - 2 * n)


def _gla_kernel(q_ref, k_ref, v_ref, sm_ref, wg_ref, bg_ref, o_ref, st_ref, *, ncb):
    d = pl.program_id(1)

    @pl.when(pl.program_id(2) == 0)
    def _():
        st_ref[...] = jnp.zeros_like(st_ref)

    same, causal, _, cum = _direction_masks(d)

    def chunk(ci, carry):
        ce = ci + d * (ncb - 1 - 2 * ci)
        rows = pl.ds(pl.multiple_of(ce * CHUNK, CHUNK), CHUNK)
        z = _dot(sm_ref[rows, :].astype(BF16), wg_ref[...]) + bg_ref[...]
        log_a = (jnp.minimum(z, 0.0) - jnp.log1p(jnp.exp(-jnp.abs(z)))) * (1.0 / GATE_NORM)
        b = _dot_exact(cum, log_a)
        b_last = jnp.sum(log_a, axis=0, keepdims=True)
        q_in = _block_diag(q_ref[rows, :] * (LIN_DK ** -0.5) * jnp.exp(b), same).astype(BF16)
        k = k_ref[rows, :]
        k_in = _block_diag(k * jnp.exp(-b), same).astype(BF16)
        k_out = _block_diag(k * jnp.exp(b_last - b), same).astype(BF16)
        v = _stack_heads(v_ref[rows, :], LIN_DV).astype(BF16)
        st = st_ref[...]
        sc = jnp.where(causal, _dot_nt(q_in, k_in), 0.0).astype(BF16)
        o = _dot(sc, v) + _dot_nt(q_in, st.astype(BF16))
        st_ref[...] = jnp.exp(b_last) * st + _dot_tn(v, k_out)
        for h in range(LIN_HEADS):
            o_ref[rows, h * LIN_DV:(h + 1) * LIN_DV] = o[h * CHUNK:(h + 1) * CHUNK, :]
        return carry

    lax.fori_loop(0, ncb, chunk, 0)


def gla_scan(q, k, v, small, wg, bg, batch, seq, tc):
    nblk = seq // tc
    t = batch * seq
    row = lambda b, d, n: (_scan_rows(b, n, d, nblk), 0)
    dsel = lambda b, d, n: (d, 0, 0)
    hk, hv = q.shape[1], v.shape[1]
    return pl.pallas_call(
        functools.partial(_gla_kernel, ncb=tc // CHUNK),
        grid=(batch, 2, nblk),
        in_specs=[pl.BlockSpec((tc, hk), row), pl.BlockSpec((tc, hk), row), pl.BlockSpec((tc, hv), row),
                  pl.BlockSpec((tc, LANE), row),
                  pl.BlockSpec((None, LANE, hk), dsel), pl.BlockSpec((None, 1, hk), dsel)],
        out_specs=pl.BlockSpec((None, tc, hv), lambda b, d, n: (d, _scan_rows(b, n, d, nblk), 0)),
        out_shape=jax.ShapeDtypeStruct((2, t, hv), F32),
        scratch_shapes=[pltpu.VMEM((LIN_DV, hk), F32)],
        compiler_params=_cparams("parallel", "parallel", "arbitrary"),
        name="gla_scan",
    )(q, k, v, small, wg, bg)


def _gdn_prep_kernel(xp_ref, xc_ref, xn_ref, sm_ref, cw_ref, gsum_ref, alog_ref, dtb_ref,
                     q_ref, k_ref, v_ref, g_ref, buf_ref, *, nt):
    i = pl.program_id(1)
    tm = xc_ref.shape[0]
    halo = SUBLANE
    buf_ref[:halo, :] = xp_ref[...] * (i > 0).astype(F32)
    buf_ref[halo:halo + tm, :] = xc_ref[...]
    buf_ref[halo + tm:, :] = xn_ref[...] * (i < nt - 1).astype(F32)
    acc = jnp.zeros(xc_ref.shape, F32)
    for j in range(CONV_TAPS):
        off = halo + j - CONV_TAPS // 2
        acc = acc + buf_ref[off:off + tm, :] * cw_ref[j:j + 1, :]
    y = acc * _sigmoid(acc)
    nqk = LIN_HEADS * LIN_DK
    for idx, (o_ref, scale) in enumerate(((q_ref, LIN_DK ** -0.5), (k_ref, 1.0))):
        part = y[:, idx * nqk:(idx + 1) * nqk]
        ssq = _dot_exact(part * part, gsum_ref[...])
        o_ref[...] = part * lax.rsqrt(ssq + EPS) * scale
    v_ref[...] = y[:, 2 * nqk:]
    sm = sm_ref[...]
    beta = _sigmoid(sm)
    g = -jnp.exp(alog_ref[...]) * _softplus(sm + dtb_ref[...])
    lane = lax.broadcasted_iota(jnp.int32, sm.shape, 1) - GATE_LANE0
    is_beta = (lane >= 0) & (lane < 2 * LIN_HEADS)
    is_g = (lane >= 2 * LIN_HEADS) & (lane < 4 * LIN_HEADS)
    g_ref[...] = jnp.where(is_beta, beta, jnp.where(is_g, g, 0.0))


def gdn_prep(dqkv, small, conv_w, gsum, alog, dtb, batch, seq, tm):
    nt = seq // tm
    t, c = dqkv.shape
    hb = tm // SUBLANE
    nqk = LIN_HEADS * LIN_DK
    row = lambda b, i: (b * nt + i, 0)
    prev = lambda b, i: (jnp.maximum((b * nt + i) * hb - 1, 0), 0)
    nxt = lambda b, i: (jnp.minimum((b * nt + i + 1) * hb, t // SUBLANE - 1), 0)
    return pl.pallas_call(
        functools.partial(_gdn_prep_kernel, nt=nt),
        grid=(batch, nt),
        in_specs=[pl.BlockSpec((SUBLANE, c), prev), pl.BlockSpec((tm, c), row), pl.BlockSpec((SUBLANE, c), nxt),
                  pl.BlockSpec((tm, LANE), row),
                  _resident(conv_w.shape), _resident(gsum.shape), _resident((1, LANE)), _resident((1, LANE))],
        out_specs=[pl.BlockSpec((tm, nqk), row), pl.BlockSpec((tm, nqk), row),
                   pl.BlockSpec((tm, c - 2 * nqk), row), pl.BlockSpec((tm, LANE), row)],
        out_shape=[jax.ShapeDtypeStruct((t, nqk), F32), jax.ShapeDtypeStruct((t, nqk), F32),
                   jax.ShapeDtypeStruct((t, c - 2 * nqk), F32), jax.ShapeDtypeStruct((t, LANE), F32)],
        scratch_shapes=[pltpu.VMEM((tm + 2 * SUBLANE, c), F32)],
        compiler_params=_cparams("parallel", "parallel"),
        name="gdn_prep",
    )(dqkv, dqkv, dqkv, small, conv_w, gsum, alog, dtb)


def _gdn_kernel(q_ref, k_ref, v_ref, g_ref, o_ref, st_ref, *, ncb):
    d = pl.program_id(1)

    @pl.when(pl.program_id(2) == 0)
    def _():
        st_ref[...] = jnp.zeros_like(st_ref)

    same, incl, strict, cum = _direction_masks(d)
    r = lax.broadcasted_iota(jnp.int32, (STACK, STACK), 0)
    c = lax.broadcasted_iota(jnp.int32, (STACK, STACK), 1)
    eye = (r == c).astype(F32)
    pick = (lax.broadcasted_iota(jnp.int32, (LANE, STACK), 0)
            == (lax.broadcasted_iota(jnp.int32, (LANE, STACK), 1) // CHUNK) + LIN_HEADS)
    tt = lax.broadcasted_iota(jnp.int32, (CHUNK, STACK), 0)
    ss = lax.broadcasted_iota(jnp.int32, (CHUNK, STACK), 1) & (CHUNK - 1)
    cum_t = (((ss - tt) * (1 - 2 * d)) >= 0).astype(F32)
    ones_t = jnp.ones((CHUNK, STACK), F32)

    def rows_of(x, lane0):
        return jnp.concatenate(
            [jnp.broadcast_to(x[:, lane0 + h:lane0 + h + 1], (CHUNK, STACK)) for h in range(LIN_HEADS)], axis=0)

    def lane_row(x_t):
        return jnp.sum(jnp.where(pick, x_t, 0.0), axis=0, keepdims=True)

    def chunk(ci, carry):
        ce = ci + d * (ncb - 1 - 2 * ci)
        rows = pl.ds(pl.multiple_of(ce * CHUNK, CHUNK), CHUNK)
        gates = g_ref[rows, :]
        bsum = _dot_exact(cum, gates)
        total = jnp.sum(gates, axis=0, keepdims=True)
        b_col = rows_of(bsum, LIN_HEADS)
        beta_col = rows_of(gates, 0)
        bl_col = rows_of(jnp.broadcast_to(total, gates.shape), LIN_HEADS)
        b_row = lane_row(_dot_tn_exact(gates, cum_t))
        gamma = jnp.exp(jnp.where(incl, b_col - b_row, -jnp.inf))
        e_b = jnp.exp(b_col)
        k_bd = _block_diag(k_ref[rows, :], same)
        q_bd = _block_diag(q_ref[rows, :], same)
        kb_bd = k_bd * beta_col
        k16 = k_bd.astype(BF16)
        lower = jnp.where(strict, _dot_nt(kb_bd.astype(BF16), k16) * gamma, 0.0)
        attn = (_dot_nt(q_bd.astype(BF16), k16) * gamma).astype(BF16)
        inv = eye - lower
        pw = lower.astype(BF16)
        for _ in range(5):
            pw_f = _dot(pw, pw)
            pw = pw_f.astype(BF16)
            inv = inv + _dot(inv.astype(BF16), pw)
        inv16 = inv.astype(BF16)
        v = _stack_heads(v_ref[rows, :], LIN_DV) * beta_col[:, :LIN_DV]
        rhs = jnp.concatenate([v, kb_bd * e_b], axis=1)
        x16 = _dot(inv16, rhs.astype(BF16)).astype(BF16)
        l_hi = lower.astype(BF16)
        l_lo = (lower - l_hi.astype(F32)).astype(BF16)
        x_hi = x16.astype(F32)
        resid = rhs - (x_hi + _dot(l_hi, x16) + _dot(l_lo, x16))
        sol = x_hi + _dot(inv16, resid.astype(BF16))
        u = sol[:, :LIN_DV]
        w = sol[:, LIN_DV:].astype(BF16)
        q_dec = (q_bd * e_b).astype(BF16)
        k_dec = (k_bd * jnp.exp(bl_col - b_col)).astype(BF16)
        st = st_ref[...]
        st16 = st.astype(BF16)
        v_new = u - _dot_nt(w, st16)
        v_new16 = v_new.astype(BF16)
        o = _dot_nt(q_dec, st16) + _dot(attn, v_new16)
        dec_row = jnp.exp(lane_row(_dot_tn_exact(gates, ones_t)))
        st_ref[...] = dec_row * st + _dot_tn(v_new16, k_dec)
        for h in range(LIN_HEADS):
            o_ref[rows, h * LIN_DV:(h + 1) * LIN_DV] = o[h * CHUNK:(h + 1) * CHUNK, :]
        return carry

    lax.fori_loop(0, ncb, chunk, 0)


def gdn_scan(q, k, v, gates, batch, seq, tc):
    nblk = seq // tc
    t = batch * seq
    row = lambda b, d, n: (_scan_rows(b, n, d, nblk), 0)
    hk, hv = q.shape[1], v.shape[1]
    return pl.pallas_call(
        functools.partial(_gdn_kernel, ncb=tc // CHUNK),
        grid=(batch, 2, nblk),
        in_specs=[pl.BlockSpec((tc, hk), row), pl.BlockSpec((tc, hk), row), pl.BlockSpec((tc, hv), row),
                  pl.BlockSpec((None, tc, LANE), lambda b, d, n: (d, _scan_rows(b, n, d, nblk), 0))],
        out_specs=pl.BlockSpec((None, tc, hv), lambda b, d, n: (d, _scan_rows(b, n, d, nblk), 0)),
        out_shape=jax.ShapeDtypeStruct((2, t, hv), F32),
        scratch_shapes=[pltpu.VMEM((LIN_DV, hk), F32)],
        compiler_params=_cparams("parallel", "parallel", "arbitrary"),
        name="gdn_scan",
    )(q, k, v, gates)


def _lin_out_kernel(x_ref, oc_ref, od_ref, gc_ref, gd_ref, cw_ref, dw_ref, w_ref, o_ref):
    def gated(o_ref2, gate_ref, nw_ref):
        o = o_ref2[0] + o_ref2[1]
        parts = []
        for h in range(LIN_HEADS):
            cols = slice(h * LIN_DV, (h + 1) * LIN_DV)
            gate = gate_ref[:, cols]
            parts.append((_rms(o[:, cols], nw_ref[...]) * (gate * _sigmoid(gate))).astype(BF16))
        return jnp.concatenate(parts, axis=1)

    kc = oc_ref.shape[2]
    o_ref[...] = (x_ref[...] + _dot(gated(oc_ref, gc_ref, cw_ref), w_ref[:kc, :])
                  + _dot(gated(od_ref, gd_ref, dw_ref), w_ref[kc:, :]))


def lin_out(x, oc, od, gc, gd, cw, dw, w, tm):
    t, dm = x.shape
    row = lambda i: (i, 0)
    row3 = lambda i: (0, i, 0)
    wc, wd = oc.shape[2], od.shape[2]
    return pl.pallas_call(
        _lin_out_kernel,
        grid=(t // tm,),
        in_specs=[pl.BlockSpec((tm, dm), row), pl.BlockSpec((2, tm, wc), row3), pl.BlockSpec((2, tm, wd), row3),
                  pl.BlockSpec((tm, wc), row), pl.BlockSpec((tm, wd), row),
                  _resident((1, LIN_DV)), _resident((1, LIN_DV)), _resident(w.shape)],
        out_specs=pl.BlockSpec((tm, dm), row),
        out_shape=jax.ShapeDtypeStruct((t, dm), F32),
        compiler_params=_cparams("parallel"),
        name="lin_out",
    )(x, oc, od, gc, gd, cw.reshape(1, LIN_DV), dw.reshape(1, LIN_DV), w)


def _rot_half_cols(w):
    half = w.shape[-1] // 2
    return jnp.concatenate([-w[..., half:], w[..., :half]], axis=-1)


def _pad_heads(w, heads, dim, offset=0):
    k = w.shape[0]
    out = jnp.zeros((k, heads, LANE), w.dtype)
    out = out.at[:, :, offset:offset + dim].set(w.reshape(k, heads, dim))
    return out.reshape(k, heads * LANE)


def _layer0_in_weight(w_in):
    aq = A_HEADS * HEAD_DIM
    akv = A_KV_HEADS * HEAD_DIM
    o = 0
    w_q = w_in[:, o:o + aq]; o += aq
    w_k = w_in[:, o:o + akv]; o += akv
    w_v = w_in[:, o:o + akv]; o += akv
    w_cq = w_in[:, o:o + B_Q_RANK]; o += B_Q_RANK
    w_ckv = w_in[:, o:o + B_KV_RANK]; o += B_KV_RANK
    w_pe = w_in[:, o:o + B_ROPE]
    zeros = jnp.zeros((w_in.shape[0], B_NOPE), w_in.dtype)
    v_lo = _pad_heads(w_v, A_KV_HEADS, HEAD_DIM, 0).reshape(-1, A_KV_HEADS, LANE)
    v_hi = _pad_heads(w_v, A_KV_HEADS, HEAD_DIM, HEAD_DIM).reshape(-1, A_KV_HEADS, LANE)
    w_v2 = jnp.stack([v_lo, v_hi], axis=2).reshape(w_in.shape[0], 2 * A_KV_HEADS * LANE)
    w_pe_rot = _rot_half_cols(w_pe)
    pieces = [_pad_heads(w_q, A_HEADS, HEAD_DIM), _pad_heads(w_k, A_KV_HEADS, HEAD_DIM), w_v2,
              w_cq, w_ckv,
              jnp.concatenate([zeros, w_pe, w_pe, zeros, w_pe_rot, w_pe_rot], axis=1)]
    segs, start = [], 0
    for p in pieces:
        segs.append((start, p.shape[1]))
        start += p.shape[1]
    return jnp.concatenate(pieces, axis=1).astype(BF16), segs


def _mla_weights(w_uq, w_ukv):
    r = w_uq.shape[0]
    wq = w_uq.reshape(r, B_HEADS, B_NOPE + B_ROPE)
    pe = wq[:, :, B_NOPE:]
    wq = jnp.concatenate([wq[:, :, :B_NOPE], pe, _rot_half_cols(pe)], axis=-1).reshape(r, B_HEADS * LANE)
    rk = w_ukv.shape[0]
    wkv = w_ukv.reshape(rk, B_HEADS, B_NOPE + B_VDIM)
    wk = _pad_heads(wkv[:, :, :B_NOPE].reshape(rk, B_HEADS * B_NOPE), B_HEADS, B_NOPE)
    wv = wkv[:, :, B_NOPE:].reshape(rk, B_HEADS * B_VDIM)
    return wq.astype(BF16), wk.astype(BF16), wv.astype(BF16)


def _rope_tables(seq):
    half = B_ROPE // 2
    inv = ROPE_THETA ** (-jnp.arange(half, dtype=F32) / half)
    ang = jnp.arange(seq, dtype=F32)[:, None] * inv[None, :]
    cos2 = jnp.tile(jnp.cos(ang), (1, 4))
    sin2 = jnp.tile(jnp.sin(ang), (1, 4))
    zeros = jnp.zeros((seq, B_NOPE), F32)
    qscale = (B_NOPE + B_ROPE) ** -0.5 * math.log2(math.e)
    q_tab = jnp.concatenate([jnp.ones((seq, B_NOPE), F32), cos2[:, :B_ROPE], sin2[:, :B_ROPE]], axis=1) * qscale
    k_cos = jnp.concatenate([zeros, cos2], axis=1)
    k_sin = jnp.concatenate([zeros, sin2], axis=1)
    return q_tab, k_cos, k_sin


def _layer1_in_weight(w_in):
    hk = LIN_HEADS * LIN_DK
    hv = LIN_HEADS * LIN_DV
    conv_ch = 2 * hk + hv
    main = 2 * hk + 2 * hv + 2 * GATE_RANK
    o = 0
    w_main = w_in[:, :2 * hk + 2 * hv]; o = 2 * hk + 2 * hv
    w_gl = w_in[:, o:o + 2 * GATE_RANK]; o += 2 * GATE_RANK
    w_conv = w_in[:, o:o + conv_ch]; o += conv_ch
    w_z = w_in[:, o:o + hv]; o += hv
    w_small = w_in[:, o:o + 4 * LIN_HEADS]
    del main
    pad = jnp.zeros((w_in.shape[0], LANE - 2 * GATE_RANK - 4 * LIN_HEADS), w_in.dtype)
    w = jnp.concatenate([w_main, w_conv, w_z, w_gl, w_small, pad], axis=1).astype(BF16)
    widths = [hk, hk, hv, hv, conv_ch, hv, LANE]
    segs, start = [], 0
    for wd in widths:
        segs.append((start, wd))
        start += wd
    return w, segs


def _pick_tile(n, pref):
    t = min(n, pref)
    while n % t:
        t //= 2
    return t


def kernel(x, att_norm, att_w_in, att_sink, mla_q_norm, mla_w_uq, mla_kv_norm, mla_w_ukv, att_w_out, lin_norm, lin_w_in, gla_w_gate_f, gla_b_gate_f, gla_w_gate_b, gla_b_gate_b, gla_norm, gdn_conv, gdn_a_log_f, gdn_dt_bias_f, gdn_a_log_b, gdn_dt_bias_b, gdn_norm, lin_w_out, mlp_norm, mlp_w1, mlp_w2, final_norm):
    batch, seq, dm = x.shape
    t = batch * seq
    depth = mlp_norm.shape[0]
    tm = _pick_tile(seq, 512)
    xs = x.reshape(t, dm)
    for layer in range(depth):
        i = layer // 2
        if layer % 2 == 0:
            w0, segs = _layer0_in_weight(att_w_in[i])
            qa, ka, va, cq, ckv, kab = norm_proj(xs, att_norm[i], w0, segs, [BF16, BF16, BF16, F32, F32, F32], tm)
            o_a = window_attention(qa, ka, va, att_sink[i].astype(F32), batch, seq)
            wq, wk, wv = _mla_weights(mla_w_uq[i], mla_w_ukv[i])
            q_tab, k_cos, k_sin = _rope_tables(seq)
            qb = mla_q_prep(cq, mla_q_norm[i], wq, q_tab, batch, seq, tm)
            kb, vb = mla_kv_prep(ckv, kab, mla_kv_norm[i], wk, wv, k_cos, k_sin, batch, seq, tm)
            o_b = mla_flash(qb, kb, vb, batch, seq, _pick_tile(seq, 512), _pick_tile(seq, 512))
            xs = out_proj(xs, o_a, o_b, att_w_out[i].astype(BF16), tm)
        else:
            w1, segs = _layer1_in_weight(lin_w_in[i])
            cq, ck, cv, cg, dqkv, dz, small = norm_proj(xs, lin_norm[i], w1, segs, [F32] * 7, _pick_tile(seq, 256))
            hk = LIN_HEADS * LIN_DK
            wg = jnp.zeros((2, LANE, hk), F32)
            wg = wg.at[0, :GATE_RANK].set(gla_w_gate_f[i]).at[1, GATE_RANK:2 * GATE_RANK].set(gla_w_gate_b[i])
            bg = jnp.stack([gla_b_gate_f[i], gla_b_gate_b[i]]).reshape(2, 1, hk).astype(F32)
            tc = _pick_tile(seq, 256)
            o_c = gla_scan(cq, ck, cv, small, wg.astype(BF16), bg, batch, seq, tc)
            gsum = (jnp.arange(hk)[:, None] // LIN_DK == jnp.arange(hk)[None, :] // LIN_DK).astype(F32)
            nh = LIN_HEADS
            a0 = GATE_LANE0 + 2 * nh
            alog = jnp.zeros((LANE,), F32).at[a0:a0 + nh].set(gdn_a_log_f[i]).at[a0 + nh:a0 + 2 * nh].set(gdn_a_log_b[i])
            dtb = jnp.zeros((LANE,), F32).at[a0:a0 + nh].set(gdn_dt_bias_f[i]).at[a0 + nh:a0 + 2 * nh].set(gdn_dt_bias_b[i])
            dq, dk, dv, mixed = gdn_prep(dqkv, small, gdn_conv[i].astype(F32), gsum,
                                         alog.reshape(1, LANE), dtb.reshape(1, LANE), batch, seq, tm)
            g0 = GATE_LANE0
            zpad = jnp.zeros((t, LANE - 2 * nh), F32)
            gates = jnp.stack([
                jnp.concatenate([mixed[:, g0:g0 + nh], mixed[:, g0 + 2 * nh:g0 + 3 * nh], zpad], axis=1),
                jnp.concatenate([mixed[:, g0 + nh:g0 + 2 * nh], mixed[:, g0 + 3 * nh:g0 + 4 * nh], zpad], axis=1)])
            o_d = gdn_scan(dq, dk, dv, gates, batch, seq, tc)
            xs = lin_out(xs, o_c, o_d, cg, dz, gla_norm[i], gdn_norm[i], lin_w_out[i].astype(BF16), tm)
        last = layer == depth - 1
        xs = mlp(xs, mlp_norm[layer], mlp_w1[layer].astype(BF16), mlp_w2[layer].astype(BF16),
                 final_norm, last, tm, 1024)
    return xs.reshape(batch, seq, dm)
```

```python
import functools
import math

import numpy as np
import jax
import jax.numpy as jnp
from jax import lax
from jax.experimental import pallas as pl
from jax.experimental.pallas import tpu as pltpu

F32 = jnp.float32
BF16 = jnp.bfloat16
EPS = 1e-6

LANE = 128
SUBLANE = 8
VMEM_LIMIT_BYTES = 56 * 1024 * 1024

HEAD_DIM = 64
A_HEADS = 8
A_KV_HEADS = 2
A_GROUP = A_HEADS // A_KV_HEADS
WINDOW = 128
BLOCK = 128
B_HEADS = 8
B_NOPE = 64
B_ROPE = 32
B_VDIM = 64
B_Q_RANK = 384
B_KV_RANK = 256
ROPE_THETA = 10000.0
LIN_HEADS = 4
LIN_DK = 64
LIN_DV = 128
GATE_RANK = 16
GATE_NORM = 16.0
CHUNK = 64
CONV_TAPS = 5
STACK = LIN_HEADS * CHUNK
GATE_LANE0 = 2 * GATE_RANK

NT_DIMS = (((1,), (1,)), ((), ()))
TN_DIMS = (((0,), (0,)), ((), ()))


def _cparams(*sem):
    return pltpu.CompilerParams(dimension_semantics=sem, vmem_limit_bytes=VMEM_LIMIT_BYTES)


def _resident(shape):
    nd = len(shape)
    return pl.BlockSpec(shape, lambda *_: (0,) * nd, pipeline_mode=pl.Buffered(1))


def _rms(x, w):
    return x * lax.rsqrt(jnp.mean(x * x, axis=-1, keepdims=True) + EPS) * w


def _sigmoid(x):
    return 1.0 / (1.0 + jnp.exp(-x))


def _softplus(x):
    return jnp.maximum(x, 0.0) + jnp.log1p(jnp.exp(-jnp.abs(x)))


def _dot(a, b):
    return jnp.dot(a, b, preferred_element_type=F32)


def _dot_nt(a, b):
    return lax.dot_general(a, b, NT_DIMS, preferred_element_type=F32)


def _dot_tn(a, b):
    return lax.dot_general(a, b, TN_DIMS, preferred_element_type=F32)


def _dot_exact(a, b):
    return jnp.dot(a, b, preferred_element_type=F32, precision=lax.Precision.HIGHEST)


def _dot_tn_exact(a, b):
    return lax.dot_general(a, b, TN_DIMS, preferred_element_type=F32, precision=lax.Precision.HIGHEST)


def _norm_proj_kernel(x_ref, nw_ref, w_ref, *out_refs, segs):
    xn = _rms(x_ref[...], nw_ref[...]).astype(BF16)
    for (start, width), o_ref in zip(segs, out_refs):
        o_ref[...] = _dot(xn, w_ref[:, start:start + width]).astype(o_ref.dtype)


def norm_proj(x, norm_w, w, segs, dtypes, tm):
    t, k = x.shape
    n = w.shape[1]
    return pl.pallas_call(
        functools.partial(_norm_proj_kernel, segs=tuple(segs)),
        grid=(t // tm,),
        in_specs=[pl.BlockSpec((tm, k), lambda i: (i, 0)),
                  _resident((1, k)),
                  _resident((k, n))],
        out_specs=[pl.BlockSpec((tm, wd), lambda i: (i, 0)) for _, wd in segs],
        out_shape=[jax.ShapeDtypeStruct((t, wd), dt) for (_, wd), dt in zip(segs, dtypes)],
        compiler_params=_cparams("parallel"),
        name="norm_proj",
    )(x, norm_w.reshape(1, k), w)


def _win_attn_kernel(sink_ref, q_ref, kp_ref, kc_ref, kn_ref, vp_ref, vc_ref, vn_ref, o_ref, *, seq):
    n = pl.program_id(1)
    k3 = jnp.concatenate([kp_ref[...], kc_ref[...], kn_ref[...]], axis=0)
    v3 = jnp.concatenate([vp_ref[...], vc_ref[...], vn_ref[...]], axis=0)
    qi = lax.broadcasted_iota(jnp.int32, (BLOCK, 3 * BLOCK), 0)
    kj = lax.broadcasted_iota(jnp.int32, (BLOCK, 3 * BLOCK), 1)
    dist = jnp.abs(qi + BLOCK - kj)
    kpos = n * BLOCK + kj - BLOCK
    valid = (dist <= WINDOW) & (kpos >= 0) & (kpos < seq)
    distf = dist.astype(F32)
    scale = HEAD_DIM ** -0.5
    for pair in range(A_HEADS // 2):
        acc = jnp.zeros((BLOCK, LANE), F32)
        for e in range(2):
            h = 2 * pair + e
            g = h // A_GROUP
            slope = 2.0 ** (-8.0 * (h + 1) / A_HEADS)
            q = q_ref[:, h * LANE:(h + 1) * LANE]
            k = k3[:, g * LANE:(g + 1) * LANE]
            s = _dot_nt(q, k) * scale - slope * distf
            s = jnp.where(valid, s, -jnp.inf)
            sink = sink_ref[h]
            m = jnp.maximum(jnp.max(s, axis=-1, keepdims=True), sink)
            p = jnp.exp(s - m)
            den = jnp.sum(p, axis=-1, keepdims=True) + jnp.exp(sink - m)
            p = (p / den).astype(BF16)
            acc = acc + _dot(p, v3[:, (2 * g + e) * LANE:(2 * g + e + 1) * LANE])
        o_ref[:, pair * LANE:(pair + 1) * LANE] = acc.astype(o_ref.dtype)


def window_attention(q, k, v, sink, batch, seq):
    nb = seq // BLOCK
    cur = lambda b, n: (b * nb + n, 0)
    prev = lambda b, n: (b * nb + jnp.maximum(n - 1, 0), 0)
    nxt = lambda b, n: (b * nb + jnp.minimum(n + 1, nb - 1), 0)
    kw, vw = k.shape[1], v.shape[1]
    return pl.pallas_call(
        functools.partial(_win_attn_kernel, seq=seq),
        grid=(batch, nb),
        in_specs=[pl.BlockSpec(memory_space=pltpu.SMEM),
                  pl.BlockSpec((BLOCK, q.shape[1]), cur),
                  pl.BlockSpec((BLOCK, kw), prev), pl.BlockSpec((BLOCK, kw), cur), pl.BlockSpec((BLOCK, kw), nxt),
                  pl.BlockSpec((BLOCK, vw), prev), pl.BlockSpec((BLOCK, vw), cur), pl.BlockSpec((BLOCK, vw), nxt)],
        out_specs=pl.BlockSpec((BLOCK, A_HEADS * HEAD_DIM), cur),
        out_shape=jax.ShapeDtypeStruct((batch * seq, A_HEADS * HEAD_DIM), BF16),
        compiler_params=_cparams("parallel", "parallel"),
        name="window_attention",
    )(sink, q, k, k, k, v, v, v)


def _mla_q_kernel(cq_ref, nw_ref, w_ref, tab_ref, o_ref):
    xn = _rms(cq_ref[...], nw_ref[...]).astype(BF16)
    tab = tab_ref[...]
    for h in range(B_HEADS):
        cols = slice(h * LANE, (h + 1) * LANE)
        o_ref[:, cols] = (_dot(xn, w_ref[:, cols]) * tab).astype(o_ref.dtype)


def mla_q_prep(cq, norm_w, w, tab, batch, seq, tm):
    nt = seq // tm
    r = cq.shape[1]
    n = w.shape[1]
    return pl.pallas_call(
        _mla_q_kernel,
        grid=(batch, nt),
        in_specs=[pl.BlockSpec((tm, r), lambda b, i: (b * nt + i, 0)),
                  _resident((1, r)), _resident((r, n)),
                  pl.BlockSpec((tm, LANE), lambda b, i: (i, 0))],
        out_specs=pl.BlockSpec((tm, n), lambda b, i: (b * nt + i, 0)),
        out_shape=jax.ShapeDtypeStruct((batch * seq, n), BF16),
        compiler_params=_cparams("parallel", "parallel"),
        name="mla_q_prep",
    )(cq, norm_w.reshape(1, r), w, tab)


def _mla_kv_kernel(ckv_ref, kab_ref, nw_ref, wk_ref, wv_ref, ctab_ref, stab_ref, k_ref, v_ref):
    xn = _rms(ckv_ref[...], nw_ref[...]).astype(BF16)
    kr = kab_ref[:, :LANE] * ctab_ref[...] + kab_ref[:, LANE:] * stab_ref[...]
    for h in range(B_HEADS):
        cols = slice(h * LANE, (h + 1) * LANE)
        k_ref[:, cols] = (_dot(xn, wk_ref[:, cols]) + kr).astype(k_ref.dtype)
    lane = lax.broadcasted_iota(jnp.int32, v_ref.shape, 1) % LANE
    v_ref[...] = jnp.where(lane == B_VDIM, 1.0, _dot(xn, wv_ref[...])).astype(v_ref.dtype)


def mla_kv_prep(ckv, kab, norm_w, wk, wv, ctab, stab, batch, seq, tm):
    nt = seq // tm
    r = ckv.shape[1]
    row = lambda b, i: (b * nt + i, 0)
    pos = lambda b, i: (i, 0)
    return pl.pallas_call(
        _mla_kv_kernel,
        grid=(batch, nt),
        in_specs=[pl.BlockSpec((tm, r), row), pl.BlockSpec((tm, 2 * LANE), row),
                  _resident((1, r)), _resident(wk.shape), _resident(wv.shape),
                  pl.BlockSpec((tm, LANE), pos), pl.BlockSpec((tm, LANE), pos)],
        out_specs=[pl.BlockSpec((tm, wk.shape[1]), row), pl.BlockSpec((tm, wv.shape[1]), row)],
        out_shape=[jax.ShapeDtypeStruct((batch * seq, wk.shape[1]), BF16),
                   jax.ShapeDtypeStruct((batch * seq, wv.shape[1]), BF16)],
        compiler_params=_cparams("parallel", "parallel"),
        name="mla_kv_prep",
    )(ckv, kab, norm_w.reshape(1, r), wk, wv, ctab, stab)


def _mla_flash_kernel(q_ref, k_ref, v_ref, o_ref, s_ref, p_ref, m_ref, a_ref, acc_ref, *, tk, nk, rb):
    tq = q_ref.shape[0]
    outs = []
    for e in range(2):
        cols = slice(e * LANE, (e + 1) * LANE)
        m_ref[...] = jnp.full(m_ref.shape, -jnp.inf, F32)
        acc_ref[...] = jnp.zeros(acc_ref.shape, F32)

        def body(j, carry, cols=cols):
            rows = pl.ds(pl.multiple_of(j * tk, tk), tk)
            s_ref[...] = _dot_nt(q_ref[:, cols], k_ref[rows, cols])
            for r in range(tq // rb):
                rs = slice(r * rb, (r + 1) * rb)
                s = s_ref[rs, :]
                m_old = m_ref[rs, :]
                m_new = jnp.maximum(m_old, jnp.max(s, axis=-1, keepdims=True))
                m_ref[rs, :] = m_new
                a_ref[rs, :] = jnp.exp2(m_old - m_new)
                p_ref[rs, :] = jnp.exp2(s - jnp.concatenate([m_new] * (tk // LANE), axis=1)).astype(BF16)
            acc_ref[...] = a_ref[...] * acc_ref[...] + _dot(p_ref[...], v_ref[rows, cols])
            return carry

        lax.fori_loop(0, nk, body, 0)
        acc = acc_ref[...]
        outs.append(acc / acc[:, B_VDIM:B_VDIM + 1])
    lane = lax.broadcasted_iota(jnp.int32, (tq, LANE), 1)
    o_ref[...] = jnp.where(lane < B_VDIM, outs[0], pltpu.roll(outs[1], B_VDIM, axis=1)).astype(o_ref.dtype)


def mla_flash(q, k, v, batch, seq, tq, tk, rb):
    nq = seq // tq
    npair = B_HEADS // 2
    return pl.pallas_call(
        functools.partial(_mla_flash_kernel, tk=tk, nk=seq // tk, rb=rb),
        grid=(batch, npair, nq),
        in_specs=[pl.BlockSpec((tq, 2 * LANE), lambda b, p, i: (b * nq + i, p)),
                  pl.BlockSpec((seq, 2 * LANE), lambda b, p, i: (b, p)),
                  pl.BlockSpec((seq, 2 * LANE), lambda b, p, i: (b, p))],
        out_specs=pl.BlockSpec((tq, LANE), lambda b, p, i: (b * nq + i, p)),
        out_shape=jax.ShapeDtypeStruct((batch * seq, B_HEADS * B_VDIM), BF16),
        scratch_shapes=[pltpu.VMEM((tq, tk), F32), pltpu.VMEM((tq, tk), BF16), pltpu.VMEM((tq, LANE), F32),
                        pltpu.VMEM((tq, LANE), F32), pltpu.VMEM((tq, LANE), F32)],
        compiler_params=_cparams("parallel", "parallel", "parallel"),
        name="mla_flash",
    )(q, k, v)


def _out_proj_kernel(x_ref, a_ref, b_ref, w_ref, o_ref):
    ka = a_ref.shape[1]
    o_ref[...] = x_ref[...] + _dot(a_ref[...], w_ref[:ka, :]) + _dot(b_ref[...], w_ref[ka:, :])


def out_proj(x, a, b, w, tm):
    t, d = x.shape
    row = lambda i: (i, 0)
    return pl.pallas_call(
        _out_proj_kernel,
        grid=(t // tm,),
        in_specs=[pl.BlockSpec((tm, d), row), pl.BlockSpec((tm, a.shape[1]), row),
                  pl.BlockSpec((tm, b.shape[1]), row), _resident(w.shape)],
        out_specs=pl.BlockSpec((tm, d), row),
        out_shape=jax.ShapeDtypeStruct((t, d), F32),
        compiler_params=_cparams("parallel"),
        name="out_proj",
    )(x, a, b, w)


def _mlp_kernel(x_ref, nw_ref, w1_ref, w2_ref, fw_ref, o_ref, *, fchunk, final_norm):
    x = x_ref[...]
    xn = _rms(x, nw_ref[...]).astype(BF16)
    acc = x
    for f in range(w1_ref.shape[1] // fchunk):
        cols = slice(f * fchunk, (f + 1) * fchunk)
        h = jnp.square(jnp.maximum(_dot(xn, w1_ref[:, cols]), 0.0)).astype(BF16)
        acc = acc + _dot(h, w2_ref[cols, :])
    if final_norm:
        acc = _rms(acc, fw_ref[...])
    o_ref[...] = acc


def mlp(x, norm_w, w1, w2, final_w, final_norm, tm, fchunk):
    t, d = x.shape
    row = lambda i: (i, 0)
    return pl.pallas_call(
        functools.partial(_mlp_kernel, fchunk=fchunk, final_norm=final_norm),
        grid=(t // tm,),
        in_specs=[pl.BlockSpec((tm, d), row), _resident((1, d)), _resident(w1.shape),
                  _resident(w2.shape), _resident((1, d))],
        out_specs=pl.BlockSpec((tm, d), row),
        out_shape=jax.ShapeDtypeStruct((t, d), F32),
        compiler_params=_cparams("parallel"),
        name="mlp",
    )(x, norm_w.reshape(1, d), w1, w2, final_w.reshape(1, d))


def _direction_masks(d):
    sgn = 1 - 2 * d
    r = lax.broadcasted_iota(jnp.int32, (STACK, STACK), 0)
    c = lax.broadcasted_iota(jnp.int32, (STACK, STACK), 1)
    same = (r & -CHUNK) == (c & -CHUNK)
    order = ((r & (CHUNK - 1)) - (c & (CHUNK - 1))) * sgn
    i = lax.broadcasted_iota(jnp.int32, (CHUNK, CHUNK), 0)
    j = lax.broadcasted_iota(jnp.int32, (CHUNK, CHUNK), 1)
    cum = (((i - j) * sgn) >= 0).astype(F32)
    return same, same & (order >= 0), same & (order > 0), cum


def _block_diag(x, same):
    return jnp.where(same, jnp.concatenate([x] * LIN_HEADS, axis=0), 0.0)


def _stack_heads(v, width):
    return jnp.concatenate([v[:, h * width:(h + 1) * width] for h in range(LIN_HEADS)], axis=0)


def _scan_rows(b, n, d, nblk):
    return b * nblk + n + d * (nblk - 1 - 2 * n)


def _gla_kernel(q_ref, k_ref, v_ref, sm_ref, wg_ref, bg_ref, o_ref, st_ref, oi_ref, qi_ref, ds_ref, dec_ref, *, ncb):
    d = pl.program_id(1)

    @pl.when(pl.program_id(2) == 0)
    def _():
        st_ref[...] = jnp.zeros_like(st_ref)

    same, causal, _, cum = _direction_masks(d)

    cs = range(ncb)

    def each(f, *lists):
        return [f(*args) for args in zip(*lists)]

    z = [_dot(sm_ref[ci * CHUNK:(ci + 1) * CHUNK, :].astype(BF16), wg_ref[...]) + bg_ref[...] for ci in cs]
    log_a = each(lambda a: (jnp.minimum(a, 0.0) - jnp.log1p(jnp.exp(-jnp.abs(a)))) * (1.0 / GATE_NORM), z)
    b = each(lambda a: _dot_exact(cum, a), log_a)
    b_last = each(lambda a: jnp.sum(a, axis=0, keepdims=True), log_a)
    q_in = [_block_diag(q_ref[ci * CHUNK:(ci + 1) * CHUNK, :] * (LIN_DK ** -0.5) * jnp.exp(b[ci]), same).astype(BF16)
            for ci in cs]
    k = [k_ref[ci * CHUNK:(ci + 1) * CHUNK, :] for ci in cs]
    k_in = each(lambda a, bb: _block_diag(a * jnp.exp(-bb), same).astype(BF16), k, b)
    k_out = each(lambda a, bb, bl: _block_diag(a * jnp.exp(bl - bb), same).astype(BF16), k, b, b_last)
    v = [_stack_heads(v_ref[ci * CHUNK:(ci + 1) * CHUNK, :], LIN_DV).astype(BF16) for ci in cs]
    sc = each(lambda a, bb: jnp.where(causal, _dot_nt(a, bb), 0.0).astype(BF16), q_in, k_in)
    o_intra = each(_dot, sc, v)
    ds = each(_dot_tn, v, k_out)
    for ci in cs:
        oi_ref[ci] = o_intra[ci]
        qi_ref[ci] = q_in[ci]
        ds_ref[ci] = ds[ci]
        dec_ref[ci] = jnp.broadcast_to(jnp.exp(b_last[ci]), (SUBLANE, STACK))

    def chunk(ci, carry):
        ce = ci + d * (ncb - 1 - 2 * ci)
        rows = pl.ds(pl.multiple_of(ce * CHUNK, CHUNK), CHUNK)
        st = st_ref[...]
        o = oi_ref[ce] + _dot_nt(qi_ref[ce], st.astype(BF16))
        st_ref[...] = dec_ref[ce][:1, :] * st + ds_ref[ce]
        for h in range(LIN_HEADS):
            o_ref[rows, h * LIN_DV:(h + 1) * LIN_DV] = o[h * CHUNK:(h + 1) * CHUNK, :]
        return carry

    lax.fori_loop(0, ncb, chunk, 0, unroll=True)


def gla_scan(q, k, v, small, wg, bg, batch, seq, tc):
    nblk = seq // tc
    t = batch * seq
    row = lambda b, d, n: (_scan_rows(b, n, d, nblk), 0)
    dsel = lambda b, d, n: (d, 0, 0)
    hk, hv = q.shape[1], v.shape[1]
    ncb = tc // CHUNK
    return pl.pallas_call(
        functools.partial(_gla_kernel, ncb=ncb),
        grid=(batch, 2, nblk),
        in_specs=[pl.BlockSpec((tc, hk), row), pl.BlockSpec((tc, hk), row), pl.BlockSpec((tc, hv), row),
                  pl.BlockSpec((tc, LANE), row),
                  pl.BlockSpec((None, LANE, hk), dsel), pl.BlockSpec((None, 1, hk), dsel)],
        out_specs=pl.BlockSpec((None, tc, hv), lambda b, d, n: (d, _scan_rows(b, n, d, nblk), 0)),
        out_shape=jax.ShapeDtypeStruct((2, t, hv), F32),
        scratch_shapes=[pltpu.VMEM((LIN_DV, hk), F32),
                        pltpu.VMEM((ncb, STACK, LIN_DV), F32), pltpu.VMEM((ncb, STACK, hk), BF16),
                        pltpu.VMEM((ncb, LIN_DV, hk), F32), pltpu.VMEM((ncb, SUBLANE, hk), F32)],
        compiler_params=_cparams("parallel", "parallel", "arbitrary"),
        name="gla_scan",
    )(q, k, v, small, wg, bg)


def _gdn_prep_kernel(xp_ref, xc_ref, xn_ref, sm_ref, cw_ref, gsum_ref, alog_ref, dtb_ref,
                     q_ref, k_ref, v_ref, g_ref, buf_ref, *, nt):
    i = pl.program_id(1)
    tm = xc_ref.shape[0]
    halo = SUBLANE
    buf_ref[:halo, :] = xp_ref[...] * (i > 0).astype(F32)
    buf_ref[halo:halo + tm, :] = xc_ref[...]
    buf_ref[halo + tm:, :] = xn_ref[...] * (i < nt - 1).astype(F32)
    acc = jnp.zeros(xc_ref.shape, F32)
    for j in range(CONV_TAPS):
        off = halo + j - CONV_TAPS // 2
        acc = acc + buf_ref[off:off + tm, :] * cw_ref[j:j + 1, :]
    y = acc * _sigmoid(acc)
    nqk = LIN_HEADS * LIN_DK
    for idx, (o_ref, scale) in enumerate(((q_ref, LIN_DK ** -0.5), (k_ref, 1.0))):
        part = y[:, idx * nqk:(idx + 1) * nqk]
        ssq = _dot_exact(part * part, gsum_ref[...])
        o_ref[...] = part * lax.rsqrt(ssq + EPS) * scale
    v_ref[...] = y[:, 2 * nqk:]
    sm = sm_ref[...]
    beta = _sigmoid(sm)
    g = -jnp.exp(alog_ref[...]) * _softplus(sm + dtb_ref[...])
    lane = lax.broadcasted_iota(jnp.int32, sm.shape, 1) - GATE_LANE0
    is_beta = (lane >= 0) & (lane < 2 * LIN_HEADS)
    is_g = (lane >= 2 * LIN_HEADS) & (lane < 4 * LIN_HEADS)
    g_ref[...] = jnp.where(is_beta, beta, jnp.where(is_g, g, 0.0))


def gdn_prep(dqkv, small, conv_w, gsum, alog, dtb, batch, seq, tm):
    nt = seq // tm
    t, c = dqkv.shape
    hb = tm // SUBLANE
    nqk = LIN_HEADS * LIN_DK
    row = lambda b, i: (b * nt + i, 0)
    prev = lambda b, i: (jnp.maximum((b * nt + i) * hb - 1, 0), 0)
    nxt = lambda b, i: (jnp.minimum((b * nt + i + 1) * hb, t // SUBLANE - 1), 0)
    return pl.pallas_call(
        functools.partial(_gdn_prep_kernel, nt=nt),
        grid=(batch, nt),
        in_specs=[pl.BlockSpec((SUBLANE, c), prev), pl.BlockSpec((tm, c), row), pl.BlockSpec((SUBLANE, c), nxt),
                  pl.BlockSpec((tm, LANE), row),
                  _resident(conv_w.shape), _resident(gsum.shape), _resident((1, LANE)), _resident((1, LANE))],
        out_specs=[pl.BlockSpec((tm, nqk), row), pl.BlockSpec((tm, nqk), row),
                   pl.BlockSpec((tm, c - 2 * nqk), row), pl.BlockSpec((tm, LANE), row)],
        out_shape=[jax.ShapeDtypeStruct((t, nqk), F32), jax.ShapeDtypeStruct((t, nqk), F32),
                   jax.ShapeDtypeStruct((t, c - 2 * nqk), F32), jax.ShapeDtypeStruct((t, LANE), F32)],
        scratch_shapes=[pltpu.VMEM((tm + 2 * SUBLANE, c), F32)],
        compiler_params=_cparams("parallel", "parallel"),
        name="gdn_prep",
    )(dqkv, dqkv, dqkv, small, conv_w, gsum, alog, dtb)


def _gdn_kernel(q_ref, k_ref, v_ref, g_ref, o_ref, st_ref, u_ref, w_ref, at_ref, qd_ref, kd_ref, dec_ref, *, ncb):
    d = pl.program_id(1)

    @pl.when(pl.program_id(2) == 0)
    def _():
        st_ref[...] = jnp.zeros_like(st_ref)

    same, incl, strict, cum = _direction_masks(d)
    r = lax.broadcasted_iota(jnp.int32, (STACK, STACK), 0)
    c = lax.broadcasted_iota(jnp.int32, (STACK, STACK), 1)
    eye = (r == c).astype(F32)
    pick = (lax.broadcasted_iota(jnp.int32, (LANE, STACK), 0)
            == (lax.broadcasted_iota(jnp.int32, (LANE, STACK), 1) // CHUNK) + LIN_HEADS)
    tt = lax.broadcasted_iota(jnp.int32, (CHUNK, STACK), 0)
    ss = lax.broadcasted_iota(jnp.int32, (CHUNK, STACK), 1) & (CHUNK - 1)
    cum_t = (((ss - tt) * (1 - 2 * d)) >= 0).astype(F32)
    ones_t = jnp.ones((CHUNK, STACK), F32)

    def rows_of(x, lane0):
        return jnp.concatenate(
            [jnp.broadcast_to(x[:, lane0 + h:lane0 + h + 1], (CHUNK, STACK)) for h in range(LIN_HEADS)], axis=0)

    def lane_row(x_t):
        return jnp.sum(jnp.where(pick, x_t, 0.0), axis=0, keepdims=True)

    cs = range(ncb)

    def each(f, *lists):
        return [f(*args) for args in zip(*lists)]

    gates = [g_ref[ci * CHUNK:(ci + 1) * CHUNK, :] for ci in cs]
    bsum = each(lambda g: _dot_exact(cum, g), gates)
    b_t = each(lambda g: _dot_tn_exact(g, cum_t), gates)
    tot_t = each(lambda g: _dot_tn_exact(g, ones_t), gates)
    k_bd = [_block_diag(k_ref[ci * CHUNK:(ci + 1) * CHUNK, :], same) for ci in cs]
    q_bd = [_block_diag(q_ref[ci * CHUNK:(ci + 1) * CHUNK, :], same) for ci in cs]
    beta_col = each(lambda g: rows_of(g, 0), gates)
    kb_bd = each(lambda a, b: a * b, k_bd, beta_col)
    k16 = each(lambda a: a.astype(BF16), k_bd)
    kk = each(lambda a, b: _dot_nt(a.astype(BF16), b), kb_bd, k16)
    qk = each(lambda a, b: _dot_nt(a.astype(BF16), b), q_bd, k16)
    b_col = each(lambda x: rows_of(x, LIN_HEADS), bsum)
    bl_col = each(lambda g: rows_of(jnp.broadcast_to(jnp.sum(g, axis=0, keepdims=True), g.shape), LIN_HEADS), gates)
    gamma = each(lambda bc, bt: jnp.exp(jnp.where(incl, bc - lane_row(bt), -jnp.inf)), b_col, b_t)
    lower = each(lambda a, g: jnp.where(strict, a * g, 0.0), kk, gamma)
    attn = each(lambda a, g: (a * g).astype(BF16), qk, gamma)
    l_hi = each(lambda a: a.astype(BF16), lower)
    pw = [l_hi]
    for _ in range(5):
        pw.append(each(lambda a: _dot(a, a).astype(BF16), pw[-1]))

    def pair(a, b16):
        return a + _dot(a.astype(BF16), b16)

    f01 = each(lambda lo, p1: pair(eye - lo, p1), lower, pw[1])
    f23 = each(lambda p2, p3: pair(eye + p2.astype(F32), p3), pw[2], pw[3])
    f45 = each(lambda p4, p5: pair(eye + p4.astype(F32), p5), pw[4], pw[5])
    f0123 = each(lambda a, b: _dot(a.astype(BF16), b.astype(BF16)).astype(BF16), f01, f23)
    inv16 = each(lambda a, b: _dot(a, b.astype(BF16)).astype(BF16), f0123, f45)
    e_b = each(jnp.exp, b_col)
    vb = [_stack_heads(v_ref[ci * CHUNK:(ci + 1) * CHUNK, :], LIN_DV) * beta_col[ci][:, :LIN_DV] for ci in cs]
    rhs = each(lambda v, kb, e: jnp.concatenate([v, kb * e], axis=1), vb, kb_bd, e_b)
    x16 = each(lambda m, r_: _dot(m, r_.astype(BF16)).astype(BF16), inv16, rhs)
    l_lo = each(lambda lo, hi: (lo - hi.astype(F32)).astype(BF16), lower, l_hi)
    resid = each(lambda r_, x, hi, lo: r_ - (x.astype(F32) + _dot(hi, x) + _dot(lo, x)), rhs, x16, l_hi, l_lo)
    sol = each(lambda x, m, rs: x.astype(F32) + _dot(m, rs.astype(BF16)), x16, inv16, resid)
    for ci in cs:
        u_ref[ci] = sol[ci][:, :LIN_DV]
        w_ref[ci] = sol[ci][:, LIN_DV:].astype(BF16)
        at_ref[ci] = attn[ci]
        qd_ref[ci] = (q_bd[ci] * e_b[ci]).astype(BF16)
        kd_ref[ci] = (k_bd[ci] * jnp.exp(bl_col[ci] - b_col[ci])).astype(BF16)
        dec_ref[ci] = jnp.broadcast_to(jnp.exp(lane_row(tot_t[ci])), (SUBLANE, STACK))

    def chunk(ci, carry):
        ce = ci + d * (ncb - 1 - 2 * ci)
        rows = pl.ds(pl.multiple_of(ce * CHUNK, CHUNK), CHUNK)
        st = st_ref[...]
        st16 = st.astype(BF16)
        v_new16 = (u_ref[ce] - _dot_nt(w_ref[ce], st16)).astype(BF16)
        o = _dot_nt(qd_ref[ce], st16) + _dot(at_ref[ce], v_new16)
        st_ref[...] = dec_ref[ce][:1, :] * st + _dot_tn(v_new16, kd_ref[ce])
        for h in range(LIN_HEADS):
            o_ref[rows, h * LIN_DV:(h + 1) * LIN_DV] = o[h * CHUNK:(h + 1) * CHUNK, :]
        return carry

    lax.fori_loop(0, ncb, chunk, 0, unroll=True)


def gdn_scan(q, k, v, gates, batch, seq, tc):
    nblk = seq // tc
    t = batch * seq
    row = lambda b, d, n: (_scan_rows(b, n, d, nblk), 0)
    hk, hv = q.shape[1], v.shape[1]
    ncb = tc // CHUNK
    return pl.pallas_call(
        functools.partial(_gdn_kernel, ncb=ncb),
        grid=(batch, 2, nblk),
        in_specs=[pl.BlockSpec((tc, hk), row), pl.BlockSpec((tc, hk), row), pl.BlockSpec((tc, hv), row),
                  pl.BlockSpec((None, tc, LANE), lambda b, d, n: (d, _scan_rows(b, n, d, nblk), 0))],
        out_specs=pl.BlockSpec((None, tc, hv), lambda b, d, n: (d, _scan_rows(b, n, d, nblk), 0)),
        out_shape=jax.ShapeDtypeStruct((2, t, hv), F32),
        scratch_shapes=[pltpu.VMEM((LIN_DV, hk), F32),
                        pltpu.VMEM((ncb, STACK, LIN_DV), F32), pltpu.VMEM((ncb, STACK, hk), BF16),
                        pltpu.VMEM((ncb, STACK, STACK), BF16), pltpu.VMEM((ncb, STACK, hk), BF16),
                        pltpu.VMEM((ncb, STACK, hk), BF16), pltpu.VMEM((ncb, SUBLANE, hk), F32)],
        compiler_params=_cparams("parallel", "parallel", "arbitrary"),
        name="gdn_scan",
    )(q, k, v, gates)


def _lin_out_kernel(x_ref, oc_ref, od_ref, gc_ref, gd_ref, cw_ref, dw_ref, w_ref, o_ref):
    def gated(o_ref2, gate_ref, nw_ref):
        o = o_ref2[0] + o_ref2[1]
        parts = []
        for h in range(LIN_HEADS):
            cols = slice(h * LIN_DV, (h + 1) * LIN_DV)
            gate = gate_ref[:, cols]
            parts.append((_rms(o[:, cols], nw_ref[...]) * (gate * _sigmoid(gate))).astype(BF16))
        return jnp.concatenate(parts, axis=1)

    kc = oc_ref.shape[2]
    o_ref[...] = (x_ref[...] + _dot(gated(oc_ref, gc_ref, cw_ref), w_ref[:kc, :])
                  + _dot(gated(od_ref, gd_ref, dw_ref), w_ref[kc:, :]))


def lin_out(x, oc, od, gc, gd, cw, dw, w, tm):
    t, dm = x.shape
    row = lambda i: (i, 0)
    row3 = lambda i: (0, i, 0)
    wc, wd = oc.shape[2], od.shape[2]
    return pl.pallas_call(
        _lin_out_kernel,
        grid=(t // tm,),
        in_specs=[pl.BlockSpec((tm, dm), row), pl.BlockSpec((2, tm, wc), row3), pl.BlockSpec((2, tm, wd), row3),
                  pl.BlockSpec((tm, wc), row), pl.BlockSpec((tm, wd), row),
                  _resident((1, LIN_DV)), _resident((1, LIN_DV)), _resident(w.shape)],
        out_specs=pl.BlockSpec((tm, dm), row),
        out_shape=jax.ShapeDtypeStruct((t, dm), F32),
        compiler_params=_cparams("parallel"),
        name="lin_out",
    )(x, oc, od, gc, gd, cw.reshape(1, LIN_DV), dw.reshape(1, LIN_DV), w)


def _rot_half_cols(w):
    half = w.shape[-1] // 2
    return jnp.concatenate([-w[..., half:], w[..., :half]], axis=-1)


def _pad_heads(w, heads, dim, offset=0):
    k = w.shape[0]
    out = jnp.zeros((k, heads, LANE), w.dtype)
    out = out.at[:, :, offset:offset + dim].set(w.reshape(k, heads, dim))
    return out.reshape(k, heads * LANE)


def _layer0_in_weight(w_in):
    aq = A_HEADS * HEAD_DIM
    akv = A_KV_HEADS * HEAD_DIM
    o = 0
    w_q = w_in[:, o:o + aq]; o += aq
    w_k = w_in[:, o:o + akv]; o += akv
    w_v = w_in[:, o:o + akv]; o += akv
    w_cq = w_in[:, o:o + B_Q_RANK]; o += B_Q_RANK
    w_ckv = w_in[:, o:o + B_KV_RANK]; o += B_KV_RANK
    w_pe = w_in[:, o:o + B_ROPE]
    zeros = jnp.zeros((w_in.shape[0], B_NOPE), w_in.dtype)
    v_lo = _pad_heads(w_v, A_KV_HEADS, HEAD_DIM, 0).reshape(-1, A_KV_HEADS, LANE)
    v_hi = _pad_heads(w_v, A_KV_HEADS, HEAD_DIM, HEAD_DIM).reshape(-1, A_KV_HEADS, LANE)
    w_v2 = jnp.stack([v_lo, v_hi], axis=2).reshape(w_in.shape[0], 2 * A_KV_HEADS * LANE)
    w_pe_rot = _rot_half_cols(w_pe)
    pieces = [_pad_heads(w_q, A_HEADS, HEAD_DIM), _pad_heads(w_k, A_KV_HEADS, HEAD_DIM), w_v2,
              w_cq, w_ckv,
              jnp.concatenate([zeros, w_pe, w_pe, zeros, w_pe_rot, w_pe_rot], axis=1)]
    segs, start = [], 0
    for p in pieces:
        segs.append((start, p.shape[1]))
        start += p.shape[1]
    return jnp.concatenate(pieces, axis=1).astype(BF16), segs


def _mla_weights(w_uq, w_ukv):
    r = w_uq.shape[0]
    wq = w_uq.reshape(r, B_HEADS, B_NOPE + B_ROPE)
    pe = wq[:, :, B_NOPE:]
    wq = jnp.concatenate([wq[:, :, :B_NOPE], pe, _rot_half_cols(pe)], axis=-1).reshape(r, B_HEADS * LANE)
    rk = w_ukv.shape[0]
    wkv = w_ukv.reshape(rk, B_HEADS, B_NOPE + B_VDIM)
    wk = _pad_heads(wkv[:, :, :B_NOPE].reshape(rk, B_HEADS * B_NOPE), B_HEADS, B_NOPE)
    wv = _pad_heads(wkv[:, :, B_NOPE:].reshape(rk, B_HEADS * B_VDIM), B_HEADS, B_VDIM)
    return wq.astype(BF16), wk.astype(BF16), wv.astype(BF16)


def _rope_tables(seq):
    half = B_ROPE // 2
    inv = ROPE_THETA ** (-jnp.arange(half, dtype=F32) / half)
    ang = jnp.arange(seq, dtype=F32)[:, None] * inv[None, :]
    cos2 = jnp.tile(jnp.cos(ang), (1, 4))
    sin2 = jnp.tile(jnp.sin(ang), (1, 4))
    zeros = jnp.zeros((seq, B_NOPE), F32)
    qscale = (B_NOPE + B_ROPE) ** -0.5 * math.log2(math.e)
    q_tab = jnp.concatenate([jnp.ones((seq, B_NOPE), F32), cos2[:, :B_ROPE], sin2[:, :B_ROPE]], axis=1) * qscale
    k_cos = jnp.concatenate([zeros, cos2], axis=1)
    k_sin = jnp.concatenate([zeros, sin2], axis=1)
    return q_tab, k_cos, k_sin


def _layer1_in_weight(w_in):
    hk = LIN_HEADS * LIN_DK
    hv = LIN_HEADS * LIN_DV
    conv_ch = 2 * hk + hv
    main = 2 * hk + 2 * hv + 2 * GATE_RANK
    o = 0
    w_main = w_in[:, :2 * hk + 2 * hv]; o = 2 * hk + 2 * hv
    w_gl = w_in[:, o:o + 2 * GATE_RANK]; o += 2 * GATE_RANK
    w_conv = w_in[:, o:o + conv_ch]; o += conv_ch
    w_z = w_in[:, o:o + hv]; o += hv
    w_small = w_in[:, o:o + 4 * LIN_HEADS]
    del main
    pad = jnp.zeros((w_in.shape[0], LANE - 2 * GATE_RANK - 4 * LIN_HEADS), w_in.dtype)
    w = jnp.concatenate([w_main, w_conv, w_z, w_gl, w_small, pad], axis=1).astype(BF16)
    widths = [hk, hk, hv, hv, conv_ch, hv, LANE]
    segs, start = [], 0
    for wd in widths:
        segs.append((start, wd))
        start += wd
    return w, segs


def _pick_tile(n, pref):
    t = min(n, pref)
    while n % t:
        t //= 2
    return t


def kernel(x, att_norm, att_w_in, att_sink, mla_q_norm, mla_w_uq, mla_kv_norm, mla_w_ukv, att_w_out, lin_norm, lin_w_in, gla_w_gate_f, gla_b_gate_f, gla_w_gate_b, gla_b_gate_b, gla_norm, gdn_conv, gdn_a_log_f, gdn_dt_bias_f, gdn_a_log_b, gdn_dt_bias_b, gdn_norm, lin_w_out, mlp_norm, mlp_w1, mlp_w2, final_norm):
    batch, seq, dm = x.shape
    t = batch * seq
    depth = mlp_norm.shape[0]
    tm = _pick_tile(seq, 512)
    xs = x.reshape(t, dm)
    for layer in range(depth):
        i = layer // 2
        if layer % 2 == 0:
            w0, segs = _layer0_in_weight(att_w_in[i])
            qa, ka, va, cq, ckv, kab = norm_proj(xs, att_norm[i], w0, segs, [BF16, BF16, BF16, F32, F32, F32], tm)
            o_a = window_attention(qa, ka, va, att_sink[i].astype(F32), batch, seq)
            wq, wk, wv = _mla_weights(mla_w_uq[i], mla_w_ukv[i])
            q_tab, k_cos, k_sin = _rope_tables(seq)
            qb = mla_q_prep(cq, mla_q_norm[i], wq, q_tab, batch, seq, tm)
            kb, vb = mla_kv_prep(ckv, kab, mla_kv_norm[i], wk, wv, k_cos, k_sin, batch, seq, tm)
            o_b = mla_flash(qb, kb, vb, batch, seq, _pick_tile(seq, 512), _pick_tile(seq, 2048), 64)
            xs = out_proj(xs, o_a, o_b, att_w_out[i].astype(BF16), tm)
        else:
            w1, segs = _layer1_in_weight(lin_w_in[i])
            cq, ck, cv, cg, dqkv, dz, small = norm_proj(xs, lin_norm[i], w1, segs, [F32] * 7, _pick_tile(seq, 256))
            hk = LIN_HEADS * LIN_DK
            wg = jnp.zeros((2, LANE, hk), F32)
            wg = wg.at[0, :GATE_RANK].set(gla_w_gate_f[i]).at[1, GATE_RANK:2 * GATE_RANK].set(gla_w_gate_b[i])
            bg = jnp.stack([gla_b_gate_f[i], gla_b_gate_b[i]]).reshape(2, 1, hk).astype(F32)
            tc = _pick_tile(seq, 256)
            o_c = gla_scan(cq, ck, cv, small, wg.astype(BF16), bg, batch, seq, tc)
            gsum = (jnp.arange(hk)[:, None] // LIN_DK == jnp.arange(hk)[None, :] // LIN_DK).astype(F32)
            nh = LIN_HEADS
            a0 = GATE_LANE0 + 2 * nh
            alog = jnp.zeros((LANE,), F32).at[a0:a0 + nh].set(gdn_a_log_f[i]).at[a0 + nh:a0 + 2 * nh].set(gdn_a_log_b[i])
            dtb = jnp.zeros((LANE,), F32).at[a0:a0 + nh].set(gdn_dt_bias_f[i]).at[a0 + nh:a0 + 2 * nh].set(gdn_dt_bias_b[i])
            dq, dk, dv, mixed = gdn_prep(dqkv, small, gdn_conv[i].astype(F32), gsum,
                                         alog.reshape(1, LANE), dtb.reshape(1, LANE), batch, seq, tm)
            g0 = GATE_LANE0
            zpad = jnp.zeros((t, LANE - 2 * nh), F32)
            gates = jnp.stack([
                jnp.concatenate([mixed[:, g0:g0 + nh], mixed[:, g0 + 2 * nh:g0 + 3 * nh], zpad], axis=1),
                jnp.concatenate([mixed[:, g0 + nh:g0 + 2 * nh], mixed[:, g0 + 3 * nh:g0 + 4 * nh], zpad], axis=1)])
            o_d = gdn_scan(dq, dk, dv, gates, batch, seq, tc)
            xs = lin_out(xs, o_c, o_d, cg, dz, gla_norm[i], gdn_norm[i], lin_w_out[i].astype(BF16), tm)
        last = layer == depth - 1
        xs = mlp(xs, mlp_norm[layer], mlp_w1[layer].astype(BF16), mlp_w2[layer].astype(BF16),
                 final_norm, last, tm, 1024)
    return xs.reshape(batch, seq, dm)
```

```python
import functools
import math

import numpy as np
import jax
import jax.numpy as jnp
from jax import lax
from jax.experimental import pallas as pl
from jax.experimental.pallas import tpu as pltpu

F32 = jnp.float32
BF16 = jnp.bfloat16
EPS = 1e-6

LANE = 128
SUBLANE = 8
VMEM_LIMIT_BYTES = 56 * 1024 * 1024

HEAD_DIM = 64
A_HEADS = 8
A_KV_HEADS = 2
A_GROUP = A_HEADS // A_KV_HEADS
WINDOW = 128
BLOCK = 128
B_HEADS = 8
B_NOPE = 64
B_ROPE = 32
B_VDIM = 64
B_Q_RANK = 384
B_KV_RANK = 256
ROPE_THETA = 10000.0
LIN_HEADS = 4
LIN_DK = 64
LIN_DV = 128
GATE_RANK = 16
GATE_NORM = 16.0
CHUNK = 64
CONV_TAPS = 5
STACK = LIN_HEADS * CHUNK
GATE_LANE0 = 2 * GATE_RANK

NT_DIMS = (((1,), (1,)), ((), ()))
TN_DIMS = (((0,), (0,)), ((), ()))


def _cparams(*sem):
    return pltpu.CompilerParams(dimension_semantics=sem, vmem_limit_bytes=VMEM_LIMIT_BYTES)


def _resident(shape):
    nd = len(shape)
    return pl.BlockSpec(shape, lambda *_: (0,) * nd, pipeline_mode=pl.Buffered(1))


def _rms(x, w):
    return x * lax.rsqrt(jnp.mean(x * x, axis=-1, keepdims=True) + EPS) * w


def _sigmoid(x):
    return 1.0 / (1.0 + jnp.exp(-x))


def _softplus(x):
    return jnp.maximum(x, 0.0) + jnp.log1p(jnp.exp(-jnp.abs(x)))


def _dot(a, b):
    return jnp.dot(a, b, preferred_element_type=F32)


def _dot_nt(a, b):
    return lax.dot_general(a, b, NT_DIMS, preferred_element_type=F32)


def _dot_tn(a, b):
    return lax.dot_general(a, b, TN_DIMS, preferred_element_type=F32)


def _dot_exact(a, b):
    return jnp.dot(a, b, preferred_element_type=F32, precision=lax.Precision.HIGHEST)


def _dot_tn_exact(a, b):
    return lax.dot_general(a, b, TN_DIMS, preferred_element_type=F32, precision=lax.Precision.HIGHEST)


def _dot_mask_f32(mask16, x):
    hi = x.astype(BF16)
    r1 = x - hi.astype(F32)
    mid = r1.astype(BF16)
    lo = (r1 - mid.astype(F32)).astype(BF16)
    return _dot(mask16, hi) + _dot(mask16, mid) + _dot(mask16, lo)


def _norm_proj_kernel(x_ref, nw_ref, w_ref, *out_refs, segs):
    xn = _rms(x_ref[...], nw_ref[...]).astype(BF16)
    for (start, width), o_ref in zip(segs, out_refs):
        o_ref[...] = _dot(xn, w_ref[:, start:start + width]).astype(o_ref.dtype)


def norm_proj(x, norm_w, w, segs, dtypes, tm):
    t, k = x.shape
    n = w.shape[1]
    return pl.pallas_call(
        functools.partial(_norm_proj_kernel, segs=tuple(segs)),
        grid=(t // tm,),
        in_specs=[pl.BlockSpec((tm, k), lambda i: (i, 0)),
                  _resident((1, k)),
                  _resident((k, n))],
        out_specs=[pl.BlockSpec((tm, wd), lambda i: (i, 0)) for _, wd in segs],
        out_shape=[jax.ShapeDtypeStruct((t, wd), dt) for (_, wd), dt in zip(segs, dtypes)],
        compiler_params=_cparams("parallel"),
        name="norm_proj",
    )(x, norm_w.reshape(1, k), w)


def _win_attn_kernel(sink_ref, bias_ref, q_ref, kp_ref, kc_ref, kn_ref, vp_ref, vc_ref, vn_ref, o_ref):
    k3 = jnp.concatenate([kp_ref[...], kc_ref[...], kn_ref[...]], axis=0)
    v3 = jnp.concatenate([vp_ref[...], vc_ref[...], vn_ref[...]], axis=0)
    scale = HEAD_DIM ** -0.5
    heads = range(A_HEADS)
    s = [_dot_nt(q_ref[:, h * LANE:(h + 1) * LANE], k3[:, (h // A_GROUP) * LANE:(h // A_GROUP + 1) * LANE])
         for h in heads]
    s = [s[h] * scale + bias_ref[h] for h in heads]
    m = [jnp.maximum(jnp.max(s[h], axis=-1, keepdims=True), sink_ref[h]) for h in heads]
    p = [jnp.exp(s[h] - m[h]) for h in heads]
    rden = [1.0 / (jnp.sum(p[h], axis=-1, keepdims=True) + jnp.exp(sink_ref[h] - m[h])) for h in heads]
    pv = [_dot(p[h].astype(BF16), v3[:, (2 * (h // A_GROUP) + h % 2) * LANE:(2 * (h // A_GROUP) + h % 2 + 1) * LANE])
          for h in heads]
    for pair in range(A_HEADS // 2):
        h0, h1 = 2 * pair, 2 * pair + 1
        o_ref[:, pair * LANE:(pair + 1) * LANE] = (pv[h0] * rden[h0] + pv[h1] * rden[h1]).astype(o_ref.dtype)


def _window_bias():
    qi = np.arange(BLOCK)[:, None]
    kj = np.arange(3 * BLOCK)[None, :]
    dist = np.abs(qi + BLOCK - kj)
    slopes = np.array([2.0 ** (-8.0 * (h + 1) / A_HEADS) for h in range(A_HEADS)], np.float32)
    base = -slopes[:, None, None] * dist[None].astype(np.float32)
    out = []
    for first in (0, 1):
        for last in (0, 1):
            valid = (dist <= WINDOW) & ((kj >= BLOCK) | (first == 0)) & ((kj < 2 * BLOCK) | (last == 0))
            out.append(np.where(valid[None], base, -np.inf))
    return jnp.asarray(np.stack(out), F32)


def window_attention(q, k, v, sink, batch, seq):
    nb = seq // BLOCK
    cur = lambda b, n: (b * nb + n, 0)
    prev = lambda b, n: (b * nb + jnp.maximum(n - 1, 0), 0)
    nxt = lambda b, n: (b * nb + jnp.minimum(n + 1, nb - 1), 0)
    variant = lambda b, n: (2 * (n == 0).astype(jnp.int32) + (n == nb - 1).astype(jnp.int32), 0, 0, 0)
    kw, vw = k.shape[1], v.shape[1]
    return pl.pallas_call(
        _win_attn_kernel,
        grid=(batch, nb),
        in_specs=[pl.BlockSpec(memory_space=pltpu.SMEM),
                  pl.BlockSpec((None, A_HEADS, BLOCK, 3 * BLOCK), variant),
                  pl.BlockSpec((BLOCK, q.shape[1]), cur),
                  pl.BlockSpec((BLOCK, kw), prev), pl.BlockSpec((BLOCK, kw), cur), pl.BlockSpec((BLOCK, kw), nxt),
                  pl.BlockSpec((BLOCK, vw), prev), pl.BlockSpec((BLOCK, vw), cur), pl.BlockSpec((BLOCK, vw), nxt)],
        out_specs=pl.BlockSpec((BLOCK, A_HEADS * HEAD_DIM), cur),
        out_shape=jax.ShapeDtypeStruct((batch * seq, A_HEADS * HEAD_DIM), BF16),
        compiler_params=_cparams("parallel", "parallel"),
        name="window_attention",
    )(sink, _window_bias(), q, k, k, k, v, v, v)


def _mla_q_kernel(cq_ref, nw_ref, w_ref, tab_ref, o_ref):
    xn = _rms(cq_ref[...].astype(F32), nw_ref[...]).astype(BF16)
    tab = tab_ref[...]
    for h in range(B_HEADS):
        cols = slice(h * LANE, (h + 1) * LANE)
        o_ref[:, cols] = (_dot(xn, w_ref[:, cols]) * tab).astype(o_ref.dtype)


def mla_q_prep(cq, norm_w, w, tab, batch, seq, tm):
    nt = seq // tm
    r = cq.shape[1]
    n = w.shape[1]
    return pl.pallas_call(
        _mla_q_kernel,
        grid=(batch, nt),
        in_specs=[pl.BlockSpec((tm, r), lambda b, i: (b * nt + i, 0)),
                  _resident((1, r)), _resident((r, n)),
                  pl.BlockSpec((tm, LANE), lambda b, i: (i, 0))],
        out_specs=pl.BlockSpec((tm, n), lambda b, i: (b * nt + i, 0)),
        out_shape=jax.ShapeDtypeStruct((batch * seq, n), BF16),
        compiler_params=_cparams("parallel", "parallel"),
        name="mla_q_prep",
    )(cq, norm_w.reshape(1, r), w, tab)


def _mla_kv_kernel(ckv_ref, kab_ref, nw_ref, wk_ref, wv_ref, ctab_ref, stab_ref, k_ref, v_ref):
    xn = _rms(ckv_ref[...].astype(F32), nw_ref[...]).astype(BF16)
    kr = kab_ref[:, :LANE] * ctab_ref[...] + kab_ref[:, LANE:] * stab_ref[...]
    for h in range(B_HEADS):
        cols = slice(h * LANE, (h + 1) * LANE)
        k_ref[:, cols] = (_dot(xn, wk_ref[:, cols]) + kr).astype(k_ref.dtype)
    lane = lax.broadcasted_iota(jnp.int32, v_ref.shape, 1) % LANE
    v_ref[...] = jnp.where(lane == B_VDIM, 1.0, _dot(xn, wv_ref[...])).astype(v_ref.dtype)


def mla_kv_prep(ckv, kab, norm_w, wk, wv, ctab, stab, batch, seq, tm):
    nt = seq // tm
    r = ckv.shape[1]
    row = lambda b, i: (b * nt + i, 0)
    pos = lambda b, i: (i, 0)
    return pl.pallas_call(
        _mla_kv_kernel,
        grid=(batch, nt),
        in_specs=[pl.BlockSpec((tm, r), row), pl.BlockSpec((tm, 2 * LANE), row),
                  _resident((1, r)), _resident(wk.shape), _resident(wv.shape),
                  pl.BlockSpec((tm, LANE), pos), pl.BlockSpec((tm, LANE), pos)],
        out_specs=[pl.BlockSpec((tm, wk.shape[1]), row), pl.BlockSpec((tm, wv.shape[1]), row)],
        out_shape=[jax.ShapeDtypeStruct((batch * seq, wk.shape[1]), BF16),
                   jax.ShapeDtypeStruct((batch * seq, wv.shape[1]), BF16)],
        compiler_params=_cparams("parallel", "parallel"),
        name="mla_kv_prep",
    )(ckv, kab, norm_w.reshape(1, r), wk, wv, ctab, stab)


def _mla_flash_kernel(q_ref, k_ref, v_ref, o_ref, s_ref, p_ref, m_ref, a_ref, acc_ref, *, tk, nk, rb):
    tq = q_ref.shape[0]
    heads = range(2)
    m_ref[...] = jnp.full(m_ref.shape, -jnp.inf, F32)
    acc_ref[...] = jnp.zeros(acc_ref.shape, F32)

    def body(j, carry):
        rows = pl.ds(pl.multiple_of(j * tk, tk), tk)
        for e in heads:
            cols = slice(e * LANE, (e + 1) * LANE)
            s_ref[e] = _dot_nt(q_ref[:, cols], k_ref[rows, cols])
        for e in heads:
            cols = slice(e * LANE, (e + 1) * LANE)
            for r in range(tq // rb):
                rs = slice(r * rb, (r + 1) * rb)
                s = s_ref[e, rs, :]
                m_old = m_ref[e, rs, :]
                m_new = jnp.maximum(m_old, jnp.max(s, axis=-1, keepdims=True))
                m_ref[e, rs, :] = m_new
                a_ref[e, rs, :] = jnp.exp2(m_old - m_new)
                p_ref[e, rs, :] = jnp.exp2(s - jnp.concatenate([m_new] * (tk // LANE), axis=1)).astype(BF16)
            acc_ref[e] = a_ref[e] * acc_ref[e] + _dot(p_ref[e], v_ref[rows, cols])
        return carry

    lax.fori_loop(0, nk, body, 0)
    outs = []
    for e in heads:
        acc = acc_ref[e]
        outs.append(acc / acc[:, B_VDIM:B_VDIM + 1])
    lane = lax.broadcasted_iota(jnp.int32, (tq, LANE), 1)
    o_ref[...] = jnp.where(lane < B_VDIM, outs[0], pltpu.roll(outs[1], B_VDIM, axis=1)).astype(o_ref.dtype)


def mla_flash(q, k, v, batch, seq, tq, tk, rb):
    nq = seq // tq
    npair = B_HEADS // 2
    return pl.pallas_call(
        functools.partial(_mla_flash_kernel, tk=tk, nk=seq // tk, rb=rb),
        grid=(batch, npair, nq),
        in_specs=[pl.BlockSpec((tq, 2 * LANE), lambda b, p, i: (b * nq + i, p)),
                  pl.BlockSpec((seq, 2 * LANE), lambda b, p, i: (b, p)),
                  pl.BlockSpec((seq, 2 * LANE), lambda b, p, i: (b, p))],
        out_specs=pl.BlockSpec((tq, LANE), lambda b, p, i: (b * nq + i, p)),
        out_shape=jax.ShapeDtypeStruct((batch * seq, B_HEADS * B_VDIM), BF16),
        scratch_shapes=[pltpu.VMEM((2, tq, tk), F32), pltpu.VMEM((2, tq, tk), BF16), pltpu.VMEM((2, tq, LANE), F32),
                        pltpu.VMEM((2, tq, LANE), F32), pltpu.VMEM((2, tq, LANE), F32)],
        compiler_params=_cparams("parallel", "parallel", "parallel"),
        name="mla_flash",
    )(q, k, v)


def _out_proj_kernel(x_ref, a_ref, b_ref, w_ref, o_ref):
    ka = a_ref.shape[1]
    o_ref[...] = x_ref[...] + _dot(a_ref[...], w_ref[:ka, :]) + _dot(b_ref[...], w_ref[ka:, :])


def out_proj(x, a, b, w, tm):
    t, d = x.shape
    row = lambda i: (i, 0)
    return pl.pallas_call(
        _out_proj_kernel,
        grid=(t // tm,),
        in_specs=[pl.BlockSpec((tm, d), row), pl.BlockSpec((tm, a.shape[1]), row),
                  pl.BlockSpec((tm, b.shape[1]), row), _resident(w.shape)],
        out_specs=pl.BlockSpec((tm, d), row),
        out_shape=jax.ShapeDtypeStruct((t, d), F32),
        compiler_params=_cparams("parallel"),
        name="out_proj",
    )(x, a, b, w)


def _mlp_kernel(x_ref, nw_ref, w1_ref, w2_ref, fw_ref, o_ref, *, fchunk, final_norm):
    x = x_ref[...]
    xn = _rms(x, nw_ref[...]).astype(BF16)
    acc = x
    for f in range(w1_ref.shape[1] // fchunk):
        cols = slice(f * fchunk, (f + 1) * fchunk)
        h = jnp.square(jnp.maximum(_dot(xn, w1_ref[:, cols]), 0.0)).astype(BF16)
        acc = acc + _dot(h, w2_ref[cols, :])
    if final_norm:
        acc = _rms(acc, fw_ref[...])
    o_ref[...] = acc


def mlp(x, norm_w, w1, w2, final_w, final_norm, tm, fchunk):
    t, d = x.shape
    row = lambda i: (i, 0)
    return pl.pallas_call(
        functools.partial(_mlp_kernel, fchunk=fchunk, final_norm=final_norm),
        grid=(t // tm,),
        in_specs=[pl.BlockSpec((tm, d), row), _resident((1, d)), _resident(w1.shape),
                  _resident(w2.shape), _resident((1, d))],
        out_specs=pl.BlockSpec((tm, d), row),
        out_shape=jax.ShapeDtypeStruct((t, d), F32),
        compiler_params=_cparams("parallel"),
        name="mlp",
    )(x, norm_w.reshape(1, d), w1, w2, final_w.reshape(1, d))


def _direction_masks(d):
    sgn = 1 - 2 * d
    r = lax.broadcasted_iota(jnp.int32, (STACK, STACK), 0)
    c = lax.broadcasted_iota(jnp.int32, (STACK, STACK), 1)
    same = (r & -CHUNK) == (c & -CHUNK)
    order = ((r & (CHUNK - 1)) - (c & (CHUNK - 1))) * sgn
    i = lax.broadcasted_iota(jnp.int32, (CHUNK, CHUNK), 0)
    j = lax.broadcasted_iota(jnp.int32, (CHUNK, CHUNK), 1)
    cum = (((i - j) * sgn) >= 0).astype(F32)
    return same, same & (order >= 0), same & (order > 0), cum


def _block_diag(x, same):
    return jnp.where(same, jnp.concatenate([x] * LIN_HEADS, axis=0), 0.0)


def _stack_heads(v, width):
    return jnp.concatenate([v[:, h * width:(h + 1) * width] for h in range(LIN_HEADS)], axis=0)


def _scan_rows(b, n, d, nblk):
    return b * nblk + n + d * (nblk - 1 - 2 * n)


def _gla_kernel(q_ref, k_ref, v_ref, sm_ref, wg_ref, bg_ref, o_ref, st_ref, oi_ref, qi_ref, ds_ref, dec_ref, *, ncb):
    d = pl.program_id(1)

    @pl.when(pl.program_id(2) == 0)
    def _():
        st_ref[...] = jnp.zeros_like(st_ref)

    same, causal, _, cum = _direction_masks(d)

    cs = range(ncb)

    def each(f, *lists):
        return [f(*args) for args in zip(*lists)]

    z = [_dot(sm_ref[ci * CHUNK:(ci + 1) * CHUNK, :].astype(BF16), wg_ref[...]) + bg_ref[...] for ci in cs]
    log_a = each(lambda a: (jnp.minimum(a, 0.0) - jnp.log1p(jnp.exp(-jnp.abs(a)))) * (1.0 / GATE_NORM), z)
    cum16 = cum.astype(BF16)
    b = each(lambda a: _dot_mask_f32(cum16, a), log_a)
    b_last = each(lambda a: jnp.sum(a, axis=0, keepdims=True), log_a)
    q_in = [_block_diag(q_ref[ci * CHUNK:(ci + 1) * CHUNK, :] * (LIN_DK ** -0.5) * jnp.exp(b[ci]), same).astype(BF16)
            for ci in cs]
    k = [k_ref[ci * CHUNK:(ci + 1) * CHUNK, :] for ci in cs]
    k_in = each(lambda a, bb: _block_diag(a * jnp.exp(-bb), same).astype(BF16), k, b)
    k_out = each(lambda a, bb, bl: _block_diag(a * jnp.exp(bl - bb), same).astype(BF16), k, b, b_last)
    v = [_stack_heads(v_ref[ci * CHUNK:(ci + 1) * CHUNK, :], LIN_DV).astype(BF16) for ci in cs]
    sc = each(lambda a, bb: jnp.where(causal, _dot_nt(a, bb), 0.0).astype(BF16), q_in, k_in)
    o_intra = each(_dot, sc, v)
    ds = each(_dot_tn, v, k_out)
    for ci in cs:
        oi_ref[ci] = o_intra[ci]
        qi_ref[ci] = q_in[ci]
        ds_ref[ci] = ds[ci]
        dec_ref[ci] = jnp.broadcast_to(jnp.exp(b_last[ci]), (SUBLANE, STACK))

    def chunk(ci, carry):
        ce = ci + d * (ncb - 1 - 2 * ci)
        rows = pl.ds(pl.multiple_of(ce * CHUNK, CHUNK), CHUNK)
        st = st_ref[...]
        o = oi_ref[ce] + _dot_nt(qi_ref[ce], st.astype(BF16))
        st_ref[...] = dec_ref[ce][:1, :] * st + ds_ref[ce]
        for h in range(LIN_HEADS):
            o_ref[rows, h * LIN_DV:(h + 1) * LIN_DV] = o[h * CHUNK:(h + 1) * CHUNK, :]
        return carry

    lax.fori_loop(0, ncb, chunk, 0, unroll=True)


def gla_scan(q, k, v, small, wg, bg, batch, seq, tc):
    nblk = seq // tc
    t = batch * seq
    row = lambda b, d, n: (_scan_rows(b, n, d, nblk), 0)
    dsel = lambda b, d, n: (d, 0, 0)
    hk, hv = q.shape[1], v.shape[1]
    ncb = tc // CHUNK
    return pl.pallas_call(
        functools.partial(_gla_kernel, ncb=ncb),
        grid=(batch, 2, nblk),
        in_specs=[pl.BlockSpec((tc, hk), row), pl.BlockSpec((tc, hk), row), pl.BlockSpec((tc, hv), row),
                  pl.BlockSpec((tc, LANE), row),
                  pl.BlockSpec((None, LANE, hk), dsel), pl.BlockSpec((None, 1, hk), dsel)],
        out_specs=pl.BlockSpec((None, tc, hv), lambda b, d, n: (d, _scan_rows(b, n, d, nblk), 0)),
        out_shape=jax.ShapeDtypeStruct((2, t, hv), F32),
        scratch_shapes=[pltpu.VMEM((LIN_DV, hk), F32),
                        pltpu.VMEM((ncb, STACK, LIN_DV), F32), pltpu.VMEM((ncb, STACK, hk), BF16),
                        pltpu.VMEM((ncb, LIN_DV, hk), F32), pltpu.VMEM((ncb, SUBLANE, hk), F32)],
        compiler_params=_cparams("parallel", "parallel", "arbitrary"),
        name="gla_scan",
    )(q, k, v, small, wg, bg)


def _gdn_prep_kernel(xp_ref, xc_ref, xn_ref, sm_ref, cw_ref, gsum_ref, alog_ref, dtb_ref,
                     q_ref, k_ref, v_ref, g_ref, b_ref, bt_ref, buf_ref, *, nt):
    i = pl.program_id(1)
    tm = xc_ref.shape[0]
    halo = xp_ref.shape[0]
    buf_ref[:halo, :] = xp_ref[...].astype(F32) * (i > 0).astype(F32)
    buf_ref[halo:halo + tm, :] = xc_ref[...].astype(F32)
    buf_ref[halo + tm:, :] = xn_ref[...].astype(F32) * (i < nt - 1).astype(F32)
    acc = jnp.zeros(xc_ref.shape, F32)
    for j in range(CONV_TAPS):
        off = halo + j - CONV_TAPS // 2
        acc = acc + buf_ref[off:off + tm, :] * cw_ref[j:j + 1, :]
    y = acc * _sigmoid(acc)
    nqk = LIN_HEADS * LIN_DK
    for idx, (o_ref, scale) in enumerate(((q_ref, LIN_DK ** -0.5), (k_ref, 1.0))):
        part = y[:, idx * nqk:(idx + 1) * nqk]
        ssq = _dot_exact(part * part, gsum_ref[...])
        o_ref[...] = (part * lax.rsqrt(ssq + EPS) * scale).astype(o_ref.dtype)
    v_ref[...] = y[:, 2 * nqk:].astype(v_ref.dtype)
    sm = sm_ref[...]
    beta = _sigmoid(sm)
    g = -jnp.exp(alog_ref[...]) * _softplus(sm + dtb_ref[...])
    lane = lax.broadcasted_iota(jnp.int32, sm.shape, 1) - GATE_LANE0
    is_beta = (lane >= 0) & (lane < 2 * LIN_HEADS)
    is_g = (lane >= 2 * LIN_HEADS) & (lane < 4 * LIN_HEADS)
    g = jnp.where(is_g, g, 0.0)
    g_ref[...] = jnp.where(is_beta, beta, g)
    ri = lax.broadcasted_iota(jnp.int32, (CHUNK, CHUNK), 0)
    ci = lax.broadcasted_iota(jnp.int32, (CHUNK, CHUNK), 1)
    tril16 = (ri >= ci).astype(BF16)
    ones16 = jnp.ones((CHUNK, CHUNK), BF16)
    is_bwd = lax.broadcasted_iota(jnp.int32, (CHUNK, LANE), 1) >= GATE_LANE0 + 3 * LIN_HEADS
    for c in range(tm // CHUNK):
        rows = slice(c * CHUNK, (c + 1) * CHUNK)
        gc = g[rows, :]
        prefix = _dot_mask_f32(tril16, gc)
        total = _dot_mask_f32(ones16, gc)
        b_ref[rows, :] = jnp.where(is_bwd, total - prefix + gc, prefix)
        bt_ref[rows, :] = total


def gdn_prep(dqkv, small, conv_w, gsum, alog, dtb, batch, seq, tm):
    nt = seq // tm
    t, c = dqkv.shape
    halo = 2 * SUBLANE
    hb = tm // halo
    nqk = LIN_HEADS * LIN_DK
    row = lambda b, i: (b * nt + i, 0)
    prev = lambda b, i: (jnp.maximum((b * nt + i) * hb - 1, 0), 0)
    nxt = lambda b, i: (jnp.minimum((b * nt + i + 1) * hb, t // halo - 1), 0)
    return pl.pallas_call(
        functools.partial(_gdn_prep_kernel, nt=nt),
        grid=(batch, nt),
        in_specs=[pl.BlockSpec((halo, c), prev), pl.BlockSpec((tm, c), row), pl.BlockSpec((halo, c), nxt),
                  pl.BlockSpec((tm, LANE), row),
                  _resident(conv_w.shape), _resident(gsum.shape), _resident((1, LANE)), _resident((1, LANE))],
        out_specs=[pl.BlockSpec((tm, nqk), row), pl.BlockSpec((tm, nqk), row),
                   pl.BlockSpec((tm, c - 2 * nqk), row)] + [pl.BlockSpec((tm, LANE), row)] * 3,
        out_shape=[jax.ShapeDtypeStruct((t, nqk), BF16), jax.ShapeDtypeStruct((t, nqk), BF16),
                   jax.ShapeDtypeStruct((t, c - 2 * nqk), BF16)] + [jax.ShapeDtypeStruct((t, LANE), F32)] * 3,
        scratch_shapes=[pltpu.VMEM((tm + 2 * halo, c), F32)],
        compiler_params=_cparams("parallel", "parallel"),
        name="gdn_prep",
    )(dqkv, dqkv, dqkv, small, conv_w, gsum, alog, dtb)


def _gdn_kernel(q_ref, k_ref, v_ref, g_ref, o_ref, st_ref, u_ref, w_ref, at_ref, qd_ref, kd_ref, dec_ref, *, ncb):
    d = pl.program_id(1)

    @pl.when(pl.program_id(2) == 0)
    def _():
        st_ref[...] = jnp.zeros_like(st_ref)

    same, incl, strict, _ = _direction_masks(d)
    r = lax.broadcasted_iota(jnp.int32, (STACK, STACK), 0)
    c = lax.broadcasted_iota(jnp.int32, (STACK, STACK), 1)
    diag = r == c
    eye = diag.astype(F32)
    lane_blk = lax.broadcasted_iota(jnp.int32, (SUBLANE, STACK), 1) // CHUNK

    def rows_of(x, lane0):
        return jnp.concatenate(
            [jnp.broadcast_to(x[:, lane0 + h:lane0 + h + 1], (CHUNK, STACK)) for h in range(LIN_HEADS)], axis=0)

    def head_lanes(row, lane0):
        out = jnp.broadcast_to(row[:, lane0:lane0 + 1], (SUBLANE, STACK))
        for h in range(1, LIN_HEADS):
            out = jnp.where(lane_blk == h, jnp.broadcast_to(row[:, lane0 + h:lane0 + h + 1], (SUBLANE, STACK)), out)
        return out

    cs = range(ncb)

    def each(f, *lists):
        return [f(*args) for args in zip(*lists)]

    gates = [g_ref[ci * CHUNK:(ci + 1) * CHUNK, :] for ci in cs]
    k_bd = [_block_diag(k_ref[ci * CHUNK:(ci + 1) * CHUNK, :], same) for ci in cs]
    q_bd = [_block_diag(q_ref[ci * CHUNK:(ci + 1) * CHUNK, :], same) for ci in cs]
    beta_col = each(lambda g: rows_of(g, 0), gates)
    kb_bd = each(lambda a, b: a * b, k_bd, beta_col)
    k16 = each(lambda a: a.astype(BF16), k_bd)
    kk = each(lambda a, b: _dot_nt(a.astype(BF16), b), kb_bd, k16)
    qk = each(lambda a, b: _dot_nt(a.astype(BF16), b), q_bd, k16)
    b_col = each(lambda g: rows_of(g, 2 * LIN_HEADS), gates)
    bl_col = each(lambda g: rows_of(g, 3 * LIN_HEADS), gates)
    b_row = each(lambda bc: jnp.sum(jnp.where(diag, bc, 0.0), axis=0, keepdims=True), b_col)
    gamma = each(lambda bc, br: jnp.exp(jnp.where(incl, bc - br, -jnp.inf)), b_col, b_row)
    lower = each(lambda a, g: jnp.where(strict, a * g, 0.0), kk, gamma)
    attn = each(lambda a, g: (a * g).astype(BF16), qk, gamma)
    l_hi = each(lambda a: a.astype(BF16), lower)
    pw = [l_hi]
    for _ in range(5):
        pw.append(each(lambda a: _dot(a, a).astype(BF16), pw[-1]))

    def pair(a, b16):
        return a + _dot(a.astype(BF16), b16)

    f01 = each(lambda lo, p1: pair(eye - lo, p1), lower, pw[1])
    f23 = each(lambda p2, p3: pair(eye + p2.astype(F32), p3), pw[2], pw[3])
    f45 = each(lambda p4, p5: pair(eye + p4.astype(F32), p5), pw[4], pw[5])
    f0123 = each(lambda a, b: _dot(a.astype(BF16), b.astype(BF16)).astype(BF16), f01, f23)
    inv16 = each(lambda a, b: _dot(a, b.astype(BF16)).astype(BF16), f0123, f45)
    e_b = each(jnp.exp, b_col)
    vb = [_stack_heads(v_ref[ci * CHUNK:(ci + 1) * CHUNK, :], LIN_DV) * beta_col[ci][:, :LIN_DV] for ci in cs]
    rhs = each(lambda v, kb, e: jnp.concatenate([v, kb * e], axis=1), vb, kb_bd, e_b)
    x16 = each(lambda m, r_: _dot(m, r_.astype(BF16)).astype(BF16), inv16, rhs)
    l_lo = each(lambda lo, hi: (lo - hi.astype(F32)).astype(BF16), lower, l_hi)
    resid = each(lambda r_, x, hi, lo: r_ - (x.astype(F32) + _dot(hi, x) + _dot(lo, x)), rhs, x16, l_hi, l_lo)
    sol = each(lambda x, m, rs: x.astype(F32) + _dot(m, rs.astype(BF16)), x16, inv16, resid)
    for ci in cs:
        u_ref[ci] = sol[ci][:, :LIN_DV]
        w_ref[ci] = sol[ci][:, LIN_DV:].astype(BF16)
        at_ref[ci] = attn[ci]
        qd_ref[ci] = (q_bd[ci] * e_b[ci]).astype(BF16)
        kd_ref[ci] = (k_bd[ci] * jnp.exp(bl_col[ci] - b_col[ci])).astype(BF16)
        dec_ref[ci] = jnp.exp(head_lanes(gates[ci][:SUBLANE, :], 3 * LIN_HEADS))

    def chunk(ci, carry):
        ce = ci + d * (ncb - 1 - 2 * ci)
        rows = pl.ds(pl.multiple_of(ce * CHUNK, CHUNK), CHUNK)
        st = st_ref[...]
        st16 = st.astype(BF16)
        v_new16 = (u_ref[ce] - _dot_nt(w_ref[ce], st16)).astype(BF16)
        o = _dot_nt(qd_ref[ce], st16) + _dot(at_ref[ce], v_new16)
        st_ref[...] = dec_ref[ce][:1, :] * st + _dot_tn(v_new16, kd_ref[ce])
        for h in range(LIN_HEADS):
            o_ref[rows, h * LIN_DV:(h + 1) * LIN_DV] = o[h * CHUNK:(h + 1) * CHUNK, :]
        return carry

    lax.fori_loop(0, ncb, chunk, 0, unroll=True)


def gdn_scan(q, k, v, gates, batch, seq, tc):
    nblk = seq // tc
    t = batch * seq
    row = lambda b, d, n: (_scan_rows(b, n, d, nblk), 0)
    hk, hv = q.shape[1], v.shape[1]
    ncb = tc // CHUNK
    return pl.pallas_call(
        functools.partial(_gdn_kernel, ncb=ncb),
        grid=(batch, 2, nblk),
        in_specs=[pl.BlockSpec((tc, hk), row), pl.BlockSpec((tc, hk), row), pl.BlockSpec((tc, hv), row),
                  pl.BlockSpec((None, tc, LANE), lambda b, d, n: (d, _scan_rows(b, n, d, nblk), 0))],
        out_specs=pl.BlockSpec((None, tc, hv), lambda b, d, n: (d, _scan_rows(b, n, d, nblk), 0)),
        out_shape=jax.ShapeDtypeStruct((2, t, hv), F32),
        scratch_shapes=[pltpu.VMEM((LIN_DV, hk), F32),
                        pltpu.VMEM((ncb, STACK, LIN_DV), F32), pltpu.VMEM((ncb, STACK, hk), BF16),
                        pltpu.VMEM((ncb, STACK, STACK), BF16), pltpu.VMEM((ncb, STACK, hk), BF16),
                        pltpu.VMEM((ncb, STACK, hk), BF16), pltpu.VMEM((ncb, SUBLANE, hk), F32)],
        compiler_params=_cparams("parallel", "parallel", "arbitrary"),
        name="gdn_scan",
    )(q, k, v, gates)


def _lin_out_kernel(x_ref, oc_ref, od_ref, gc_ref, gd_ref, cw_ref, dw_ref, w_ref, o_ref):
    def gated(o_ref2, gate_ref, nw_ref):
        o = o_ref2[0] + o_ref2[1]
        parts = []
        for h in range(LIN_HEADS):
            cols = slice(h * LIN_DV, (h + 1) * LIN_DV)
            gate = gate_ref[:, cols].astype(F32)
            parts.append((_rms(o[:, cols], nw_ref[...]) * (gate * _sigmoid(gate))).astype(BF16))
        return jnp.concatenate(parts, axis=1)

    kc = oc_ref.shape[2]
    o_ref[...] = (x_ref[...] + _dot(gated(oc_ref, gc_ref, cw_ref), w_ref[:kc, :])
                  + _dot(gated(od_ref, gd_ref, dw_ref), w_ref[kc:, :]))


def lin_out(x, oc, od, gc, gd, cw, dw, w, tm):
    t, dm = x.shape
    row = lambda i: (i, 0)
    row3 = lambda i: (0, i, 0)
    wc, wd = oc.shape[2], od.shape[2]
    return pl.pallas_call(
        _lin_out_kernel,
        grid=(t // tm,),
        in_specs=[pl.BlockSpec((tm, dm), row), pl.BlockSpec((2, tm, wc), row3), pl.BlockSpec((2, tm, wd), row3),
                  pl.BlockSpec((tm, wc), row), pl.BlockSpec((tm, wd), row),
                  _resident((1, LIN_DV)), _resident((1, LIN_DV)), _resident(w.shape)],
        out_specs=pl.BlockSpec((tm, dm), row),
        out_shape=jax.ShapeDtypeStruct((t, dm), F32),
        compiler_params=_cparams("parallel"),
        name="lin_out",
    )(x, oc, od, gc, gd, cw.reshape(1, LIN_DV), dw.reshape(1, LIN_DV), w)


def _rot_half_cols(w):
    half = w.shape[-1] // 2
    return jnp.concatenate([-w[..., half:], w[..., :half]], axis=-1)


def _pad_heads(w, heads, dim, offset=0):
    k = w.shape[0]
    out = jnp.zeros((k, heads, LANE), w.dtype)
    out = out.at[:, :, offset:offset + dim].set(w.reshape(k, heads, dim))
    return out.reshape(k, heads * LANE)


def _layer0_in_weight(w_in):
    aq = A_HEADS * HEAD_DIM
    akv = A_KV_HEADS * HEAD_DIM
    o = 0
    w_q = w_in[:, o:o + aq]; o += aq
    w_k = w_in[:, o:o + akv]; o += akv
    w_v = w_in[:, o:o + akv]; o += akv
    w_cq = w_in[:, o:o + B_Q_RANK]; o += B_Q_RANK
    w_ckv = w_in[:, o:o + B_KV_RANK]; o += B_KV_RANK
    w_pe = w_in[:, o:o + B_ROPE]
    zeros = jnp.zeros((w_in.shape[0], B_NOPE), w_in.dtype)
    v_lo = _pad_heads(w_v, A_KV_HEADS, HEAD_DIM, 0).reshape(-1, A_KV_HEADS, LANE)
    v_hi = _pad_heads(w_v, A_KV_HEADS, HEAD_DIM, HEAD_DIM).reshape(-1, A_KV_HEADS, LANE)
    w_v2 = jnp.stack([v_lo, v_hi], axis=2).reshape(w_in.shape[0], 2 * A_KV_HEADS * LANE)
    w_pe_rot = _rot_half_cols(w_pe)
    pieces = [_pad_heads(w_q, A_HEADS, HEAD_DIM), _pad_heads(w_k, A_KV_HEADS, HEAD_DIM), w_v2,
              w_cq, w_ckv,
              jnp.concatenate([zeros, w_pe, w_pe, zeros, w_pe_rot, w_pe_rot], axis=1)]
    segs, start = [], 0
    for p in pieces:
        segs.append((start, p.shape[1]))
        start += p.shape[1]
    return jnp.concatenate(pieces, axis=1).astype(BF16), segs


def _mla_weights(w_uq, w_ukv):
    r = w_uq.shape[0]
    wq = w_uq.reshape(r, B_HEADS, B_NOPE + B_ROPE)
    pe = wq[:, :, B_NOPE:]
    wq = jnp.concatenate([wq[:, :, :B_NOPE], pe, _rot_half_cols(pe)], axis=-1).reshape(r, B_HEADS * LANE)
    rk = w_ukv.shape[0]
    wkv = w_ukv.reshape(rk, B_HEADS, B_NOPE + B_VDIM)
    wk = _pad_heads(wkv[:, :, :B_NOPE].reshape(rk, B_HEADS * B_NOPE), B_HEADS, B_NOPE)
    wv = _pad_heads(wkv[:, :, B_NOPE:].reshape(rk, B_HEADS * B_VDIM), B_HEADS, B_VDIM)
    return wq.astype(BF16), wk.astype(BF16), wv.astype(BF16)


def _rope_tables(seq):
    half = B_ROPE // 2
    inv = ROPE_THETA ** (-jnp.arange(half, dtype=F32) / half)
    ang = jnp.arange(seq, dtype=F32)[:, None] * inv[None, :]
    cos2 = jnp.tile(jnp.cos(ang), (1, 4))
    sin2 = jnp.tile(jnp.sin(ang), (1, 4))
    zeros = jnp.zeros((seq, B_NOPE), F32)
    qscale = (B_NOPE + B_ROPE) ** -0.5 * math.log2(math.e)
    q_tab = jnp.concatenate([jnp.ones((seq, B_NOPE), F32), cos2[:, :B_ROPE], sin2[:, :B_ROPE]], axis=1) * qscale
    k_cos = jnp.concatenate([zeros, cos2], axis=1)
    k_sin = jnp.concatenate([zeros, sin2], axis=1)
    return q_tab, k_cos, k_sin


def _layer1_in_weight(w_in):
    hk = LIN_HEADS * LIN_DK
    hv = LIN_HEADS * LIN_DV
    conv_ch = 2 * hk + hv
    main = 2 * hk + 2 * hv + 2 * GATE_RANK
    o = 0
    w_main = w_in[:, :2 * hk + 2 * hv]; o = 2 * hk + 2 * hv
    w_gl = w_in[:, o:o + 2 * GATE_RANK]; o += 2 * GATE_RANK
    w_conv = w_in[:, o:o + conv_ch]; o += conv_ch
    w_z = w_in[:, o:o + hv]; o += hv
    w_small = w_in[:, o:o + 4 * LIN_HEADS]
    del main
    pad = jnp.zeros((w_in.shape[0], LANE - 2 * GATE_RANK - 4 * LIN_HEADS), w_in.dtype)
    w = jnp.concatenate([w_main, w_conv, w_z, w_gl, w_small, pad], axis=1).astype(BF16)
    widths = [hk, hk, hv, hv, conv_ch, hv, LANE]
    segs, start = [], 0
    for wd in widths:
        segs.append((start, wd))
        start += wd
    return w, segs


def _pick_tile(n, pref):
    t = min(n, pref)
    while n % t:
        t //= 2
    return t


def kernel(x, att_norm, att_w_in, att_sink, mla_q_norm, mla_w_uq, mla_kv_norm, mla_w_ukv, att_w_out, lin_norm, lin_w_in, gla_w_gate_f, gla_b_gate_f, gla_w_gate_b, gla_b_gate_b, gla_norm, gdn_conv, gdn_a_log_f, gdn_dt_bias_f, gdn_a_log_b, gdn_dt_bias_b, gdn_norm, lin_w_out, mlp_norm, mlp_w1, mlp_w2, final_norm):
    batch, seq, dm = x.shape
    t = batch * seq
    depth = mlp_norm.shape[0]
    tm = _pick_tile(seq, 512)
    xs = x.reshape(t, dm)
    for layer in range(depth):
        i = layer // 2
        if layer % 2 == 0:
            w0, segs = _layer0_in_weight(att_w_in[i])
            qa, ka, va, cq, ckv, kab = norm_proj(xs, att_norm[i], w0, segs, [BF16] * 6, tm)
            o_a = window_attention(qa, ka, va, att_sink[i].astype(F32), batch, seq)
            wq, wk, wv = _mla_weights(mla_w_uq[i], mla_w_ukv[i])
            q_tab, k_cos, k_sin = _rope_tables(seq)
            qb = mla_q_prep(cq, mla_q_norm[i], wq, q_tab, batch, seq, tm)
            kb, vb = mla_kv_prep(ckv, kab, mla_kv_norm[i], wk, wv, k_cos, k_sin, batch, seq, tm)
            o_b = mla_flash(qb, kb, vb, batch, seq, _pick_tile(seq, 512), _pick_tile(seq, 2048), 64)
            xs = out_proj(xs, o_a, o_b, att_w_out[i].astype(BF16), tm)
        else:
            w1, segs = _layer1_in_weight(lin_w_in[i])
            cq, ck, cv, cg, dqkv, dz, small = norm_proj(xs, lin_norm[i], w1, segs, [BF16] * 6 + [F32], tm)
            hk = LIN_HEADS * LIN_DK
            wg = jnp.zeros((2, LANE, hk), F32)
            wg = wg.at[0, :GATE_RANK].set(gla_w_gate_f[i]).at[1, GATE_RANK:2 * GATE_RANK].set(gla_w_gate_b[i])
            bg = jnp.stack([gla_b_gate_f[i], gla_b_gate_b[i]]).reshape(2, 1, hk).astype(F32)
            tc = _pick_tile(seq, 256)
            o_c = gla_scan(cq, ck, cv, small, wg.astype(BF16), bg, batch, seq, tc)
            gsum = (jnp.arange(hk)[:, None] // LIN_DK == jnp.arange(hk)[None, :] // LIN_DK).astype(F32)
            nh = LIN_HEADS
            a0 = GATE_LANE0 + 2 * nh
            alog = jnp.zeros((LANE,), F32).at[a0:a0 + nh].set(gdn_a_log_f[i]).at[a0 + nh:a0 + 2 * nh].set(gdn_a_log_b[i])
            dtb = jnp.zeros((LANE,), F32).at[a0:a0 + nh].set(gdn_dt_bias_f[i]).at[a0 + nh:a0 + 2 * nh].set(gdn_dt_bias_b[i])
            dq, dk, dv, mixed, bcum, btot = gdn_prep(dqkv, small, gdn_conv[i].astype(F32), gsum,
                                                     alog.reshape(1, LANE), dtb.reshape(1, LANE), batch, seq, tm)
            g0 = GATE_LANE0
            zpad = jnp.zeros((t, LANE - 4 * nh), F32)

            def dir_gates(dirn):
                bsl = slice(g0 + dirn * nh, g0 + (dirn + 1) * nh)
                gsl = slice(g0 + (2 + dirn) * nh, g0 + (3 + dirn) * nh)
                return jnp.concatenate([mixed[:, bsl], mixed[:, gsl], bcum[:, gsl], btot[:, gsl], zpad], axis=1)

            gates = jnp.stack([dir_gates(0), dir_gates(1)])
            o_d = gdn_scan(dq, dk, dv, gates, batch, seq, tc)
            xs = lin_out(xs, o_c, o_d, cg, dz, gla_norm[i], gdn_norm[i], lin_w_out[i].astype(BF16), tm)
        last = layer == depth - 1
        xs = mlp(xs, mlp_norm[layer], mlp_w1[layer].astype(BF16), mlp_w2[layer].astype(BF16),
                 final_norm, last, tm, 1024)
    return xs.reshape(batch, seq, dm)
```

```python
import functools
import math

import numpy as np
import jax
import jax.numpy as jnp
from jax import lax
from jax.experimental import pallas as pl
from jax.experimental.pallas import tpu as pltpu

F32 = jnp.float32
BF16 = jnp.bfloat16
EPS = 1e-6

LANE = 128
SUBLANE = 8
VMEM_LIMIT_BYTES = 56 * 1024 * 1024

HEAD_DIM = 64
A_HEADS = 8
A_KV_HEADS = 2
A_GROUP = A_HEADS // A_KV_HEADS
WINDOW = 128
BLOCK = 128
B_HEADS = 8
B_NOPE = 64
B_ROPE = 32
B_VDIM = 64
B_Q_RANK = 384
B_KV_RANK = 256
ROPE_THETA = 10000.0
LIN_HEADS = 4
LIN_DK = 64
LIN_DV = 128
GATE_RANK = 16
GATE_NORM = 16.0
CHUNK = 64
CONV_TAPS = 5
STACK = LIN_HEADS * CHUNK
GATE_LANE0 = 2 * GATE_RANK

NT_DIMS = (((1,), (1,)), ((), ()))
TN_DIMS = (((0,), (0,)), ((), ()))


def _cparams(*sem):
    return pltpu.CompilerParams(dimension_semantics=sem, vmem_limit_bytes=VMEM_LIMIT_BYTES)


def _resident(shape):
    nd = len(shape)
    return pl.BlockSpec(shape, lambda *_: (0,) * nd, pipeline_mode=pl.Buffered(1))


def _rms(x, w):
    return x * lax.rsqrt(jnp.mean(x * x, axis=-1, keepdims=True) + EPS) * w


def _sigmoid(x):
    return 1.0 / (1.0 + jnp.exp(-x))


def _softplus(x):
    return jnp.maximum(x, 0.0) + jnp.log1p(jnp.exp(-jnp.abs(x)))


def _dot(a, b):
    return jnp.dot(a, b, preferred_element_type=F32)


def _dot_nt(a, b):
    return lax.dot_general(a, b, NT_DIMS, preferred_element_type=F32)


def _dot_tn(a, b):
    return lax.dot_general(a, b, TN_DIMS, preferred_element_type=F32)


def _dot_exact(a, b):
    return jnp.dot(a, b, preferred_element_type=F32, precision=lax.Precision.HIGHEST)


def _dot_mask_f32(mask16, x):
    hi = x.astype(BF16)
    r1 = x - hi.astype(F32)
    mid = r1.astype(BF16)
    lo = (r1 - mid.astype(F32)).astype(BF16)
    return _dot(mask16, hi) + _dot(mask16, mid) + _dot(mask16, lo)


def _norm_proj_kernel(x_ref, nw_ref, w_ref, *out_refs, segs):
    xn = _rms(x_ref[...], nw_ref[...]).astype(BF16)
    for (start, width), o_ref in zip(segs, out_refs):
        o_ref[...] = _dot(xn, w_ref[:, start:start + width]).astype(o_ref.dtype)


def norm_proj(x, norm_w, w, segs, dtypes, tm):
    t, k = x.shape
    n = w.shape[1]
    return pl.pallas_call(
        functools.partial(_norm_proj_kernel, segs=tuple(segs)),
        grid=(t // tm,),
        in_specs=[pl.BlockSpec((tm, k), lambda i: (i, 0)),
                  _resident((1, k)),
                  _resident((k, n))],
        out_specs=[pl.BlockSpec((tm, wd), lambda i: (i, 0)) for _, wd in segs],
        out_shape=[jax.ShapeDtypeStruct((t, wd), dt) for (_, wd), dt in zip(segs, dtypes)],
        compiler_params=_cparams("parallel"),
        name="norm_proj",
    )(x, norm_w.reshape(1, k), w)


def _win_attn_kernel(sink_ref, bias_ref, q_ref, kp_ref, kc_ref, kn_ref, vp_ref, vc_ref, vn_ref, o_ref):
    k3 = jnp.concatenate([kp_ref[...], kc_ref[...], kn_ref[...]], axis=0)
    v3 = jnp.concatenate([vp_ref[...], vc_ref[...], vn_ref[...]], axis=0)
    scale = HEAD_DIM ** -0.5
    heads = range(A_HEADS)
    s = [_dot_nt(q_ref[:, h * LANE:(h + 1) * LANE], k3[:, (h // A_GROUP) * LANE:(h // A_GROUP + 1) * LANE])
         for h in heads]
    s = [s[h] * scale + bias_ref[h] for h in heads]
    m = [jnp.maximum(jnp.max(s[h], axis=-1, keepdims=True), sink_ref[h]) for h in heads]
    p = [jnp.exp(s[h] - m[h]) for h in heads]
    rden = [1.0 / (jnp.sum(p[h], axis=-1, keepdims=True) + jnp.exp(sink_ref[h] - m[h])) for h in heads]
    pv = [_dot(p[h].astype(BF16), v3[:, (2 * (h // A_GROUP) + h % 2) * LANE:(2 * (h // A_GROUP) + h % 2 + 1) * LANE])
          for h in heads]
    for pair in range(A_HEADS // 2):
        h0, h1 = 2 * pair, 2 * pair + 1
        o_ref[:, pair * LANE:(pair + 1) * LANE] = (pv[h0] * rden[h0] + pv[h1] * rden[h1]).astype(o_ref.dtype)


def _window_bias():
    qi = np.arange(BLOCK)[:, None]
    kj = np.arange(3 * BLOCK)[None, :]
    dist = np.abs(qi + BLOCK - kj)
    slopes = np.array([2.0 ** (-8.0 * (h + 1) / A_HEADS) for h in range(A_HEADS)], np.float32)
    base = -slopes[:, None, None] * dist[None].astype(np.float32)
    out = []
    for first in (0, 1):
        for last in (0, 1):
            valid = (dist <= WINDOW) & ((kj >= BLOCK) | (first == 0)) & ((kj < 2 * BLOCK) | (last == 0))
            out.append(np.where(valid[None], base, -np.inf))
    return jnp.asarray(np.stack(out), F32)


def window_attention(q, k, v, sink, batch, seq):
    nb = seq // BLOCK
    cur = lambda b, n: (b * nb + n, 0)
    prev = lambda b, n: (b * nb + jnp.maximum(n - 1, 0), 0)
    nxt = lambda b, n: (b * nb + jnp.minimum(n + 1, nb - 1), 0)
    variant = lambda b, n: (2 * (n == 0).astype(jnp.int32) + (n == nb - 1).astype(jnp.int32), 0, 0, 0)
    kw, vw = k.shape[1], v.shape[1]
    return pl.pallas_call(
        _win_attn_kernel,
        grid=(batch, nb),
        in_specs=[pl.BlockSpec(memory_space=pltpu.SMEM),
                  pl.BlockSpec((None, A_HEADS, BLOCK, 3 * BLOCK), variant),
                  pl.BlockSpec((BLOCK, q.shape[1]), cur),
                  pl.BlockSpec((BLOCK, kw), prev), pl.BlockSpec((BLOCK, kw), cur), pl.BlockSpec((BLOCK, kw), nxt),
                  pl.BlockSpec((BLOCK, vw), prev), pl.BlockSpec((BLOCK, vw), cur), pl.BlockSpec((BLOCK, vw), nxt)],
        out_specs=pl.BlockSpec((BLOCK, A_HEADS * HEAD_DIM), cur),
        out_shape=jax.ShapeDtypeStruct((batch * seq, A_HEADS * HEAD_DIM), BF16),
        compiler_params=_cparams("parallel", "parallel"),
        name="window_attention",
    )(sink, _window_bias(), q, k, k, k, v, v, v)


def _mla_q_kernel(cq_ref, nw_ref, w_ref, tab_ref, o_ref):
    xn = _rms(cq_ref[...].astype(F32), nw_ref[...]).astype(BF16)
    tab = tab_ref[...]
    for h in range(B_HEADS):
        cols = slice(h * LANE, (h + 1) * LANE)
        o_ref[:, cols] = (_dot(xn, w_ref[:, cols]) * tab).astype(o_ref.dtype)


def mla_q_prep(cq, norm_w, w, tab, batch, seq, tm):
    nt = seq // tm
    r = cq.shape[1]
    n = w.shape[1]
    return pl.pallas_call(
        _mla_q_kernel,
        grid=(batch, nt),
        in_specs=[pl.BlockSpec((tm, r), lambda b, i: (b * nt + i, 0)),
                  _resident((1, r)), _resident((r, n)),
                  pl.BlockSpec((tm, LANE), lambda b, i: (i, 0))],
        out_specs=pl.BlockSpec((tm, n), lambda b, i: (b * nt + i, 0)),
        out_shape=jax.ShapeDtypeStruct((batch * seq, n), BF16),
        compiler_params=_cparams("parallel", "parallel"),
        name="mla_q_prep",
    )(cq, norm_w.reshape(1, r), w, tab)


def _mla_kv_kernel(ckv_ref, kab_ref, nw_ref, wk_ref, wv_ref, ctab_ref, stab_ref, k_ref, v_ref):
    xn = _rms(ckv_ref[...].astype(F32), nw_ref[...]).astype(BF16)
    kr = kab_ref[:, :LANE] * ctab_ref[...] + kab_ref[:, LANE:] * stab_ref[...]
    for h in range(B_HEADS):
        cols = slice(h * LANE, (h + 1) * LANE)
        k_ref[:, cols] = (_dot(xn, wk_ref[:, cols]) + kr).astype(k_ref.dtype)
    lane = lax.broadcasted_iota(jnp.int32, v_ref.shape, 1) % LANE
    v_ref[...] = jnp.where(lane == B_VDIM, 1.0, _dot(xn, wv_ref[...])).astype(v_ref.dtype)


def mla_kv_prep(ckv, kab, norm_w, wk, wv, ctab, stab, batch, seq, tm):
    nt = seq // tm
    r = ckv.shape[1]
    row = lambda b, i: (b * nt + i, 0)
    pos = lambda b, i: (i, 0)
    return pl.pallas_call(
        _mla_kv_kernel,
        grid=(batch, nt),
        in_specs=[pl.BlockSpec((tm, r), row), pl.BlockSpec((tm, 2 * LANE), row),
                  _resident((1, r)), _resident(wk.shape), _resident(wv.shape),
                  pl.BlockSpec((tm, LANE), pos), pl.BlockSpec((tm, LANE), pos)],
        out_specs=[pl.BlockSpec((tm, wk.shape[1]), row), pl.BlockSpec((tm, wv.shape[1]), row)],
        out_shape=[jax.ShapeDtypeStruct((batch * seq, wk.shape[1]), BF16),
                   jax.ShapeDtypeStruct((batch * seq, wv.shape[1]), BF16)],
        compiler_params=_cparams("parallel", "parallel"),
        name="mla_kv_prep",
    )(ckv, kab, norm_w.reshape(1, r), wk, wv, ctab, stab)


def _mla_flash_kernel(q_ref, k_ref, v_ref, o_ref, s_ref, p_ref, m_ref, a_ref, acc_ref, *, tk, nk, rb):
    tq = q_ref.shape[0]
    heads = range(2)
    m_ref[...] = jnp.full(m_ref.shape, -jnp.inf, F32)
    acc_ref[...] = jnp.zeros(acc_ref.shape, F32)

    def body(j, carry):
        rows = pl.ds(pl.multiple_of(j * tk, tk), tk)
        for e in heads:
            cols = slice(e * LANE, (e + 1) * LANE)
            s_ref[e] = _dot_nt(q_ref[:, cols], k_ref[rows, cols])
        for e in heads:
            cols = slice(e * LANE, (e + 1) * LANE)
            for r in range(tq // rb):
                rs = slice(r * rb, (r + 1) * rb)
                s = s_ref[e, rs, :]
                m_old = m_ref[e, rs, :]
                m_new = jnp.maximum(m_old, jnp.max(s, axis=-1, keepdims=True))
                m_ref[e, rs, :] = m_new
                a_ref[e, rs, :] = jnp.exp2(m_old - m_new)
                p_ref[e, rs, :] = jnp.exp2(s - jnp.concatenate([m_new] * (tk // LANE), axis=1)).astype(BF16)
            acc_ref[e] = a_ref[e] * acc_ref[e] + _dot(p_ref[e], v_ref[rows, cols])
        return carry

    lax.fori_loop(0, nk, body, 0)
    outs = []
    for e in heads:
        acc = acc_ref[e]
        outs.append(acc / acc[:, B_VDIM:B_VDIM + 1])
    lane = lax.broadcasted_iota(jnp.int32, (tq, LANE), 1)
    o_ref[...] = jnp.where(lane < B_VDIM, outs[0], pltpu.roll(outs[1], B_VDIM, axis=1)).astype(o_ref.dtype)


def mla_flash(q, k, v, batch, seq, tq, tk, rb):
    nq = seq // tq
    npair = B_HEADS // 2
    return pl.pallas_call(
        functools.partial(_mla_flash_kernel, tk=tk, nk=seq // tk, rb=rb),
        grid=(batch, npair, nq),
        in_specs=[pl.BlockSpec((tq, 2 * LANE), lambda b, p, i: (b * nq + i, p)),
                  pl.BlockSpec((seq, 2 * LANE), lambda b, p, i: (b, p)),
                  pl.BlockSpec((seq, 2 * LANE), lambda b, p, i: (b, p))],
        out_specs=pl.BlockSpec((tq, LANE), lambda b, p, i: (b * nq + i, p)),
        out_shape=jax.ShapeDtypeStruct((batch * seq, B_HEADS * B_VDIM), BF16),
        scratch_shapes=[pltpu.VMEM((2, tq, tk), F32), pltpu.VMEM((2, tq, tk), BF16), pltpu.VMEM((2, tq, LANE), F32),
                        pltpu.VMEM((2, tq, LANE), F32), pltpu.VMEM((2, tq, LANE), F32)],
        compiler_params=_cparams("parallel", "parallel", "parallel"),
        name="mla_flash",
    )(q, k, v)


def _mlp_tail(x, nw_ref, w1_ref, w2_ref, fw_ref, o_ref, fchunk, final_norm):
    xn = _rms(x, nw_ref[...]).astype(BF16)
    acc = x
    for f in range(w1_ref.shape[1] // fchunk):
        cols = slice(f * fchunk, (f + 1) * fchunk)
        h = jnp.square(jnp.maximum(_dot(xn, w1_ref[:, cols]), 0.0)).astype(BF16)
        acc = acc + _dot(h, w2_ref[cols, :])
    if final_norm:
        acc = _rms(acc, fw_ref[...])
    o_ref[...] = acc


def _attn_out_mlp_kernel(x_ref, a_ref, b_ref, wo_ref, nw_ref, w1_ref, w2_ref, fw_ref, o_ref, *, fchunk, final_norm):
    ka = a_ref.shape[1]
    o_ref[...] = x_ref[...] + _dot(a_ref[...], wo_ref[:ka, :]) + _dot(b_ref[...], wo_ref[ka:, :])
    _mlp_tail(o_ref[...], nw_ref, w1_ref, w2_ref, fw_ref, o_ref, fchunk, final_norm)


def _lin_out_mlp_kernel(x_ref, cf_ref, cb_ref, df_ref, db_ref, gc_ref, gd_ref, cw_ref, dw_ref, wo_ref,
                        nw_ref, w1_ref, w2_ref, fw_ref, o_ref, *, fchunk, final_norm):
    def gated(f_ref, b_ref, gate_ref, gnw_ref):
        o = f_ref[...] + b_ref[...]
        parts = []
        for h in range(LIN_HEADS):
            cols = slice(h * LIN_DV, (h + 1) * LIN_DV)
            gate = gate_ref[:, cols].astype(F32)
            parts.append((_rms(o[:, cols], gnw_ref[...]) * (gate * _sigmoid(gate))).astype(BF16))
        return jnp.concatenate(parts, axis=1)

    kc = cf_ref.shape[1]
    o_ref[...] = (x_ref[...] + _dot(gated(cf_ref, cb_ref, gc_ref, cw_ref), wo_ref[:kc, :])
                  + _dot(gated(df_ref, db_ref, gd_ref, dw_ref), wo_ref[kc:, :]))
    _mlp_tail(o_ref[...], nw_ref, w1_ref, w2_ref, fw_ref, o_ref, fchunk, final_norm)


def mixer_out_mlp(x, mixer_inputs, small_params, w_out, norm_w, w1, w2, final_w, final_norm, tm, fchunk):
    t, d = x.shape
    row = lambda i: (i, 0)
    body = _attn_out_mlp_kernel if len(mixer_inputs) == 2 else _lin_out_mlp_kernel
    return pl.pallas_call(
        functools.partial(body, fchunk=fchunk, final_norm=final_norm),
        grid=(t // tm,),
        in_specs=([pl.BlockSpec((tm, d), row)] + [pl.BlockSpec((tm, a.shape[1]), row) for a in mixer_inputs]
                  + [_resident(p.shape) for p in small_params]
                  + [_resident(w_out.shape), _resident((1, d)), _resident(w1.shape), _resident(w2.shape),
                     _resident((1, d))]),
        out_specs=pl.BlockSpec((tm, d), row),
        out_shape=jax.ShapeDtypeStruct((t, d), F32),
        compiler_params=_cparams("parallel"),
        name="mixer_out_mlp",
    )(x, *mixer_inputs, *small_params, w_out, norm_w.reshape(1, d), w1, w2, final_w.reshape(1, d))


def _direction_masks(d):
    sgn = 1 - 2 * d
    r = lax.broadcasted_iota(jnp.int32, (STACK, STACK), 0)
    c = lax.broadcasted_iota(jnp.int32, (STACK, STACK), 1)
    same = (r & -CHUNK) == (c & -CHUNK)
    order = ((r & (CHUNK - 1)) - (c & (CHUNK - 1))) * sgn
    i = lax.broadcasted_iota(jnp.int32, (CHUNK, CHUNK), 0)
    j = lax.broadcasted_iota(jnp.int32, (CHUNK, CHUNK), 1)
    cum = (((i - j) * sgn) >= 0).astype(F32)
    return same, same & (order >= 0), same & (order > 0), cum


def _block_diag(x, same):
    return jnp.where(same, jnp.concatenate([x] * LIN_HEADS, axis=0), 0.0)


def _stack_heads(v, width):
    return jnp.concatenate([v[:, h * width:(h + 1) * width] for h in range(LIN_HEADS)], axis=0)


def _each(f, *lists):
    return [f(*args) for args in zip(*lists)]


def _scan_order(ncb):
    return [((0, i), (1, ncb - 1 - i)) for i in range(ncb)]


def _bidir_specs(shapes, nblk, tc):
    fwd = lambda b, n: (b * nblk + n, 0)
    bwd = lambda b, n: (b * nblk + nblk - 1 - n, 0)
    return ([pl.BlockSpec((tc, w), fwd) for w in shapes], [pl.BlockSpec((tc, w), bwd) for w in shapes])


def _gla_kernel(qf_ref, kf_ref, vf_ref, sf_ref, qb_ref, kb_ref, vb_ref, sb_ref, wg_ref, bg_ref,
                of_ref, ob_ref, st_ref, *, ncb):
    @pl.when(pl.program_id(1) == 0)
    def _():
        st_ref[...] = jnp.zeros_like(st_ref)

    refs = ((qf_ref, kf_ref, vf_ref, sf_ref, of_ref), (qb_ref, kb_ref, vb_ref, sb_ref, ob_ref))
    masks = [_direction_masks(d) for d in range(2)]
    items = [(d, ci) for d in range(2) for ci in range(ncb)]
    rows = lambda ci: slice(ci * CHUNK, (ci + 1) * CHUNK)
    same = [masks[d][0] for d, _ in items]
    z = [_dot(refs[d][3][rows(ci), :].astype(BF16), wg_ref[d]) + bg_ref[d] for d, ci in items]
    log_a = _each(lambda a: (jnp.minimum(a, 0.0) - jnp.log1p(jnp.exp(-jnp.abs(a)))) * (1.0 / GATE_NORM), z)
    b = [_dot_mask_f32(masks[d][3].astype(BF16), a) for (d, _), a in zip(items, log_a)]
    b_last = _each(lambda a: jnp.sum(a, axis=0, keepdims=True), log_a)
    q_in = [_block_diag(refs[d][0][rows(ci), :] * (LIN_DK ** -0.5) * jnp.exp(bb), sm).astype(BF16)
            for (d, ci), bb, sm in zip(items, b, same)]
    k = [refs[d][1][rows(ci), :] for d, ci in items]
    k_in = _each(lambda a, bb, sm: _block_diag(a * jnp.exp(-bb), sm).astype(BF16), k, b, same)
    k_out = _each(lambda a, bb, bl, sm: _block_diag(a * jnp.exp(bl - bb), sm).astype(BF16), k, b, b_last, same)
    v = [_stack_heads(refs[d][2][rows(ci), :], LIN_DV).astype(BF16) for d, ci in items]
    sc = [jnp.where(masks[d][1], _dot_nt(a, bb), 0.0).astype(BF16) for (d, _), a, bb in zip(items, q_in, k_in)]
    o_intra = dict(zip(items, _each(_dot, sc, v)))
    ds = dict(zip(items, _each(_dot_tn, v, k_out)))
    dec = dict(zip(items, _each(jnp.exp, b_last)))
    q_in = dict(zip(items, q_in))

    st = [st_ref[0], st_ref[1]]
    for step in _scan_order(ncb):
        o = [o_intra[it] + _dot_nt(q_in[it], st[it[0]].astype(BF16)) for it in step]
        for (d, ci), od in zip(step, o):
            st[d] = dec[(d, ci)] * st[d] + ds[(d, ci)]
            for h in range(LIN_HEADS):
                refs[d][4][rows(ci), h * LIN_DV:(h + 1) * LIN_DV] = od[h * CHUNK:(h + 1) * CHUNK, :]
    st_ref[0] = st[0]
    st_ref[1] = st[1]


def gla_scan(q, k, v, small, wg, bg, batch, seq, tc):
    nblk = seq // tc
    t = batch * seq
    hk, hv = q.shape[1], v.shape[1]
    fwd, bwd = _bidir_specs((hk, hk, hv, LANE), nblk, tc)
    ofwd, obwd = _bidir_specs((hv,), nblk, tc)
    return pl.pallas_call(
        functools.partial(_gla_kernel, ncb=tc // CHUNK),
        grid=(batch, nblk),
        in_specs=fwd + bwd + [_resident(wg.shape), _resident(bg.shape)],
        out_specs=ofwd + obwd,
        out_shape=[jax.ShapeDtypeStruct((t, hv), F32)] * 2,
        scratch_shapes=[pltpu.VMEM((2, LIN_DV, hk), F32)],
        compiler_params=_cparams("parallel", "arbitrary"),
        name="gla_scan",
    )(q, k, v, small, q, k, v, small, wg, bg)


def _gdn_prep_kernel(xp_ref, xc_ref, xn_ref, sm_ref, cw_ref, gsum_ref, alog_ref, dtb_ref,
                     q_ref, k_ref, v_ref, g_ref, b_ref, bt_ref, buf_ref, *, nt):
    i = pl.program_id(1)
    tm = xc_ref.shape[0]
    halo = xp_ref.shape[0]
    buf_ref[:halo, :] = xp_ref[...].astype(F32) * (i > 0).astype(F32)
    buf_ref[halo:halo + tm, :] = xc_ref[...].astype(F32)
    buf_ref[halo + tm:, :] = xn_ref[...].astype(F32) * (i < nt - 1).astype(F32)
    acc = jnp.zeros(xc_ref.shape, F32)
    for j in range(CONV_TAPS):
        off = halo + j - CONV_TAPS // 2
        acc = acc + buf_ref[off:off + tm, :] * cw_ref[j:j + 1, :]
    y = acc * _sigmoid(acc)
    nqk = LIN_HEADS * LIN_DK
    for idx, (o_ref, scale) in enumerate(((q_ref, LIN_DK ** -0.5), (k_ref, 1.0))):
        part = y[:, idx * nqk:(idx + 1) * nqk]
        ssq = _dot_exact(part * part, gsum_ref[...])
        o_ref[...] = (part * lax.rsqrt(ssq + EPS) * scale).astype(o_ref.dtype)
    v_ref[...] = y[:, 2 * nqk:].astype(v_ref.dtype)
    sm = sm_ref[...]
    beta = _sigmoid(sm)
    g = -jnp.exp(alog_ref[...]) * _softplus(sm + dtb_ref[...])
    lane = lax.broadcasted_iota(jnp.int32, sm.shape, 1) - GATE_LANE0
    is_beta = (lane >= 0) & (lane < 2 * LIN_HEADS)
    is_g = (lane >= 2 * LIN_HEADS) & (lane < 4 * LIN_HEADS)
    g = jnp.where(is_g, g, 0.0)
    g_ref[...] = jnp.where(is_beta, beta, g)
    ri = lax.broadcasted_iota(jnp.int32, (CHUNK, CHUNK), 0)
    ci = lax.broadcasted_iota(jnp.int32, (CHUNK, CHUNK), 1)
    tril16 = (ri >= ci).astype(BF16)
    ones16 = jnp.ones((CHUNK, CHUNK), BF16)
    is_bwd = lax.broadcasted_iota(jnp.int32, (CHUNK, LANE), 1) >= GATE_LANE0 + 3 * LIN_HEADS
    for c in range(tm // CHUNK):
        rows = slice(c * CHUNK, (c + 1) * CHUNK)
        gc = g[rows, :]
        prefix = _dot_mask_f32(tril16, gc)
        total = _dot_mask_f32(ones16, gc)
        b_ref[rows, :] = jnp.where(is_bwd, total - prefix + gc, prefix)
        bt_ref[rows, :] = total


def gdn_prep(dqkv, small, conv_w, gsum, alog, dtb, batch, seq, tm):
    nt = seq // tm
    t, c = dqkv.shape
    halo = 2 * SUBLANE
    hb = tm // halo
    nqk = LIN_HEADS * LIN_DK
    row = lambda b, i: (b * nt + i, 0)
    prev = lambda b, i: (jnp.maximum((b * nt + i) * hb - 1, 0), 0)
    nxt = lambda b, i: (jnp.minimum((b * nt + i + 1) * hb, t // halo - 1), 0)
    return pl.pallas_call(
        functools.partial(_gdn_prep_kernel, nt=nt),
        grid=(batch, nt),
        in_specs=[pl.BlockSpec((halo, c), prev), pl.BlockSpec((tm, c), row), pl.BlockSpec((halo, c), nxt),
                  pl.BlockSpec((tm, LANE), row),
                  _resident(conv_w.shape), _resident(gsum.shape), _resident((1, LANE)), _resident((1, LANE))],
        out_specs=[pl.BlockSpec((tm, nqk), row), pl.BlockSpec((tm, nqk), row),
                   pl.BlockSpec((tm, c - 2 * nqk), row)] + [pl.BlockSpec((tm, LANE), row)] * 3,
        out_shape=[jax.ShapeDtypeStruct((t, nqk), BF16), jax.ShapeDtypeStruct((t, nqk), BF16),
                   jax.ShapeDtypeStruct((t, c - 2 * nqk), BF16)] + [jax.ShapeDtypeStruct((t, LANE), F32)] * 3,
        scratch_shapes=[pltpu.VMEM((tm + 2 * halo, c), F32)],
        compiler_params=_cparams("parallel", "parallel"),
        name="gdn_prep",
    )(dqkv, dqkv, dqkv, small, conv_w, gsum, alog, dtb)


def _gdn_kernel(qf_ref, kf_ref, vf_ref, gf_ref, bf_ref, tf_ref, qb_ref, kb_ref, vb_ref, gb_ref, bb_ref, tb_ref,
                of_ref, ob_ref, st_ref, *, ncb):
    @pl.when(pl.program_id(1) == 0)
    def _():
        st_ref[...] = jnp.zeros_like(st_ref)

    refs = ((qf_ref, kf_ref, vf_ref, gf_ref, bf_ref, tf_ref, of_ref),
            (qb_ref, kb_ref, vb_ref, gb_ref, bb_ref, tb_ref, ob_ref))
    masks = [_direction_masks(d) for d in range(2)]
    r = lax.broadcasted_iota(jnp.int32, (STACK, STACK), 0)
    c = lax.broadcasted_iota(jnp.int32, (STACK, STACK), 1)
    diag = r == c
    eye = diag.astype(F32)
    lane_blk = lax.broadcasted_iota(jnp.int32, (SUBLANE, STACK), 1) // CHUNK

    def rows_of(x, lane0):
        return jnp.concatenate(
            [jnp.broadcast_to(x[:, lane0 + h:lane0 + h + 1], (CHUNK, STACK)) for h in range(LIN_HEADS)], axis=0)

    def head_lanes(row, lane0):
        out = jnp.broadcast_to(row[:, lane0:lane0 + 1], (SUBLANE, STACK))
        for h in range(1, LIN_HEADS):
            out = jnp.where(lane_blk == h, jnp.broadcast_to(row[:, lane0 + h:lane0 + h + 1], (SUBLANE, STACK)), out)
        return out

    each = _each
    items = [(d, ci) for d in range(2) for ci in range(ncb)]
    cs = range(len(items))
    rows = lambda ci: slice(ci * CHUNK, (ci + 1) * CHUNK)
    same = [masks[d][0] for d, _ in items]
    beta_col = [rows_of(refs[d][3][rows(ci), :], _gate_lanes(d)[0]) for d, ci in items]
    b_col = [rows_of(refs[d][4][rows(ci), :], _gate_lanes(d)[1]) for d, ci in items]
    bl_col = [rows_of(refs[d][5][rows(ci), :], _gate_lanes(d)[1]) for d, ci in items]
    k_bd = [_block_diag(refs[d][1][rows(ci), :], sm) for (d, ci), sm in zip(items, same)]
    q_bd = [_block_diag(refs[d][0][rows(ci), :], sm) for (d, ci), sm in zip(items, same)]
    kb_bd = each(lambda a, b: a * b, k_bd, beta_col)
    k16 = each(lambda a: a.astype(BF16), k_bd)
    kk = each(lambda a, b: _dot_nt(a.astype(BF16), b), kb_bd, k16)
    qk = each(lambda a, b: _dot_nt(a.astype(BF16), b), q_bd, k16)
    b_row = each(lambda bc: jnp.sum(jnp.where(diag, bc, 0.0), axis=0, keepdims=True), b_col)
    gamma = [jnp.exp(jnp.where(masks[d][1], bc - br, -jnp.inf)) for (d, _), bc, br in zip(items, b_col, b_row)]
    lower = [jnp.where(masks[d][2], a * g, 0.0) for (d, _), a, g in zip(items, kk, gamma)]
    attn = each(lambda a, g: (a * g).astype(BF16), qk, gamma)
    l_hi = each(lambda a: a.astype(BF16), lower)
    pw = [l_hi]
    for _ in range(5):
        pw.append(each(lambda a: _dot(a, a).astype(BF16), pw[-1]))

    def pair(a, b16):
        return a + _dot(a.astype(BF16), b16)

    f01 = each(lambda lo, p1: pair(eye - lo, p1), lower, pw[1])
    f23 = each(lambda p2, p3: pair(eye + p2.astype(F32), p3), pw[2], pw[3])
    f45 = each(lambda p4, p5: pair(eye + p4.astype(F32), p5), pw[4], pw[5])
    f0123 = each(lambda a, b: _dot(a.astype(BF16), b.astype(BF16)).astype(BF16), f01, f23)
    inv16 = each(lambda a, b: _dot(a, b.astype(BF16)).astype(BF16), f0123, f45)
    e_b = each(jnp.exp, b_col)
    vb = [_stack_heads(refs[d][2][rows(ci), :], LIN_DV) * bc[:, :LIN_DV] for (d, ci), bc in zip(items, beta_col)]
    rhs = each(lambda v, kb, e: jnp.concatenate([v, kb * e], axis=1), vb, kb_bd, e_b)
    x16 = each(lambda m, r_: _dot(m, r_.astype(BF16)).astype(BF16), inv16, rhs)
    l_lo = each(lambda lo, hi: (lo - hi.astype(F32)).astype(BF16), lower, l_hi)
    resid = each(lambda r_, x, hi, lo: r_ - (x.astype(F32) + _dot(hi, x) + _dot(lo, x)), rhs, x16, l_hi, l_lo)
    sol = each(lambda x, m, rs: x.astype(F32) + _dot(m, rs.astype(BF16)), x16, inv16, resid)
    u = {it: sol[i][:, :LIN_DV] for i, it in enumerate(items)}
    w = {it: sol[i][:, LIN_DV:].astype(BF16) for i, it in enumerate(items)}
    attn = dict(zip(items, attn))
    q_dec = {it: (q_bd[i] * e_b[i]).astype(BF16) for i, it in enumerate(items)}
    k_dec = {it: (k_bd[i] * jnp.exp(bl_col[i] - b_col[i])).astype(BF16) for i, it in enumerate(items)}
    dec = {(d, ci): jnp.exp(head_lanes(refs[d][5][ci * CHUNK:ci * CHUNK + SUBLANE, :], _gate_lanes(d)[1]))[:1, :]
           for d, ci in items}

    st = [st_ref[0], st_ref[1]]
    for step in _scan_order(ncb):
        st16 = [st[d].astype(BF16) for d, _ in step]
        v_new16 = [(u[it] - _dot_nt(w[it], s16)).astype(BF16) for it, s16 in zip(step, st16)]
        o = [_dot_nt(q_dec[it], s16) + _dot(attn[it], vn) for it, s16, vn in zip(step, st16, v_new16)]
        for (d, ci), od, vn in zip(step, o, v_new16):
            st[d] = dec[(d, ci)] * st[d] + _dot_tn(vn, k_dec[(d, ci)])
            for h in range(LIN_HEADS):
                refs[d][6][rows(ci), h * LIN_DV:(h + 1) * LIN_DV] = od[h * CHUNK:(h + 1) * CHUNK, :]
    st_ref[0] = st[0]
    st_ref[1] = st[1]


def _gate_lanes(d):
    return GATE_LANE0 + d * LIN_HEADS, GATE_LANE0 + (2 + d) * LIN_HEADS


def gdn_scan(q, k, v, mixed, bcum, btot, batch, seq, tc):
    nblk = seq // tc
    t = batch * seq
    hk, hv = q.shape[1], v.shape[1]
    fwd, bwd = _bidir_specs((hk, hk, hv, LANE, LANE, LANE), nblk, tc)
    ofwd, obwd = _bidir_specs((hv,), nblk, tc)
    return pl.pallas_call(
        functools.partial(_gdn_kernel, ncb=tc // CHUNK),
        grid=(batch, nblk),
        in_specs=fwd + bwd,
        out_specs=ofwd + obwd,
        out_shape=[jax.ShapeDtypeStruct((t, hv), F32)] * 2,
        scratch_shapes=[pltpu.VMEM((2, LIN_DV, hk), F32)],
        compiler_params=_cparams("parallel", "arbitrary"),
        name="gdn_scan",
    )(q, k, v, mixed, bcum, btot, q, k, v, mixed, bcum, btot)


def _rot_half_cols(w):
    half = w.shape[-1] // 2
    return jnp.concatenate([-w[..., half:], w[..., :half]], axis=-1)


def _pad_heads(w, heads, dim, offset=0):
    k = w.shape[0]
    out = jnp.zeros((k, heads, LANE), w.dtype)
    out = out.at[:, :, offset:offset + dim].set(w.reshape(k, heads, dim))
    return out.reshape(k, heads * LANE)


def _layer0_in_weight(w_in):
    aq = A_HEADS * HEAD_DIM
    akv = A_KV_HEADS * HEAD_DIM
    o = 0
    w_q = w_in[:, o:o + aq]; o += aq
    w_k = w_in[:, o:o + akv]; o += akv
    w_v = w_in[:, o:o + akv]; o += akv
    w_cq = w_in[:, o:o + B_Q_RANK]; o += B_Q_RANK
    w_ckv = w_in[:, o:o + B_KV_RANK]; o += B_KV_RANK
    w_pe = w_in[:, o:o + B_ROPE]
    zeros = jnp.zeros((w_in.shape[0], B_NOPE), w_in.dtype)
    v_lo = _pad_heads(w_v, A_KV_HEADS, HEAD_DIM, 0).reshape(-1, A_KV_HEADS, LANE)
    v_hi = _pad_heads(w_v, A_KV_HEADS, HEAD_DIM, HEAD_DIM).reshape(-1, A_KV_HEADS, LANE)
    w_v2 = jnp.stack([v_lo, v_hi], axis=2).reshape(w_in.shape[0], 2 * A_KV_HEADS * LANE)
    w_pe_rot = _rot_half_cols(w_pe)
    pieces = [_pad_heads(w_q, A_HEADS, HEAD_DIM), _pad_heads(w_k, A_KV_HEADS, HEAD_DIM), w_v2,
              w_cq, w_ckv,
              jnp.concatenate([zeros, w_pe, w_pe, zeros, w_pe_rot, w_pe_rot], axis=1)]
    segs, start = [], 0
    for p in pieces:
        segs.append((start, p.shape[1]))
        start += p.shape[1]
    return jnp.concatenate(pieces, axis=1).astype(BF16), segs


def _mla_weights(w_uq, w_ukv):
    r = w_uq.shape[0]
    wq = w_uq.reshape(r, B_HEADS, B_NOPE + B_ROPE)
    pe = wq[:, :, B_NOPE:]
    wq = jnp.concatenate([wq[:, :, :B_NOPE], pe, _rot_half_cols(pe)], axis=-1).reshape(r, B_HEADS * LANE)
    rk = w_ukv.shape[0]
    wkv = w_ukv.reshape(rk, B_HEADS, B_NOPE + B_VDIM)
    wk = _pad_heads(wkv[:, :, :B_NOPE].reshape(rk, B_HEADS * B_NOPE), B_HEADS, B_NOPE)
    wv = _pad_heads(wkv[:, :, B_NOPE:].reshape(rk, B_HEADS * B_VDIM), B_HEADS, B_VDIM)
    return wq.astype(BF16), wk.astype(BF16), wv.astype(BF16)


def _rope_tables(seq):
    half = B_ROPE // 2
    inv = ROPE_THETA ** (-jnp.arange(half, dtype=F32) / half)
    ang = jnp.arange(seq, dtype=F32)[:, None] * inv[None, :]
    cos2 = jnp.tile(jnp.cos(ang), (1, 4))
    sin2 = jnp.tile(jnp.sin(ang), (1, 4))
    zeros = jnp.zeros((seq, B_NOPE), F32)
    qscale = (B_NOPE + B_ROPE) ** -0.5 * math.log2(math.e)
    q_tab = jnp.concatenate([jnp.ones((seq, B_NOPE), F32), cos2[:, :B_ROPE], sin2[:, :B_ROPE]], axis=1) * qscale
    k_cos = jnp.concatenate([zeros, cos2], axis=1)
    k_sin = jnp.concatenate([zeros, sin2], axis=1)
    return q_tab, k_cos, k_sin


def _layer1_in_weight(w_in):
    hk = LIN_HEADS * LIN_DK
    hv = LIN_HEADS * LIN_DV
    conv_ch = 2 * hk + hv
    main = 2 * hk + 2 * hv + 2 * GATE_RANK
    o = 0
    w_main = w_in[:, :2 * hk + 2 * hv]; o = 2 * hk + 2 * hv
    w_gl = w_in[:, o:o + 2 * GATE_RANK]; o += 2 * GATE_RANK
    w_conv = w_in[:, o:o + conv_ch]; o += conv_ch
    w_z = w_in[:, o:o + hv]; o += hv
    w_small = w_in[:, o:o + 4 * LIN_HEADS]
    del main
    pad = jnp.zeros((w_in.shape[0], LANE - 2 * GATE_RANK - 4 * LIN_HEADS), w_in.dtype)
    w = jnp.concatenate([w_main, w_conv, w_z, w_gl, w_small, pad], axis=1).astype(BF16)
    widths = [hk, hk, hv, hv, conv_ch, hv, LANE]
    segs, start = [], 0
    for wd in widths:
        segs.append((start, wd))
        start += wd
    return w, segs


def _pick_tile(n, pref):
    t = min(n, pref)
    while n % t:
        t //= 2
    return t


def kernel(x, att_norm, att_w_in, att_sink, mla_q_norm, mla_w_uq, mla_kv_norm, mla_w_ukv, att_w_out, lin_norm, lin_w_in, gla_w_gate_f, gla_b_gate_f, gla_w_gate_b, gla_b_gate_b, gla_norm, gdn_conv, gdn_a_log_f, gdn_dt_bias_f, gdn_a_log_b, gdn_dt_bias_b, gdn_norm, lin_w_out, mlp_norm, mlp_w1, mlp_w2, final_norm):
    batch, seq, dm = x.shape
    t = batch * seq
    depth = mlp_norm.shape[0]
    tm = _pick_tile(seq, 512)
    xs = x.reshape(t, dm)
    for layer in range(depth):
        i = layer // 2
        if layer % 2 == 0:
            w0, segs = _layer0_in_weight(att_w_in[i])
            qa, ka, va, cq, ckv, kab = norm_proj(xs, att_norm[i], w0, segs, [BF16] * 6, tm)
            o_a = window_attention(qa, ka, va, att_sink[i].astype(F32), batch, seq)
            wq, wk, wv = _mla_weights(mla_w_uq[i], mla_w_ukv[i])
            q_tab, k_cos, k_sin = _rope_tables(seq)
            qb = mla_q_prep(cq, mla_q_norm[i], wq, q_tab, batch, seq, tm)
            kb, vb = mla_kv_prep(ckv, kab, mla_kv_norm[i], wk, wv, k_cos, k_sin, batch, seq, tm)
            o_b = mla_flash(qb, kb, vb, batch, seq, _pick_tile(seq, 1024), _pick_tile(seq, 2048), 64)
            mixer_inputs, small_params, w_out = [o_a, o_b], [], att_w_out[i]
        else:
            w1, segs = _layer1_in_weight(lin_w_in[i])
            cq, ck, cv, cg, dqkv, dz, small = norm_proj(xs, lin_norm[i], w1, segs, [BF16] * 6 + [F32], tm)
            hk = LIN_HEADS * LIN_DK
            wg = jnp.zeros((2, LANE, hk), F32)
            wg = wg.at[0, :GATE_RANK].set(gla_w_gate_f[i]).at[1, GATE_RANK:2 * GATE_RANK].set(gla_w_gate_b[i])
            bg = jnp.stack([gla_b_gate_f[i], gla_b_gate_b[i]]).reshape(2, 1, hk).astype(F32)
            tc = _pick_tile(seq, 256)
            o_cf, o_cb = gla_scan(cq, ck, cv, small, wg.astype(BF16), bg, batch, seq, tc)
            gsum = (jnp.arange(hk)[:, None] // LIN_DK == jnp.arange(hk)[None, :] // LIN_DK).astype(F32)
            nh = LIN_HEADS
            a0 = GATE_LANE0 + 2 * nh
            alog = jnp.zeros((LANE,), F32).at[a0:a0 + nh].set(gdn_a_log_f[i]).at[a0 + nh:a0 + 2 * nh].set(gdn_a_log_b[i])
            dtb = jnp.zeros((LANE,), F32).at[a0:a0 + nh].set(gdn_dt_bias_f[i]).at[a0 + nh:a0 + 2 * nh].set(gdn_dt_bias_b[i])
            dq, dk, dv, mixed, bcum, btot = gdn_prep(dqkv, small, gdn_conv[i].astype(F32), gsum,
                                                     alog.reshape(1, LANE), dtb.reshape(1, LANE), batch, seq, tm)
            o_df, o_db = gdn_scan(dq, dk, dv, mixed, bcum, btot, batch, seq, tc)
            mixer_inputs = [o_cf, o_cb, o_df, o_db, cg, dz]
            small_params = [gla_norm[i].reshape(1, LIN_DV), gdn_norm[i].reshape(1, LIN_DV)]
            w_out = lin_w_out[i]
        last = layer == depth - 1
        xs = mixer_out_mlp(xs, mixer_inputs, small_params, w_out.astype(BF16), mlp_norm[layer],
                           mlp_w1[layer].astype(BF16), mlp_w2[layer].astype(BF16), final_norm, last, tm, 1024)
    return xs.reshape(batch, seq, dm)
```

```python
import functools
import math

import numpy as np
import jax
import jax.numpy as jnp
from jax import lax
from jax.experimental import pallas as pl
from jax.experimental.pallas import tpu as pltpu

F32 = jnp.float32
BF16 = jnp.bfloat16
EPS = 1e-6

LANE = 128
SUBLANE = 8
VMEM_LIMIT_BYTES = 56 * 1024 * 1024

HEAD_DIM = 64
A_HEADS = 8
A_KV_HEADS = 2
A_GROUP = A_HEADS // A_KV_HEADS
WINDOW = 128
BLOCK = 128
B_HEADS = 8
B_NOPE = 64
B_ROPE = 32
B_VDIM = 64
B_Q_RANK = 384
B_KV_RANK = 256
ROPE_THETA = 10000.0
LIN_HEADS = 4
LIN_DK = 64
LIN_DV = 128
GATE_RANK = 16
GATE_NORM = 16.0
CHUNK = 64
CONV_TAPS = 5
STACK = LIN_HEADS * CHUNK
GATE_LANE0 = 2 * GATE_RANK

NT_DIMS = (((1,), (1,)), ((), ()))
TN_DIMS = (((0,), (0,)), ((), ()))


def _cparams(*sem):
    return pltpu.CompilerParams(dimension_semantics=sem, vmem_limit_bytes=VMEM_LIMIT_BYTES)


def _resident(shape):
    nd = len(shape)
    return pl.BlockSpec(shape, lambda *_: (0,) * nd, pipeline_mode=pl.Buffered(1))


def _rms(x, w):
    return x * lax.rsqrt(jnp.mean(x * x, axis=-1, keepdims=True) + EPS) * w


def _sigmoid(x):
    return 1.0 / (1.0 + jnp.exp(-x))


def _softplus(x):
    return jnp.maximum(x, 0.0) + jnp.log1p(jnp.exp(-jnp.abs(x)))


def _dot(a, b):
    return jnp.dot(a, b, preferred_element_type=F32)


def _dot_nt(a, b):
    return lax.dot_general(a, b, NT_DIMS, preferred_element_type=F32)


def _dot_tn(a, b):
    return lax.dot_general(a, b, TN_DIMS, preferred_element_type=F32)


def _split3(x):
    hi = x.astype(BF16)
    r1 = x - hi.astype(F32)
    mid = r1.astype(BF16)
    return hi, mid, (r1 - mid.astype(F32)).astype(BF16)


def _dot_mask_f32(mask16, x):
    hi, mid, lo = _split3(x)
    return _dot(mask16, hi) + _dot(mask16, mid) + _dot(mask16, lo)


def _dot_f32_mask(x, mask16):
    hi, mid, lo = _split3(x)
    return _dot(hi, mask16) + _dot(mid, mask16) + _dot(lo, mask16)


def _norm_proj_kernel(x_ref, nw_ref, w_ref, *out_refs, segs):
    xn = _rms(x_ref[...], nw_ref[...]).astype(BF16)
    for (start, width), o_ref in zip(segs, out_refs):
        o_ref[...] = _dot(xn, w_ref[:, start:start + width]).astype(o_ref.dtype)


def norm_proj(x, norm_w, w, segs, dtypes, tm):
    t, k = x.shape
    n = w.shape[1]
    return pl.pallas_call(
        functools.partial(_norm_proj_kernel, segs=tuple(segs)),
        grid=(t // tm,),
        in_specs=[pl.BlockSpec((tm, k), lambda i: (i, 0)),
                  _resident((1, k)),
                  _resident((k, n))],
        out_specs=[pl.BlockSpec((tm, wd), lambda i: (i, 0)) for _, wd in segs],
        out_shape=[jax.ShapeDtypeStruct((t, wd), dt) for (_, wd), dt in zip(segs, dtypes)],
        compiler_params=_cparams("parallel"),
        name="norm_proj",
    )(x, norm_w.reshape(1, k), w)


def _win_attn_kernel(sink_ref, bias_ref, q_ref, kp_ref, kc_ref, kn_ref, vp_ref, vc_ref, vn_ref, o_ref):
    k3 = jnp.concatenate([kp_ref[...], kc_ref[...], kn_ref[...]], axis=0)
    v3 = jnp.concatenate([vp_ref[...], vc_ref[...], vn_ref[...]], axis=0)
    scale = HEAD_DIM ** -0.5
    heads = range(A_HEADS)
    s = [_dot_nt(q_ref[:, h * LANE:(h + 1) * LANE], k3[:, (h // A_GROUP) * LANE:(h // A_GROUP + 1) * LANE])
         for h in heads]
    s = [s[h] * scale + bias_ref[h] for h in heads]
    m = [jnp.maximum(jnp.max(s[h], axis=-1, keepdims=True), sink_ref[h]) for h in heads]
    p = [jnp.exp(s[h] - m[h]) for h in heads]
    rden = [1.0 / (jnp.sum(p[h], axis=-1, keepdims=True) + jnp.exp(sink_ref[h] - m[h])) for h in heads]
    pv = [_dot(p[h].astype(BF16), v3[:, (2 * (h // A_GROUP) + h % 2) * LANE:(2 * (h // A_GROUP) + h % 2 + 1) * LANE])
          for h in heads]
    for pair in range(A_HEADS // 2):
        h0, h1 = 2 * pair, 2 * pair + 1
        o_ref[:, pair * LANE:(pair + 1) * LANE] = (pv[h0] * rden[h0] + pv[h1] * rden[h1]).astype(o_ref.dtype)


def _window_bias():
    qi = np.arange(BLOCK)[:, None]
    kj = np.arange(3 * BLOCK)[None, :]
    dist = np.abs(qi + BLOCK - kj)
    slopes = np.array([2.0 ** (-8.0 * (h + 1) / A_HEADS) for h in range(A_HEADS)], np.float32)
    base = -slopes[:, None, None] * dist[None].astype(np.float32)
    out = []
    for first in (0, 1):
        for last in (0, 1):
            valid = (dist <= WINDOW) & ((kj >= BLOCK) | (first == 0)) & ((kj < 2 * BLOCK) | (last == 0))
            out.append(np.where(valid[None], base, -np.inf))
    return jnp.asarray(np.stack(out), F32)


def window_attention(q, k, v, sink, batch, seq):
    nb = seq // BLOCK
    cur = lambda b, n: (b * nb + n, 0)
    prev = lambda b, n: (b * nb + jnp.maximum(n - 1, 0), 0)
    nxt = lambda b, n: (b * nb + jnp.minimum(n + 1, nb - 1), 0)
    variant = lambda b, n: (2 * (n == 0).astype(jnp.int32) + (n == nb - 1).astype(jnp.int32), 0, 0, 0)
    kw, vw = k.shape[1], v.shape[1]
    return pl.pallas_call(
        _win_attn_kernel,
        grid=(batch, nb),
        in_specs=[pl.BlockSpec(memory_space=pltpu.SMEM),
                  pl.BlockSpec((None, A_HEADS, BLOCK, 3 * BLOCK), variant),
                  pl.BlockSpec((BLOCK, q.shape[1]), cur),
                  pl.BlockSpec((BLOCK, kw), prev), pl.BlockSpec((BLOCK, kw), cur), pl.BlockSpec((BLOCK, kw), nxt),
                  pl.BlockSpec((BLOCK, vw), prev), pl.BlockSpec((BLOCK, vw), cur), pl.BlockSpec((BLOCK, vw), nxt)],
        out_specs=pl.BlockSpec((BLOCK, A_HEADS * HEAD_DIM), cur),
        out_shape=jax.ShapeDtypeStruct((batch * seq, A_HEADS * HEAD_DIM), BF16),
        compiler_params=_cparams("parallel", "parallel"),
        name="window_attention",
    )(sink, _window_bias(), q, k, k, k, v, v, v)


def _mla_q_kernel(cq_ref, nw_ref, w_ref, tab_ref, o_ref):
    xn = _rms(cq_ref[...].astype(F32), nw_ref[...]).astype(BF16)
    tab = tab_ref[...]
    for h in range(B_HEADS):
        cols = slice(h * LANE, (h + 1) * LANE)
        o_ref[:, cols] = (_dot(xn, w_ref[:, cols]) * tab).astype(o_ref.dtype)


def mla_q_prep(cq, norm_w, w, tab, batch, seq, tm):
    nt = seq // tm
    r = cq.shape[1]
    n = w.shape[1]
    return pl.pallas_call(
        _mla_q_kernel,
        grid=(batch, nt),
        in_specs=[pl.BlockSpec((tm, r), lambda b, i: (b * nt + i, 0)),
                  _resident((1, r)), _resident((r, n)),
                  pl.BlockSpec((tm, LANE), lambda b, i: (i, 0))],
        out_specs=pl.BlockSpec((tm, n), lambda b, i: (b * nt + i, 0)),
        out_shape=jax.ShapeDtypeStruct((batch * seq, n), BF16),
        compiler_params=_cparams("parallel", "parallel"),
        name="mla_q_prep",
    )(cq, norm_w.reshape(1, r), w, tab)


def _mla_kv_kernel(ckv_ref, kab_ref, nw_ref, wk_ref, wv_ref, ctab_ref, stab_ref, k_ref, v_ref):
    xn = _rms(ckv_ref[...].astype(F32), nw_ref[...]).astype(BF16)
    kr = kab_ref[:, :LANE] * ctab_ref[...] + kab_ref[:, LANE:] * stab_ref[...]
    for h in range(B_HEADS):
        cols = slice(h * LANE, (h + 1) * LANE)
        k_ref[:, cols] = (_dot(xn, wk_ref[:, cols]) + kr).astype(k_ref.dtype)
    lane = lax.broadcasted_iota(jnp.int32, v_ref.shape, 1) % LANE
    v_ref[...] = jnp.where(lane == B_VDIM, 1.0, _dot(xn, wv_ref[...])).astype(v_ref.dtype)


def mla_kv_prep(ckv, kab, norm_w, wk, wv, ctab, stab, batch, seq, tm):
    nt = seq // tm
    r = ckv.shape[1]
    row = lambda b, i: (b * nt + i, 0)
    pos = lambda b, i: (i, 0)
    return pl.pallas_call(
        _mla_kv_kernel,
        grid=(batch, nt),
        in_specs=[pl.BlockSpec((tm, r), row), pl.BlockSpec((tm, 2 * LANE), row),
                  _resident((1, r)), _resident(wk.shape), _resident(wv.shape),
                  pl.BlockSpec((tm, LANE), pos), pl.BlockSpec((tm, LANE), pos)],
        out_specs=[pl.BlockSpec((tm, wk.shape[1]), row), pl.BlockSpec((tm, wv.shape[1]), row)],
        out_shape=[jax.ShapeDtypeStruct((batch * seq, wk.shape[1]), BF16),
                   jax.ShapeDtypeStruct((batch * seq, wv.shape[1]), BF16)],
        compiler_params=_cparams("parallel", "parallel"),
        name="mla_kv_prep",
    )(ckv, kab, norm_w.reshape(1, r), wk, wv, ctab, stab)


def _mla_flash_kernel(q_ref, k_ref, v_ref, o_ref, s_ref, p_ref, m_ref, a_ref, acc_ref, *, tk, nk, rb):
    tq = q_ref.shape[0]
    heads = range(2)
    m_ref[...] = jnp.full(m_ref.shape, -jnp.inf, F32)
    acc_ref[...] = jnp.zeros(acc_ref.shape, F32)

    def body(j, carry):
        rows = pl.ds(pl.multiple_of(j * tk, tk), tk)
        for e in heads:
            cols = slice(e * LANE, (e + 1) * LANE)
            s_ref[e] = _dot_nt(q_ref[:, cols], k_ref[rows, cols])
        for e in heads:
            cols = slice(e * LANE, (e + 1) * LANE)
            for r in range(tq // rb):
                rs = slice(r * rb, (r + 1) * rb)
                s = s_ref[e, rs, :]
                m_old = m_ref[e, rs, :]
                m_new = jnp.maximum(m_old, jnp.max(s, axis=-1, keepdims=True))
                m_ref[e, rs, :] = m_new
                a_ref[e, rs, :] = jnp.exp2(m_old - m_new)
                p_ref[e, rs, :] = jnp.exp2(s - jnp.concatenate([m_new] * (tk // LANE), axis=1)).astype(BF16)
            acc_ref[e] = a_ref[e] * acc_ref[e] + _dot(p_ref[e], v_ref[rows, cols])
        return carry

    lax.fori_loop(0, nk, body, 0)
    outs = []
    for e in heads:
        acc = acc_ref[e]
        outs.append(acc / acc[:, B_VDIM:B_VDIM + 1])
    lane = lax.broadcasted_iota(jnp.int32, (tq, LANE), 1)
    o_ref[...] = jnp.where(lane < B_VDIM, outs[0], pltpu.roll(outs[1], B_VDIM, axis=1)).astype(o_ref.dtype)


def mla_flash(q, k, v, batch, seq, tq, tk, rb):
    nq = seq // tq
    npair = B_HEADS // 2
    return pl.pallas_call(
        functools.partial(_mla_flash_kernel, tk=tk, nk=seq // tk, rb=rb),
        grid=(batch, npair, nq),
        in_specs=[pl.BlockSpec((tq, 2 * LANE), lambda b, p, i: (b * nq + i, p)),
                  pl.BlockSpec((seq, 2 * LANE), lambda b, p, i: (b, p)),
                  pl.BlockSpec((seq, 2 * LANE), lambda b, p, i: (b, p))],
        out_specs=pl.BlockSpec((tq, LANE), lambda b, p, i: (b * nq + i, p)),
        out_shape=jax.ShapeDtypeStruct((batch * seq, B_HEADS * B_VDIM), BF16),
        scratch_shapes=[pltpu.VMEM((2, tq, tk), F32), pltpu.VMEM((2, tq, tk), BF16), pltpu.VMEM((2, tq, LANE), F32),
                        pltpu.VMEM((2, tq, LANE), F32), pltpu.VMEM((2, tq, LANE), F32)],
        compiler_params=_cparams("parallel", "parallel", "parallel"),
        name="mla_flash",
    )(q, k, v)


def _mlp_tail(x, nw_ref, w1_ref, w2_ref, fw_ref, o_ref, fchunk, final_norm):
    xn = _rms(x, nw_ref[...]).astype(BF16)
    acc = x
    for f in range(w1_ref.shape[1] // fchunk):
        cols = slice(f * fchunk, (f + 1) * fchunk)
        h = jnp.square(jnp.maximum(_dot(xn, w1_ref[:, cols]), 0.0)).astype(BF16)
        acc = acc + _dot(h, w2_ref[cols, :])
    if final_norm:
        acc = _rms(acc, fw_ref[...])
    o_ref[...] = acc


def _attn_out_mlp_kernel(x_ref, a_ref, b_ref, wo_ref, nw_ref, w1_ref, w2_ref, fw_ref, o_ref, *, fchunk, final_norm):
    ka = a_ref.shape[1]
    o_ref[...] = x_ref[...] + _dot(a_ref[...], wo_ref[:ka, :]) + _dot(b_ref[...], wo_ref[ka:, :])
    _mlp_tail(o_ref[...], nw_ref, w1_ref, w2_ref, fw_ref, o_ref, fchunk, final_norm)


def _lin_out_mlp_kernel(x_ref, cf_ref, cb_ref, df_ref, db_ref, gc_ref, gd_ref, cw_ref, dw_ref, wo_ref,
                        nw_ref, w1_ref, w2_ref, fw_ref, o_ref, *, fchunk, final_norm):
    def gated(f_ref, b_ref, gate_ref, gnw_ref):
        o = f_ref[...] + b_ref[...]
        parts = []
        for h in range(LIN_HEADS):
            cols = slice(h * LIN_DV, (h + 1) * LIN_DV)
            gate = gate_ref[:, cols].astype(F32)
            parts.append((_rms(o[:, cols], gnw_ref[...]) * (gate * _sigmoid(gate))).astype(BF16))
        return jnp.concatenate(parts, axis=1)

    kc = cf_ref.shape[1]
    o_ref[...] = (x_ref[...] + _dot(gated(cf_ref, cb_ref, gc_ref, cw_ref), wo_ref[:kc, :])
                  + _dot(gated(df_ref, db_ref, gd_ref, dw_ref), wo_ref[kc:, :]))
    _mlp_tail(o_ref[...], nw_ref, w1_ref, w2_ref, fw_ref, o_ref, fchunk, final_norm)


def mixer_out_mlp(x, mixer_inputs, small_params, w_out, norm_w, w1, w2, final_w, final_norm, tm, fchunk):
    t, d = x.shape
    row = lambda i: (i, 0)
    body = _attn_out_mlp_kernel if len(mixer_inputs) == 2 else _lin_out_mlp_kernel
    return pl.pallas_call(
        functools.partial(body, fchunk=fchunk, final_norm=final_norm),
        grid=(t // tm,),
        in_specs=([pl.BlockSpec((tm, d), row)] + [pl.BlockSpec((tm, a.shape[1]), row) for a in mixer_inputs]
                  + [_resident(p.shape) for p in small_params]
                  + [_resident(w_out.shape), _resident((1, d)), _resident(w1.shape), _resident(w2.shape),
                     _resident((1, d))]),
        out_specs=pl.BlockSpec((tm, d), row),
        out_shape=jax.ShapeDtypeStruct((t, d), F32),
        compiler_params=_cparams("parallel"),
        name="mixer_out_mlp",
    )(x, *mixer_inputs, *small_params, w_out, norm_w.reshape(1, d), w1, w2, final_w.reshape(1, d))


def _direction_masks(d):
    sgn = 1 - 2 * d
    r = lax.broadcasted_iota(jnp.int32, (STACK, STACK), 0)
    c = lax.broadcasted_iota(jnp.int32, (STACK, STACK), 1)
    same = (r & -CHUNK) == (c & -CHUNK)
    order = ((r & (CHUNK - 1)) - (c & (CHUNK - 1))) * sgn
    i = lax.broadcasted_iota(jnp.int32, (CHUNK, CHUNK), 0)
    j = lax.broadcasted_iota(jnp.int32, (CHUNK, CHUNK), 1)
    cum = (((i - j) * sgn) >= 0).astype(F32)
    return same, same & (order >= 0), same & (order > 0), cum


def _block_diag(x, same):
    return jnp.where(same, jnp.concatenate([x] * LIN_HEADS, axis=0), 0.0)


def _block_diag_cols(x, same):
    return jnp.where(same, jnp.concatenate([x] * LIN_HEADS, axis=1), 0.0)


def _stack_heads(v, width):
    return jnp.concatenate([v[:, h * width:(h + 1) * width] for h in range(LIN_HEADS)], axis=0)


def _each(f, *lists):
    return [f(*args) for args in zip(*lists)]


def _scan_order(ncb):
    return [((0, i), (1, ncb - 1 - i)) for i in range(ncb)]


def _bidir_specs(shapes, nblk, tc):
    fwd = lambda b, n: (b * nblk + n, 0)
    bwd = lambda b, n: (b * nblk + nblk - 1 - n, 0)
    return ([pl.BlockSpec((tc, w), fwd) for w in shapes], [pl.BlockSpec((tc, w), bwd) for w in shapes])


def _gla_kernel(qf_ref, kf_ref, vf_ref, sf_ref, qb_ref, kb_ref, vb_ref, sb_ref, wg_ref, bg_ref,
                of_ref, ob_ref, st_ref, *, ncb):
    @pl.when(pl.program_id(1) == 0)
    def _():
        st_ref[...] = jnp.zeros_like(st_ref)

    refs = ((qf_ref, kf_ref, vf_ref, sf_ref, of_ref), (qb_ref, kb_ref, vb_ref, sb_ref, ob_ref))
    masks = [_direction_masks(d) for d in range(2)]
    items = [(d, ci) for d in range(2) for ci in range(ncb)]
    rows = lambda ci: slice(ci * CHUNK, (ci + 1) * CHUNK)
    same = [masks[d][0] for d, _ in items]
    z = [_dot(refs[d][3][rows(ci), :].astype(BF16), wg_ref[d]) + bg_ref[d] for d, ci in items]
    log_a = _each(lambda a: (jnp.minimum(a, 0.0) - jnp.log1p(jnp.exp(-jnp.abs(a)))) * (1.0 / GATE_NORM), z)
    b = [_dot_mask_f32(masks[d][3].astype(BF16), a) for (d, _), a in zip(items, log_a)]
    b_last = _each(lambda a: jnp.sum(a, axis=0, keepdims=True), log_a)
    q_in = [_block_diag(refs[d][0][rows(ci), :] * (LIN_DK ** -0.5) * jnp.exp(bb), sm).astype(BF16)
            for (d, ci), bb, sm in zip(items, b, same)]
    k = [refs[d][1][rows(ci), :] for d, ci in items]
    k_in = _each(lambda a, bb, sm: _block_diag(a * jnp.exp(-bb), sm).astype(BF16), k, b, same)
    k_out = _each(lambda a, bb, bl, sm: _block_diag(a * jnp.exp(bl - bb), sm).astype(BF16), k, b, b_last, same)
    v = [_stack_heads(refs[d][2][rows(ci), :], LIN_DV).astype(BF16) for d, ci in items]
    sc = [jnp.where(masks[d][1], _dot_nt(a, bb), 0.0).astype(BF16) for (d, _), a, bb in zip(items, q_in, k_in)]
    o_intra = dict(zip(items, _each(_dot, sc, v)))
    ds = dict(zip(items, _each(_dot_tn, v, k_out)))
    dec = dict(zip(items, _each(jnp.exp, b_last)))
    q_in = dict(zip(items, q_in))

    st = [st_ref[0], st_ref[1]]
    for step in _scan_order(ncb):
        o = [o_intra[it] + _dot_nt(q_in[it], st[it[0]].astype(BF16)) for it in step]
        for (d, ci), od in zip(step, o):
            st[d] = dec[(d, ci)] * st[d] + ds[(d, ci)]
            for h in range(LIN_HEADS):
                refs[d][4][rows(ci), h * LIN_DV:(h + 1) * LIN_DV] = od[h * CHUNK:(h + 1) * CHUNK, :]
    st_ref[0] = st[0]
    st_ref[1] = st[1]


def gla_scan(q, k, v, small, wg, bg, batch, seq, tc):
    nblk = seq // tc
    t = batch * seq
    hk, hv = q.shape[1], v.shape[1]
    fwd, bwd = _bidir_specs((hk, hk, hv, LANE), nblk, tc)
    ofwd, obwd = _bidir_specs((hv,), nblk, tc)
    return pl.pallas_call(
        functools.partial(_gla_kernel, ncb=tc // CHUNK),
        grid=(batch, nblk),
        in_specs=fwd + bwd + [_resident(wg.shape), _resident(bg.shape)],
        out_specs=ofwd + obwd,
        out_shape=[jax.ShapeDtypeStruct((t, hv), F32)] * 2,
        scratch_shapes=[pltpu.VMEM((2, LIN_DV, hk), F32)],
        compiler_params=_cparams("parallel", "arbitrary"),
        name="gla_scan",
    )(q, k, v, small, q, k, v, small, wg, bg)


def _gdn_prep_kernel(xp_ref, xc_ref, xn_ref, sm_ref, cw_ref, gsum_ref, alog_ref, dtb_ref,
                     q_ref, k_ref, v_ref, g_ref, b_ref, bt_ref, buf_ref, *, nt):
    i = pl.program_id(1)
    tm = xc_ref.shape[0]
    halo = xp_ref.shape[0]
    buf_ref[:halo, :] = xp_ref[...].astype(F32) * (i > 0).astype(F32)
    buf_ref[halo:halo + tm, :] = xc_ref[...].astype(F32)
    buf_ref[halo + tm:, :] = xn_ref[...].astype(F32) * (i < nt - 1).astype(F32)
    acc = jnp.zeros(xc_ref.shape, F32)
    for j in range(CONV_TAPS):
        off = halo + j - CONV_TAPS // 2
        acc = acc + buf_ref[off:off + tm, :] * cw_ref[j:j + 1, :]
    y = acc * _sigmoid(acc)
    nqk = LIN_HEADS * LIN_DK
    for idx, (o_ref, scale) in enumerate(((q_ref, LIN_DK ** -0.5), (k_ref, 1.0))):
        part = y[:, idx * nqk:(idx + 1) * nqk]
        ssq = _dot_f32_mask(part * part, gsum_ref[...])
        o_ref[...] = (part * lax.rsqrt(ssq + EPS) * scale).astype(o_ref.dtype)
    v_ref[...] = y[:, 2 * nqk:].astype(v_ref.dtype)
    sm = sm_ref[...]
    beta = _sigmoid(sm)
    g = -jnp.exp(alog_ref[...]) * _softplus(sm + dtb_ref[...])
    lane = lax.broadcasted_iota(jnp.int32, sm.shape, 1) - GATE_LANE0
    is_beta = (lane >= 0) & (lane < 2 * LIN_HEADS)
    is_g = (lane >= 2 * LIN_HEADS) & (lane < 4 * LIN_HEADS)
    g = jnp.where(is_g, g, 0.0)
    g_ref[...] = jnp.where(is_beta, beta, g)
    ri = lax.broadcasted_iota(jnp.int32, (CHUNK, CHUNK), 0)
    ci = lax.broadcasted_iota(jnp.int32, (CHUNK, CHUNK), 1)
    tril16 = (ri >= ci).astype(BF16)
    ones16 = jnp.ones((CHUNK, CHUNK), BF16)
    is_bwd = lax.broadcasted_iota(jnp.int32, (CHUNK, LANE), 1) >= GATE_LANE0 + 3 * LIN_HEADS
    for c in range(tm // CHUNK):
        rows = slice(c * CHUNK, (c + 1) * CHUNK)
        gc = g[rows, :]
        prefix = _dot_mask_f32(tril16, gc)
        total = _dot_mask_f32(ones16, gc)
        b_ref[rows, :] = jnp.where(is_bwd, total - prefix + gc, prefix)
        bt_ref[rows, :] = total


def gdn_prep(dqkv, small, conv_w, gsum, alog, dtb, batch, seq, tm):
    nt = seq // tm
    t, c = dqkv.shape
    halo = 2 * SUBLANE
    hb = tm // halo
    nqk = LIN_HEADS * LIN_DK
    row = lambda b, i: (b * nt + i, 0)
    prev = lambda b, i: (jnp.maximum((b * nt + i) * hb - 1, 0), 0)
    nxt = lambda b, i: (jnp.minimum((b * nt + i + 1) * hb, t // halo - 1), 0)
    return pl.pallas_call(
        functools.partial(_gdn_prep_kernel, nt=nt),
        grid=(batch, nt),
        in_specs=[pl.BlockSpec((halo, c), prev), pl.BlockSpec((tm, c), row), pl.BlockSpec((halo, c), nxt),
                  pl.BlockSpec((tm, LANE), row),
                  _resident(conv_w.shape), _resident(gsum.shape), _resident((1, LANE)), _resident((1, LANE))],
        out_specs=[pl.BlockSpec((tm, nqk), row), pl.BlockSpec((tm, nqk), row),
                   pl.BlockSpec((tm, c - 2 * nqk), row)] + [pl.BlockSpec((tm, LANE), row)] * 3,
        out_shape=[jax.ShapeDtypeStruct((t, nqk), BF16), jax.ShapeDtypeStruct((t, nqk), BF16),
                   jax.ShapeDtypeStruct((t, c - 2 * nqk), BF16)] + [jax.ShapeDtypeStruct((t, LANE), F32)] * 3,
        scratch_shapes=[pltpu.VMEM((tm + 2 * halo, c), F32)],
        compiler_params=_cparams("parallel", "parallel"),
        name="gdn_prep",
    )(dqkv, dqkv, dqkv, small, conv_w, gsum, alog, dtb)


def _gdn_kernel(qf_ref, kf_ref, vf_ref, gf_ref, bf_ref, tf_ref, qb_ref, kb_ref, vb_ref, gb_ref, bb_ref, tb_ref,
                of_ref, ob_ref, st_ref, *, ncb, nb):
    @pl.when(pl.program_id(1) == 0)
    def _():
        st_ref[...] = jnp.zeros_like(st_ref)

    refs = ((qf_ref, kf_ref, vf_ref, gf_ref, bf_ref, tf_ref, of_ref),
            (qb_ref, kb_ref, vb_ref, gb_ref, bb_ref, tb_ref, ob_ref))
    masks = [_direction_masks(d) for d in range(2)]
    r = lax.broadcasted_iota(jnp.int32, (STACK, STACK), 0)
    c = lax.broadcasted_iota(jnp.int32, (STACK, STACK), 1)
    diag = r == c
    eye = diag.astype(F32)
    lane_blk = lax.broadcasted_iota(jnp.int32, (SUBLANE, STACK), 1) // CHUNK

    def rows_of(x, lane0):
        return jnp.concatenate(
            [jnp.broadcast_to(x[:, lane0 + h:lane0 + h + 1], (CHUNK, STACK)) for h in range(LIN_HEADS)], axis=0)

    def head_lanes(row, lane0):
        out = jnp.broadcast_to(row[:, lane0:lane0 + 1], (SUBLANE, STACK))
        for h in range(1, LIN_HEADS):
            out = jnp.where(lane_blk == h, jnp.broadcast_to(row[:, lane0 + h:lane0 + h + 1], (SUBLANE, STACK)), out)
        return out

    each = _each
    items = [(bi, d, ci) for bi in range(nb) for d in range(2) for ci in range(ncb)]
    rows = lambda ci: slice(ci * CHUNK, (ci + 1) * CHUNK)
    chunk = lambda which, it: refs[it[1]][which][it[0], rows(it[2]), :]
    same = [masks[d][0] for _, d, _ in items]
    beta_col = [rows_of(chunk(3, it), _gate_lanes(it[1])[0]) for it in items]
    b_col = [rows_of(chunk(4, it), _gate_lanes(it[1])[1]) for it in items]
    bl_col = [rows_of(chunk(5, it), _gate_lanes(it[1])[1]) for it in items]
    k_bd = [_block_diag(chunk(1, it), sm) for it, sm in zip(items, same)]
    q_bd = [_block_diag(chunk(0, it), sm) for it, sm in zip(items, same)]
    kb_bd = each(lambda a, b: a * b, k_bd, beta_col)
    k16 = each(lambda a: a.astype(BF16), k_bd)
    kk = each(lambda a, b: _dot_nt(a.astype(BF16), b), kb_bd, k16)
    qk = each(lambda a, b: _dot_nt(a.astype(BF16), b), q_bd, k16)
    b_row = each(lambda bc: jnp.sum(jnp.where(diag, bc, 0.0), axis=0, keepdims=True), b_col)
    gamma = [jnp.exp(jnp.where(masks[it[1]][1], bc - br, -jnp.inf)) for it, bc, br in zip(items, b_col, b_row)]
    lower = [jnp.where(masks[it[1]][2], a * g, 0.0) for it, a, g in zip(items, kk, gamma)]
    attn = each(lambda a, g: (a * g).astype(BF16), qk, gamma)
    l_hi = each(lambda a: a.astype(BF16), lower)
    pw = [l_hi]
    for _ in range(5):
        pw.append(each(lambda a: _dot(a, a).astype(BF16), pw[-1]))

    def pair(a, b16):
        return a + _dot(a.astype(BF16), b16)

    f01 = each(lambda lo, p1: pair(eye - lo, p1), lower, pw[1])
    f23 = each(lambda p2, p3: pair(eye + p2.astype(F32), p3), pw[2], pw[3])
    f45 = each(lambda p4, p5: pair(eye + p4.astype(F32), p5), pw[4], pw[5])
    f0123 = each(lambda a, b: _dot(a.astype(BF16), b.astype(BF16)).astype(BF16), f01, f23)
    inv16 = each(lambda a, b: _dot(a, b.astype(BF16)).astype(BF16), f0123, f45)
    e_b = each(jnp.exp, b_col)
    vb = [_stack_heads(chunk(2, it), LIN_DV) * bc[:, :LIN_DV] for it, bc in zip(items, beta_col)]
    kbe = [_stack_heads(chunk(1, it), LIN_DK) * (bc * e)[:, :LIN_DK] for it, bc, e in zip(items, beta_col, e_b)]
    rhs = each(lambda v, kb: jnp.concatenate([v, kb], axis=1), vb, kbe)
    x16 = each(lambda m, r_: _dot(m, r_.astype(BF16)).astype(BF16), inv16, rhs)
    l_lo = each(lambda lo, hi: (lo - hi.astype(F32)).astype(BF16), lower, l_hi)
    resid = each(lambda r_, x, hi, lo: r_ - (x.astype(F32) + _dot(hi, x) + _dot(lo, x)), rhs, x16, l_hi, l_lo)
    sol = each(lambda x, m, rs: x.astype(F32) + _dot(m, rs.astype(BF16)), x16, inv16, resid)
    u = {it: sol[i][:, :LIN_DV] for i, it in enumerate(items)}
    w = {it: _block_diag_cols(sol[i][:, LIN_DV:].astype(BF16), same[i])
         for i, it in enumerate(items)}
    attn = dict(zip(items, attn))
    q_dec = {it: (q_bd[i] * e_b[i]).astype(BF16) for i, it in enumerate(items)}
    k_dec = {it: (k_bd[i] * jnp.exp(bl_col[i] - b_col[i])).astype(BF16) for i, it in enumerate(items)}
    dec = {(bi, d, ci): jnp.exp(head_lanes(refs[d][5][bi, ci * CHUNK:ci * CHUNK + SUBLANE, :],
                                           _gate_lanes(d)[1]))[:1, :] for bi, d, ci in items}

    streams = [(bi, d) for bi in range(nb) for d in range(2)]
    st = {sd: st_ref[i] for i, sd in enumerate(streams)}
    for fwd_bwd in _scan_order(ncb):
        step = [(bi, d, ci) for bi in range(nb) for d, ci in fwd_bwd]
        st16 = [st[it[:2]].astype(BF16) for it in step]
        v_new16 = [(u[it] - _dot_nt(w[it], s16)).astype(BF16) for it, s16 in zip(step, st16)]
        o = [_dot_nt(q_dec[it], s16) + _dot(attn[it], vn) for it, s16, vn in zip(step, st16, v_new16)]
        for it, od, vn in zip(step, o, v_new16):
            bi, d, ci = it
            st[(bi, d)] = dec[it] * st[(bi, d)] + _dot_tn(vn, k_dec[it])
            for h in range(LIN_HEADS):
                refs[d][6][bi, rows(ci), h * LIN_DV:(h + 1) * LIN_DV] = od[h * CHUNK:(h + 1) * CHUNK, :]
    for i, sd in enumerate(streams):
        st_ref[i] = st[sd]


def _gate_lanes(d):
    return GATE_LANE0 + d * LIN_HEADS, GATE_LANE0 + (2 + d) * LIN_HEADS


def gdn_scan(q, k, v, mixed, bcum, btot, batch, seq, tc, nb):
    nblk = seq // tc
    hk, hv = q.shape[1], v.shape[1]
    ins = [a.reshape(batch, seq, a.shape[1]) for a in (q, k, v, mixed, bcum, btot)]
    fwd = lambda bb, n: (bb, n, 0)
    bwd = lambda bb, n: (bb, nblk - 1 - n, 0)
    spec = lambda a, idx: pl.BlockSpec((nb, tc, a.shape[2]), idx)
    o_f, o_b = pl.pallas_call(
        functools.partial(_gdn_kernel, ncb=tc // CHUNK, nb=nb),
        grid=(batch // nb, nblk),
        in_specs=[spec(a, fwd) for a in ins] + [spec(a, bwd) for a in ins],
        out_specs=[pl.BlockSpec((nb, tc, hv), fwd), pl.BlockSpec((nb, tc, hv), bwd)],
        out_shape=[jax.ShapeDtypeStruct((batch, seq, hv), F32)] * 2,
        scratch_shapes=[pltpu.VMEM((2 * nb, LIN_DV, hk), F32)],
        compiler_params=_cparams("parallel", "arbitrary"),
        name="gdn_scan",
    )(*ins, *ins)
    return o_f.reshape(batch * seq, hv), o_b.reshape(batch * seq, hv)


def _rot_half_cols(w):
    half = w.shape[-1] // 2
    return jnp.concatenate([-w[..., half:], w[..., :half]], axis=-1)


def _pad_heads(w, heads, dim, offset=0):
    k = w.shape[0]
    out = jnp.zeros((k, heads, LANE), w.dtype)
    out = out.at[:, :, offset:offset + dim].set(w.reshape(k, heads, dim))
    return out.reshape(k, heads * LANE)


def _layer0_in_weight(w_in):
    aq = A_HEADS * HEAD_DIM
    akv = A_KV_HEADS * HEAD_DIM
    o = 0
    w_q = w_in[:, o:o + aq]; o += aq
    w_k = w_in[:, o:o + akv]; o += akv
    w_v = w_in[:, o:o + akv]; o += akv
    w_cq = w_in[:, o:o + B_Q_RANK]; o += B_Q_RANK
    w_ckv = w_in[:, o:o + B_KV_RANK]; o += B_KV_RANK
    w_pe = w_in[:, o:o + B_ROPE]
    zeros = jnp.zeros((w_in.shape[0], B_NOPE), w_in.dtype)
    v_lo = _pad_heads(w_v, A_KV_HEADS, HEAD_DIM, 0).reshape(-1, A_KV_HEADS, LANE)
    v_hi = _pad_heads(w_v, A_KV_HEADS, HEAD_DIM, HEAD_DIM).reshape(-1, A_KV_HEADS, LANE)
    w_v2 = jnp.stack([v_lo, v_hi], axis=2).reshape(w_in.shape[0], 2 * A_KV_HEADS * LANE)
    w_pe_rot = _rot_half_cols(w_pe)
    pieces = [_pad_heads(w_q, A_HEADS, HEAD_DIM), _pad_heads(w_k, A_KV_HEADS, HEAD_DIM), w_v2,
              w_cq, w_ckv,
              jnp.concatenate([zeros, w_pe, w_pe, zeros, w_pe_rot, w_pe_rot], axis=1)]
    segs, start = [], 0
    for p in pieces:
        segs.append((start, p.shape[1]))
        start += p.shape[1]
    return jnp.concatenate(pieces, axis=1).astype(BF16), segs


def _mla_weights(w_uq, w_ukv):
    r = w_uq.shape[0]
    wq = w_uq.reshape(r, B_HEADS, B_NOPE + B_ROPE)
    pe = wq[:, :, B_NOPE:]
    wq = jnp.concatenate([wq[:, :, :B_NOPE], pe, _rot_half_cols(pe)], axis=-1).reshape(r, B_HEADS * LANE)
    rk = w_ukv.shape[0]
    wkv = w_ukv.reshape(rk, B_HEADS, B_NOPE + B_VDIM)
    wk = _pad_heads(wkv[:, :, :B_NOPE].reshape(rk, B_HEADS * B_NOPE), B_HEADS, B_NOPE)
    wv = _pad_heads(wkv[:, :, B_NOPE:].reshape(rk, B_HEADS * B_VDIM), B_HEADS, B_VDIM)
    return wq.astype(BF16), wk.astype(BF16), wv.astype(BF16)


def _rope_tables(seq):
    half = B_ROPE // 2
    inv = ROPE_THETA ** (-jnp.arange(half, dtype=F32) / half)
    ang = jnp.arange(seq, dtype=F32)[:, None] * inv[None, :]
    cos2 = jnp.tile(jnp.cos(ang), (1, 4))
    sin2 = jnp.tile(jnp.sin(ang), (1, 4))
    zeros = jnp.zeros((seq, B_NOPE), F32)
    qscale = (B_NOPE + B_ROPE) ** -0.5 * math.log2(math.e)
    q_tab = jnp.concatenate([jnp.ones((seq, B_NOPE), F32), cos2[:, :B_ROPE], sin2[:, :B_ROPE]], axis=1) * qscale
    k_cos = jnp.concatenate([zeros, cos2], axis=1)
    k_sin = jnp.concatenate([zeros, sin2], axis=1)
    return q_tab, k_cos, k_sin


def _layer1_in_weight(w_in):
    hk = LIN_HEADS * LIN_DK
    hv = LIN_HEADS * LIN_DV
    conv_ch = 2 * hk + hv
    main = 2 * hk + 2 * hv + 2 * GATE_RANK
    o = 0
    w_main = w_in[:, :2 * hk + 2 * hv]; o = 2 * hk + 2 * hv
    w_gl = w_in[:, o:o + 2 * GATE_RANK]; o += 2 * GATE_RANK
    w_conv = w_in[:, o:o + conv_ch]; o += conv_ch
    w_z = w_in[:, o:o + hv]; o += hv
    w_small = w_in[:, o:o + 4 * LIN_HEADS]
    del main
    pad = jnp.zeros((w_in.shape[0], LANE - 2 * GATE_RANK - 4 * LIN_HEADS), w_in.dtype)
    w = jnp.concatenate([w_main, w_conv, w_z, w_gl, w_small, pad], axis=1).astype(BF16)
    widths = [hk, hk, hv, hv, conv_ch, hv, LANE]
    segs, start = [], 0
    for wd in widths:
        segs.append((start, wd))
        start += wd
    return w, segs


def _pick_tile(n, pref):
    t = min(n, pref)
    while n % t:
        t //= 2
    return t


def kernel(x, att_norm, att_w_in, att_sink, mla_q_norm, mla_w_uq, mla_kv_norm, mla_w_ukv, att_w_out, lin_norm, lin_w_in, gla_w_gate_f, gla_b_gate_f, gla_w_gate_b, gla_b_gate_b, gla_norm, gdn_conv, gdn_a_log_f, gdn_dt_bias_f, gdn_a_log_b, gdn_dt_bias_b, gdn_norm, lin_w_out, mlp_norm, mlp_w1, mlp_w2, final_norm):
    batch, seq, dm = x.shape
    t = batch * seq
    depth = mlp_norm.shape[0]
    tm = _pick_tile(seq, 512)
    xs = x.reshape(t, dm)
    for layer in range(depth):
        i = layer // 2
        if layer % 2 == 0:
            w0, segs = _layer0_in_weight(att_w_in[i])
            qa, ka, va, cq, ckv, kab = norm_proj(xs, att_norm[i], w0, segs, [BF16] * 6, tm)
            o_a = window_attention(qa, ka, va, att_sink[i].astype(F32), batch, seq)
            wq, wk, wv = _mla_weights(mla_w_uq[i], mla_w_ukv[i])
            q_tab, k_cos, k_sin = _rope_tables(seq)
            qb = mla_q_prep(cq, mla_q_norm[i], wq, q_tab, batch, seq, tm)
            kb, vb = mla_kv_prep(ckv, kab, mla_kv_norm[i], wk, wv, k_cos, k_sin, batch, seq, tm)
            o_b = mla_flash(qb, kb, vb, batch, seq, _pick_tile(seq, 1024), _pick_tile(seq, 2048), 64)
            mixer_inputs, small_params, w_out = [o_a, o_b], [], att_w_out[i]
        else:
            w1, segs = _layer1_in_weight(lin_w_in[i])
            cq, ck, cv, cg, dqkv, dz, small = norm_proj(xs, lin_norm[i], w1, segs, [BF16] * 6 + [F32], tm)
            hk = LIN_HEADS * LIN_DK
            wg = jnp.zeros((2, LANE, hk), F32)
            wg = wg.at[0, :GATE_RANK].set(gla_w_gate_f[i]).at[1, GATE_RANK:2 * GATE_RANK].set(gla_w_gate_b[i])
            bg = jnp.stack([gla_b_gate_f[i], gla_b_gate_b[i]]).reshape(2, 1, hk).astype(F32)
            tc = _pick_tile(seq, 256)
            o_cf, o_cb = gla_scan(cq, ck, cv, small, wg.astype(BF16), bg, batch, seq, tc)
            gsum = (jnp.arange(hk)[:, None] // LIN_DK == jnp.arange(hk)[None, :] // LIN_DK).astype(BF16)
            nh = LIN_HEADS
            a0 = GATE_LANE0 + 2 * nh
            alog = jnp.zeros((LANE,), F32).at[a0:a0 + nh].set(gdn_a_log_f[i]).at[a0 + nh:a0 + 2 * nh].set(gdn_a_log_b[i])
            dtb = jnp.zeros((LANE,), F32).at[a0:a0 + nh].set(gdn_dt_bias_f[i]).at[a0 + nh:a0 + 2 * nh].set(gdn_dt_bias_b[i])
            dq, dk, dv, mixed, bcum, btot = gdn_prep(dqkv, small, gdn_conv[i].astype(F32), gsum,
                                                     alog.reshape(1, LANE), dtb.reshape(1, LANE), batch, seq, tm)
            o_df, o_db = gdn_scan(dq, dk, dv, mixed, bcum, btot, batch, seq, tc, 2 if batch % 2 == 0 else 1)
            mixer_inputs = [o_cf, o_cb, o_df, o_db, cg, dz]
            small_params = [gla_norm[i].reshape(1, LIN_DV), gdn_norm[i].reshape(1, LIN_DV)]
            w_out = lin_w_out[i]
        last = layer == depth - 1
        xs = mixer_out_mlp(xs, mixer_inputs, small_params, w_out.astype(BF16), mlp_norm[layer],
                           mlp_w1[layer].astype(BF16), mlp_w2[layer].astype(BF16), final_norm, last, tm, 1024)
    return xs.reshape(batch, seq, dm)
```

```python
import functools
import math

import numpy as np
import jax
import jax.numpy as jnp
from jax import lax
from jax.experimental import pallas as pl
from jax.experimental.pallas import tpu as pltpu

F32 = jnp.float32
BF16 = jnp.bfloat16
EPS = 1e-6

LANE = 128
SUBLANE = 8
VMEM_LIMIT_BYTES = 56 * 1024 * 1024

HEAD_DIM = 64
A_HEADS = 8
A_KV_HEADS = 2
A_GROUP = A_HEADS // A_KV_HEADS
WINDOW = 128
BLOCK = 128
B_HEADS = 8
B_NOPE = 64
B_ROPE = 32
B_VDIM = 64
B_Q_RANK = 384
B_KV_RANK = 256
ROPE_THETA = 10000.0
LIN_HEADS = 4
LIN_DK = 64
LIN_DV = 128
GATE_RANK = 16
GATE_NORM = 16.0
CHUNK = 64
CONV_TAPS = 5
STACK = LIN_HEADS * CHUNK
GATE_LANE0 = 2 * GATE_RANK

NT_DIMS = (((1,), (1,)), ((), ()))
TN_DIMS = (((0,), (0,)), ((), ()))


def _cparams(*sem):
    return pltpu.CompilerParams(dimension_semantics=sem, vmem_limit_bytes=VMEM_LIMIT_BYTES)


def _resident(shape):
    nd = len(shape)
    return pl.BlockSpec(shape, lambda *_: (0,) * nd, pipeline_mode=pl.Buffered(1))


def _rms(x, w):
    return x * lax.rsqrt(jnp.mean(x * x, axis=-1, keepdims=True) + EPS) * w


def _sigmoid(x):
    return 1.0 / (1.0 + jnp.exp(-x))


def _softplus(x):
    return jnp.maximum(x, 0.0) + jnp.log1p(jnp.exp(-jnp.abs(x)))


def _dot(a, b):
    return jnp.dot(a, b, preferred_element_type=F32)


def _dot_nt(a, b):
    return lax.dot_general(a, b, NT_DIMS, preferred_element_type=F32)


def _dot_tn(a, b):
    return lax.dot_general(a, b, TN_DIMS, preferred_element_type=F32)


def _split3(x):
    hi = x.astype(BF16)
    r1 = x - hi.astype(F32)
    mid = r1.astype(BF16)
    return hi, mid, (r1 - mid.astype(F32)).astype(BF16)


def _dot_mask_f32(mask16, x):
    hi, mid, lo = _split3(x)
    return _dot(mask16, hi) + _dot(mask16, mid) + _dot(mask16, lo)


def _dot_f32_mask(x, mask16):
    hi, mid, lo = _split3(x)
    return _dot(hi, mask16) + _dot(mid, mask16) + _dot(lo, mask16)


def _norm_proj_kernel(x_ref, nw_ref, w_ref, *out_refs, segs):
    xn = _rms(x_ref[...], nw_ref[...]).astype(BF16)
    for (start, width), o_ref in zip(segs, out_refs):
        o_ref[...] = _dot(xn, w_ref[:, start:start + width]).astype(o_ref.dtype)


def norm_proj(x, norm_w, w, segs, dtypes, tm):
    t, k = x.shape
    n = w.shape[1]
    return pl.pallas_call(
        functools.partial(_norm_proj_kernel, segs=tuple(segs)),
        grid=(t // tm,),
        in_specs=[pl.BlockSpec((tm, k), lambda i: (i, 0)),
                  _resident((1, k)),
                  _resident((k, n))],
        out_specs=[pl.BlockSpec((tm, wd), lambda i: (i, 0)) for _, wd in segs],
        out_shape=[jax.ShapeDtypeStruct((t, wd), dt) for (_, wd), dt in zip(segs, dtypes)],
        compiler_params=_cparams("parallel"),
        name="norm_proj",
    )(x, norm_w.reshape(1, k), w)


def _win_attn_kernel(sink_ref, bias_ref, q_ref, kp_ref, kc_ref, kn_ref, vp_ref, vc_ref, vn_ref, o_ref):
    k3 = jnp.concatenate([kp_ref[...], kc_ref[...], kn_ref[...]], axis=0)
    v3 = jnp.concatenate([vp_ref[...], vc_ref[...], vn_ref[...]], axis=0)
    scale = HEAD_DIM ** -0.5
    heads = range(A_HEADS)
    s = [_dot_nt(q_ref[:, h * LANE:(h + 1) * LANE], k3[:, (h // A_GROUP) * LANE:(h // A_GROUP + 1) * LANE])
         for h in heads]
    s = [s[h] * scale + bias_ref[h] for h in heads]
    m = [jnp.maximum(jnp.max(s[h], axis=-1, keepdims=True), sink_ref[h]) for h in heads]
    p = [jnp.exp(s[h] - m[h]) for h in heads]
    rden = [1.0 / (jnp.sum(p[h], axis=-1, keepdims=True) + jnp.exp(sink_ref[h] - m[h])) for h in heads]
    pv = [_dot(p[h].astype(BF16), v3[:, (2 * (h // A_GROUP) + h % 2) * LANE:(2 * (h // A_GROUP) + h % 2 + 1) * LANE])
          for h in heads]
    for pair in range(A_HEADS // 2):
        h0, h1 = 2 * pair, 2 * pair + 1
        o_ref[:, pair * LANE:(pair + 1) * LANE] = (pv[h0] * rden[h0] + pv[h1] * rden[h1]).astype(o_ref.dtype)


def _window_bias():
    qi = np.arange(BLOCK)[:, None]
    kj = np.arange(3 * BLOCK)[None, :]
    dist = np.abs(qi + BLOCK - kj)
    slopes = np.array([2.0 ** (-8.0 * (h + 1) / A_HEADS) for h in range(A_HEADS)], np.float32)
    base = -slopes[:, None, None] * dist[None].astype(np.float32)
    out = []
    for first in (0, 1):
        for last in (0, 1):
            valid = (dist <= WINDOW) & ((kj >= BLOCK) | (first == 0)) & ((kj < 2 * BLOCK) | (last == 0))
            out.append(np.where(valid[None], base, -np.inf))
    return jnp.asarray(np.stack(out), F32)


def window_attention(q, k, v, sink, batch, seq):
    nb = seq // BLOCK
    cur = lambda b, n: (b * nb + n, 0)
    prev = lambda b, n: (b * nb + jnp.maximum(n - 1, 0), 0)
    nxt = lambda b, n: (b * nb + jnp.minimum(n + 1, nb - 1), 0)
    variant = lambda b, n: (2 * (n == 0).astype(jnp.int32) + (n == nb - 1).astype(jnp.int32), 0, 0, 0)
    kw, vw = k.shape[1], v.shape[1]
    return pl.pallas_call(
        _win_attn_kernel,
        grid=(batch, nb),
        in_specs=[pl.BlockSpec(memory_space=pltpu.SMEM),
                  pl.BlockSpec((None, A_HEADS, BLOCK, 3 * BLOCK), variant),
                  pl.BlockSpec((BLOCK, q.shape[1]), cur),
                  pl.BlockSpec((BLOCK, kw), prev), pl.BlockSpec((BLOCK, kw), cur), pl.BlockSpec((BLOCK, kw), nxt),
                  pl.BlockSpec((BLOCK, vw), prev), pl.BlockSpec((BLOCK, vw), cur), pl.BlockSpec((BLOCK, vw), nxt)],
        out_specs=pl.BlockSpec((BLOCK, A_HEADS * HEAD_DIM), cur),
        out_shape=jax.ShapeDtypeStruct((batch * seq, A_HEADS * HEAD_DIM), BF16),
        compiler_params=_cparams("parallel", "parallel"),
        name="window_attention",
    )(sink, _window_bias(), q, k, k, k, v, v, v)


def _attn_in_kernel(x_ref, nw_ref, w_ref, qnw_ref, wq_ref, kvnw_ref, wk_ref, wv_ref, qtab_ref, ctab_ref, stab_ref,
                    qa_ref, ka_ref, va_ref, q_ref, k_ref, v_ref, *, segs):
    xn = _rms(x_ref[...], nw_ref[...]).astype(BF16)
    proj = [_dot(xn, w_ref[:, start:start + width]) for start, width in segs]
    for o_ref, val in zip((qa_ref, ka_ref, va_ref), proj):
        o_ref[...] = val.astype(o_ref.dtype)
    cq, ckv, kab = proj[3:]
    xq = _rms(cq, qnw_ref[...]).astype(BF16)
    tab = qtab_ref[...]
    for h in range(B_HEADS):
        cols = slice(h * LANE, (h + 1) * LANE)
        q_ref[:, cols] = (_dot(xq, wq_ref[:, cols]) * tab).astype(q_ref.dtype)
    xkv = _rms(ckv, kvnw_ref[...]).astype(BF16)
    kr = kab[:, :LANE] * ctab_ref[...] + kab[:, LANE:] * stab_ref[...]
    for h in range(B_HEADS):
        cols = slice(h * LANE, (h + 1) * LANE)
        k_ref[:, cols] = (_dot(xkv, wk_ref[:, cols]) + kr).astype(k_ref.dtype)
    lane = lax.broadcasted_iota(jnp.int32, v_ref.shape, 1) % LANE
    v_ref[...] = jnp.where(lane == B_VDIM, 1.0, _dot(xkv, wv_ref[...])).astype(v_ref.dtype)


def attn_in_proj(x, norm_w, w, segs, q_norm_w, wq, kv_norm_w, wk, wv, q_tab, k_cos, k_sin, seq, tm):
    t, k = x.shape
    nt = seq // tm
    row = lambda i: (i, 0)
    pos = lambda i: (i % nt, 0)
    widths = [wd for _, wd in segs[:3]] + [wq.shape[1], wk.shape[1], wv.shape[1]]
    return pl.pallas_call(
        functools.partial(_attn_in_kernel, segs=tuple(segs)),
        grid=(t // tm,),
        in_specs=[pl.BlockSpec((tm, k), row), _resident((1, k)), _resident(w.shape),
                  _resident((1, wq.shape[0])), _resident(wq.shape),
                  _resident((1, wk.shape[0])), _resident(wk.shape), _resident(wv.shape),
                  pl.BlockSpec((tm, LANE), pos), pl.BlockSpec((tm, LANE), pos), pl.BlockSpec((tm, LANE), pos)],
        out_specs=[pl.BlockSpec((tm, wd), row) for wd in widths],
        out_shape=[jax.ShapeDtypeStruct((t, wd), BF16) for wd in widths],
        compiler_params=_cparams("parallel"),
        name="attn_in_proj",
    )(x, norm_w.reshape(1, k), w, q_norm_w.reshape(1, -1), wq, kv_norm_w.reshape(1, -1), wk, wv,
      q_tab, k_cos, k_sin)


def _mla_flash_kernel(q_ref, k_ref, v_ref, o_ref, s_ref, p_ref, m_ref, a_ref, acc_ref, *, tk, nk, rb):
    tq = q_ref.shape[0]
    heads = range(2)
    m_ref[...] = jnp.full(m_ref.shape, -jnp.inf, F32)
    acc_ref[...] = jnp.zeros(acc_ref.shape, F32)

    def body(j, carry):
        rows = pl.ds(pl.multiple_of(j * tk, tk), tk)
        for e in heads:
            cols = slice(e * LANE, (e + 1) * LANE)
            s_ref[e] = _dot_nt(q_ref[:, cols], k_ref[rows, cols])
        for e in heads:
            cols = slice(e * LANE, (e + 1) * LANE)
            for r in range(tq // rb):
                rs = slice(r * rb, (r + 1) * rb)
                s = s_ref[e, rs, :]
                m_old = m_ref[e, rs, :]
                m_new = jnp.maximum(m_old, jnp.max(s, axis=-1, keepdims=True))
                m_ref[e, rs, :] = m_new
                a_ref[e, rs, :] = jnp.exp2(m_old - m_new)
                p_ref[e, rs, :] = jnp.exp2(s - jnp.concatenate([m_new] * (tk // LANE), axis=1)).astype(BF16)
            acc_ref[e] = a_ref[e] * acc_ref[e] + _dot(p_ref[e], v_ref[rows, cols])
        return carry

    lax.fori_loop(0, nk, body, 0)
    outs = []
    for e in heads:
        acc = acc_ref[e]
        outs.append(acc / acc[:, B_VDIM:B_VDIM + 1])
    lane = lax.broadcasted_iota(jnp.int32, (tq, LANE), 1)
    o_ref[...] = jnp.where(lane < B_VDIM, outs[0], pltpu.roll(outs[1], B_VDIM, axis=1)).astype(o_ref.dtype)


def mla_flash(q, k, v, batch, seq, tq, tk, rb):
    nq = seq // tq
    npair = B_HEADS // 2
    return pl.pallas_call(
        functools.partial(_mla_flash_kernel, tk=tk, nk=seq // tk, rb=rb),
        grid=(batch, npair, nq),
        in_specs=[pl.BlockSpec((tq, 2 * LANE), lambda b, p, i: (b * nq + i, p)),
                  pl.BlockSpec((seq, 2 * LANE), lambda b, p, i: (b, p)),
                  pl.BlockSpec((seq, 2 * LANE), lambda b, p, i: (b, p))],
        out_specs=pl.BlockSpec((tq, LANE), lambda b, p, i: (b * nq + i, p)),
        out_shape=jax.ShapeDtypeStruct((batch * seq, B_HEADS * B_VDIM), BF16),
        scratch_shapes=[pltpu.VMEM((2, tq, tk), F32), pltpu.VMEM((2, tq, tk), BF16), pltpu.VMEM((2, tq, LANE), F32),
                        pltpu.VMEM((2, tq, LANE), F32), pltpu.VMEM((2, tq, LANE), F32)],
        compiler_params=_cparams("parallel", "parallel", "parallel"),
        name="mla_flash",
    )(q, k, v)


def _mlp_tail(x, nw_ref, w1_ref, w2_ref, fw_ref, o_ref, fchunk, final_norm):
    xn = _rms(x, nw_ref[...]).astype(BF16)
    acc = x
    for f in range(w1_ref.shape[1] // fchunk):
        cols = slice(f * fchunk, (f + 1) * fchunk)
        h = jnp.square(jnp.maximum(_dot(xn, w1_ref[:, cols]), 0.0)).astype(BF16)
        acc = acc + _dot(h, w2_ref[cols, :])
    if final_norm:
        acc = _rms(acc, fw_ref[...])
    o_ref[...] = acc


def _attn_out_mlp_kernel(x_ref, a_ref, b_ref, wo_ref, nw_ref, w1_ref, w2_ref, fw_ref, o_ref, *, fchunk, final_norm):
    ka = a_ref.shape[1]
    o_ref[...] = x_ref[...] + _dot(a_ref[...], wo_ref[:ka, :]) + _dot(b_ref[...], wo_ref[ka:, :])
    _mlp_tail(o_ref[...], nw_ref, w1_ref, w2_ref, fw_ref, o_ref, fchunk, final_norm)


def _lin_out_mlp_kernel(x_ref, cf_ref, cb_ref, df_ref, db_ref, gc_ref, gd_ref, cw_ref, dw_ref, wo_ref,
                        nw_ref, w1_ref, w2_ref, fw_ref, o_ref, *, fchunk, final_norm):
    def gated(f_ref, b_ref, gate_ref, gnw_ref):
        o = f_ref[...] + b_ref[...]
        parts = []
        for h in range(LIN_HEADS):
            cols = slice(h * LIN_DV, (h + 1) * LIN_DV)
            gate = gate_ref[:, cols].astype(F32)
            parts.append((_rms(o[:, cols], gnw_ref[...]) * (gate * _sigmoid(gate))).astype(BF16))
        return jnp.concatenate(parts, axis=1)

    kc = cf_ref.shape[1]
    o_ref[...] = (x_ref[...] + _dot(gated(cf_ref, cb_ref, gc_ref, cw_ref), wo_ref[:kc, :])
                  + _dot(gated(df_ref, db_ref, gd_ref, dw_ref), wo_ref[kc:, :]))
    _mlp_tail(o_ref[...], nw_ref, w1_ref, w2_ref, fw_ref, o_ref, fchunk, final_norm)


def mixer_out_mlp(x, mixer_inputs, small_params, w_out, norm_w, w1, w2, final_w, final_norm, tm, fchunk):
    t, d = x.shape
    row = lambda i: (i, 0)
    body = _attn_out_mlp_kernel if len(mixer_inputs) == 2 else _lin_out_mlp_kernel
    return pl.pallas_call(
        functools.partial(body, fchunk=fchunk, final_norm=final_norm),
        grid=(t // tm,),
        in_specs=([pl.BlockSpec((tm, d), row)] + [pl.BlockSpec((tm, a.shape[1]), row) for a in mixer_inputs]
                  + [_resident(p.shape) for p in small_params]
                  + [_resident(w_out.shape), _resident((1, d)), _resident(w1.shape), _resident(w2.shape),
                     _resident((1, d))]),
        out_specs=pl.BlockSpec((tm, d), row),
        out_shape=jax.ShapeDtypeStruct((t, d), F32),
        compiler_params=_cparams("parallel"),
        name="mixer_out_mlp",
    )(x, *mixer_inputs, *small_params, w_out, norm_w.reshape(1, d), w1, w2, final_w.reshape(1, d))


def _direction_masks(d):
    sgn = 1 - 2 * d
    r = lax.broadcasted_iota(jnp.int32, (STACK, STACK), 0)
    c = lax.broadcasted_iota(jnp.int32, (STACK, STACK), 1)
    same = (r & -CHUNK) == (c & -CHUNK)
    order = ((r & (CHUNK - 1)) - (c & (CHUNK - 1))) * sgn
    i = lax.broadcasted_iota(jnp.int32, (CHUNK, CHUNK), 0)
    j = lax.broadcasted_iota(jnp.int32, (CHUNK, CHUNK), 1)
    cum = (((i - j) * sgn) >= 0).astype(F32)
    return same, same & (order >= 0), same & (order > 0), cum


def _block_diag(x, same):
    return jnp.where(same, jnp.concatenate([x] * LIN_HEADS, axis=0), 0.0)


def _block_diag_cols(x, same):
    return jnp.where(same, jnp.concatenate([x] * LIN_HEADS, axis=1), 0.0)


def _stack_heads(v, width):
    return jnp.concatenate([v[:, h * width:(h + 1) * width] for h in range(LIN_HEADS)], axis=0)


def _each(f, *lists):
    return [f(*args) for args in zip(*lists)]


def _scan_order(ncb):
    return [((0, i), (1, ncb - 1 - i)) for i in range(ncb)]


def _bidir_specs(shapes, nblk, tc):
    fwd = lambda b, n: (b * nblk + n, 0)
    bwd = lambda b, n: (b * nblk + nblk - 1 - n, 0)
    return ([pl.BlockSpec((tc, w), fwd) for w in shapes], [pl.BlockSpec((tc, w), bwd) for w in shapes])


def _gla_kernel(qf_ref, kf_ref, vf_ref, sf_ref, qb_ref, kb_ref, vb_ref, sb_ref, wg_ref, bg_ref,
                of_ref, ob_ref, st_ref, *, ncb):
    @pl.when(pl.program_id(1) == 0)
    def _():
        st_ref[...] = jnp.zeros_like(st_ref)

    refs = ((qf_ref, kf_ref, vf_ref, sf_ref, of_ref), (qb_ref, kb_ref, vb_ref, sb_ref, ob_ref))
    masks = [_direction_masks(d) for d in range(2)]
    items = [(d, ci) for d in range(2) for ci in range(ncb)]
    rows = lambda ci: slice(ci * CHUNK, (ci + 1) * CHUNK)
    same = [masks[d][0] for d, _ in items]
    z = [_dot(refs[d][3][rows(ci), :].astype(BF16), wg_ref[d]) + bg_ref[d] for d, ci in items]
    log_a = _each(lambda a: (jnp.minimum(a, 0.0) - jnp.log1p(jnp.exp(-jnp.abs(a)))) * (1.0 / GATE_NORM), z)
    b = [_dot_mask_f32(masks[d][3].astype(BF16), a) for (d, _), a in zip(items, log_a)]
    b_last = _each(lambda a: jnp.sum(a, axis=0, keepdims=True), log_a)
    q_in = [_block_diag(refs[d][0][rows(ci), :] * (LIN_DK ** -0.5) * jnp.exp(bb), sm).astype(BF16)
            for (d, ci), bb, sm in zip(items, b, same)]
    k = [refs[d][1][rows(ci), :] for d, ci in items]
    k_in = _each(lambda a, bb, sm: _block_diag(a * jnp.exp(-bb), sm).astype(BF16), k, b, same)
    k_out = _each(lambda a, bb, bl, sm: _block_diag(a * jnp.exp(bl - bb), sm).astype(BF16), k, b, b_last, same)
    v = [_stack_heads(refs[d][2][rows(ci), :], LIN_DV).astype(BF16) for d, ci in items]
    sc = [jnp.where(masks[d][1], _dot_nt(a, bb), 0.0).astype(BF16) for (d, _), a, bb in zip(items, q_in, k_in)]
    o_intra = dict(zip(items, _each(_dot, sc, v)))
    ds = dict(zip(items, _each(_dot_tn, v, k_out)))
    dec = dict(zip(items, _each(jnp.exp, b_last)))
    q_in = dict(zip(items, q_in))

    st = [st_ref[0], st_ref[1]]
    for step in _scan_order(ncb):
        o = [o_intra[it] + _dot_nt(q_in[it], st[it[0]].astype(BF16)) for it in step]
        for (d, ci), od in zip(step, o):
            st[d] = dec[(d, ci)] * st[d] + ds[(d, ci)]
            for h in range(LIN_HEADS):
                refs[d][4][rows(ci), h * LIN_DV:(h + 1) * LIN_DV] = od[h * CHUNK:(h + 1) * CHUNK, :]
    st_ref[0] = st[0]
    st_ref[1] = st[1]


def gla_scan(q, k, v, small, wg, bg, batch, seq, tc):
    nblk = seq // tc
    t = batch * seq
    hk, hv = q.shape[1], v.shape[1]
    fwd, bwd = _bidir_specs((hk, hk, hv, LANE), nblk, tc)
    ofwd, obwd = _bidir_specs((hv,), nblk, tc)
    return pl.pallas_call(
        functools.partial(_gla_kernel, ncb=tc // CHUNK),
        grid=(batch, nblk),
        in_specs=fwd + bwd + [_resident(wg.shape), _resident(bg.shape)],
        out_specs=ofwd + obwd,
        out_shape=[jax.ShapeDtypeStruct((t, hv), F32)] * 2,
        scratch_shapes=[pltpu.VMEM((2, LIN_DV, hk), F32)],
        compiler_params=_cparams("parallel", "arbitrary"),
        name="gla_scan",
    )(q, k, v, small, q, k, v, small, wg, bg)


def _gdn_prep_kernel(xp_ref, xc_ref, xn_ref, sm_ref, cw_ref, gsum_ref, alog_ref, dtb_ref,
                     q_ref, k_ref, v_ref, g_ref, b_ref, bt_ref, buf_ref, *, nt):
    i = pl.program_id(1)
    tm = xc_ref.shape[0]
    halo = xp_ref.shape[0]
    buf_ref[:halo, :] = xp_ref[...].astype(F32) * (i > 0).astype(F32)
    buf_ref[halo:halo + tm, :] = xc_ref[...].astype(F32)
    buf_ref[halo + tm:, :] = xn_ref[...].astype(F32) * (i < nt - 1).astype(F32)
    acc = jnp.zeros(xc_ref.shape, F32)
    for j in range(CONV_TAPS):
        off = halo + j - CONV_TAPS // 2
        acc = acc + buf_ref[off:off + tm, :] * cw_ref[j:j + 1, :]
    y = acc * _sigmoid(acc)
    nqk = LIN_HEADS * LIN_DK
    for idx, (o_ref, scale) in enumerate(((q_ref, LIN_DK ** -0.5), (k_ref, 1.0))):
        part = y[:, idx * nqk:(idx + 1) * nqk]
        ssq = _dot_f32_mask(part * part, gsum_ref[...])
        o_ref[...] = (part * lax.rsqrt(ssq + EPS) * scale).astype(o_ref.dtype)
    v_ref[...] = y[:, 2 * nqk:].astype(v_ref.dtype)
    sm = sm_ref[...]
    beta = _sigmoid(sm)
    g = -jnp.exp(alog_ref[...]) * _softplus(sm + dtb_ref[...])
    lane = lax.broadcasted_iota(jnp.int32, sm.shape, 1) - GATE_LANE0
    is_beta = (lane >= 0) & (lane < 2 * LIN_HEADS)
    is_g = (lane >= 2 * LIN_HEADS) & (lane < 4 * LIN_HEADS)
    g = jnp.where(is_g, g, 0.0)
    g_ref[...] = jnp.where(is_beta, beta, g)
    ri = lax.broadcasted_iota(jnp.int32, (CHUNK, CHUNK), 0)
    ci = lax.broadcasted_iota(jnp.int32, (CHUNK, CHUNK), 1)
    tril16 = (ri >= ci).astype(BF16)
    ones16 = jnp.ones((CHUNK, CHUNK), BF16)
    is_bwd = lax.broadcasted_iota(jnp.int32, (CHUNK, LANE), 1) >= GATE_LANE0 + 3 * LIN_HEADS
    for c in range(tm // CHUNK):
        rows = slice(c * CHUNK, (c + 1) * CHUNK)
        gc = g[rows, :]
        prefix = _dot_mask_f32(tril16, gc)
        total = _dot_mask_f32(ones16, gc)
        b_ref[rows, :] = jnp.where(is_bwd, total - prefix + gc, prefix)
        bt_ref[rows, :] = total


def gdn_prep(dqkv, small, conv_w, gsum, alog, dtb, batch, seq, tm):
    nt = seq // tm
    t, c = dqkv.shape
    halo = 2 * SUBLANE
    hb = tm // halo
    nqk = LIN_HEADS * LIN_DK
    row = lambda b, i: (b * nt + i, 0)
    prev = lambda b, i: (jnp.maximum((b * nt + i) * hb - 1, 0), 0)
    nxt = lambda b, i: (jnp.minimum((b * nt + i + 1) * hb, t // halo - 1), 0)
    return pl.pallas_call(
        functools.partial(_gdn_prep_kernel, nt=nt),
        grid=(batch, nt),
        in_specs=[pl.BlockSpec((halo, c), prev), pl.BlockSpec((tm, c), row), pl.BlockSpec((halo, c), nxt),
                  pl.BlockSpec((tm, LANE), row),
                  _resident(conv_w.shape), _resident(gsum.shape), _resident((1, LANE)), _resident((1, LANE))],
        out_specs=[pl.BlockSpec((tm, nqk), row), pl.BlockSpec((tm, nqk), row),
                   pl.BlockSpec((tm, c - 2 * nqk), row)] + [pl.BlockSpec((tm, LANE), row)] * 3,
        out_shape=[jax.ShapeDtypeStruct((t, nqk), BF16), jax.ShapeDtypeStruct((t, nqk), BF16),
                   jax.ShapeDtypeStruct((t, c - 2 * nqk), BF16)] + [jax.ShapeDtypeStruct((t, LANE), F32)] * 3,
        scratch_shapes=[pltpu.VMEM((tm + 2 * halo, c), F32)],
        compiler_params=_cparams("parallel", "parallel"),
        name="gdn_prep",
    )(dqkv, dqkv, dqkv, small, conv_w, gsum, alog, dtb)


def _gdn_kernel(qf_ref, kf_ref, vf_ref, gf_ref, bf_ref, tf_ref, qb_ref, kb_ref, vb_ref, gb_ref, bb_ref, tb_ref,
                of_ref, ob_ref, st_ref, *, ncb, nb):
    @pl.when(pl.program_id(1) == 0)
    def _():
        st_ref[...] = jnp.zeros_like(st_ref)

    refs = ((qf_ref, kf_ref, vf_ref, gf_ref, bf_ref, tf_ref, of_ref),
            (qb_ref, kb_ref, vb_ref, gb_ref, bb_ref, tb_ref, ob_ref))
    masks = [_direction_masks(d) for d in range(2)]
    r = lax.broadcasted_iota(jnp.int32, (STACK, STACK), 0)
    c = lax.broadcasted_iota(jnp.int32, (STACK, STACK), 1)
    diag = r == c
    eye = diag.astype(F32)
    lane_blk = lax.broadcasted_iota(jnp.int32, (SUBLANE, STACK), 1) // CHUNK

    def rows_of(x, lane0):
        return jnp.concatenate(
            [jnp.broadcast_to(x[:, lane0 + h:lane0 + h + 1], (CHUNK, STACK)) for h in range(LIN_HEADS)], axis=0)

    def head_lanes(row, lane0):
        out = jnp.broadcast_to(row[:, lane0:lane0 + 1], (SUBLANE, STACK))
        for h in range(1, LIN_HEADS):
            out = jnp.where(lane_blk == h, jnp.broadcast_to(row[:, lane0 + h:lane0 + h + 1], (SUBLANE, STACK)), out)
        return out

    each = _each
    items = [(bi, d, ci) for bi in range(nb) for d in range(2) for ci in range(ncb)]
    rows = lambda ci: slice(ci * CHUNK, (ci + 1) * CHUNK)
    chunk = lambda which, it: refs[it[1]][which][it[0], rows(it[2]), :]
    same = [masks[d][0] for _, d, _ in items]
    beta_col = [rows_of(chunk(3, it), _gate_lanes(it[1])[0]) for it in items]
    b_col = [rows_of(chunk(4, it), _gate_lanes(it[1])[1]) for it in items]
    bl_col = [rows_of(chunk(5, it), _gate_lanes(it[1])[1]) for it in items]
    k_bd = [_block_diag(chunk(1, it), sm) for it, sm in zip(items, same)]
    q_bd = [_block_diag(chunk(0, it), sm) for it, sm in zip(items, same)]
    kb_bd = each(lambda a, b: a * b, k_bd, beta_col)
    k16 = each(lambda a: a.astype(BF16), k_bd)
    kk = each(lambda a, b: _dot_nt(a.astype(BF16), b), kb_bd, k16)
    qk = each(lambda a, b: _dot_nt(a.astype(BF16), b), q_bd, k16)
    b_row = each(lambda bc: jnp.sum(jnp.where(diag, bc, 0.0), axis=0, keepdims=True), b_col)
    gamma = [jnp.exp(jnp.where(masks[it[1]][1], bc - br, -jnp.inf)) for it, bc, br in zip(items, b_col, b_row)]
    lower = [jnp.where(masks[it[1]][2], a * g, 0.0) for it, a, g in zip(items, kk, gamma)]
    attn = each(lambda a, g: (a * g).astype(BF16), qk, gamma)
    l_hi = each(lambda a: a.astype(BF16), lower)
    pw = [l_hi]
    for _ in range(5):
        pw.append(each(lambda a: _dot(a, a).astype(BF16), pw[-1]))

    def pair(a, b16):
        return a + _dot(a.astype(BF16), b16)

    f01 = each(lambda lo, p1: pair(eye - lo, p1), lower, pw[1])
    f23 = each(lambda p2, p3: pair(eye + p2.astype(F32), p3), pw[2], pw[3])
    f45 = each(lambda p4, p5: pair(eye + p4.astype(F32), p5), pw[4], pw[5])
    f0123 = each(lambda a, b: _dot(a.astype(BF16), b.astype(BF16)).astype(BF16), f01, f23)
    inv16 = each(lambda a, b: _dot(a, b.astype(BF16)).astype(BF16), f0123, f45)
    e_b = each(jnp.exp, b_col)
    vb = [_stack_heads(chunk(2, it), LIN_DV) * bc[:, :LIN_DV] for it, bc in zip(items, beta_col)]
    kbe = [_stack_heads(chunk(1, it), LIN_DK) * (bc * e)[:, :LIN_DK] for it, bc, e in zip(items, beta_col, e_b)]
    rhs = each(lambda v, kb: jnp.concatenate([v, kb], axis=1), vb, kbe)
    x16 = each(lambda m, r_: _dot(m, r_.astype(BF16)).astype(BF16), inv16, rhs)
    l_lo = each(lambda lo, hi: (lo - hi.astype(F32)).astype(BF16), lower, l_hi)
    resid = each(lambda r_, x, hi, lo: r_ - (x.astype(F32) + _dot(hi, x) + _dot(lo, x)), rhs, x16, l_hi, l_lo)
    sol = each(lambda x, m, rs: x.astype(F32) + _dot(m, rs.astype(BF16)), x16, inv16, resid)
    u = {it: sol[i][:, :LIN_DV] for i, it in enumerate(items)}
    w = {it: _block_diag_cols(sol[i][:, LIN_DV:].astype(BF16), same[i])
         for i, it in enumerate(items)}
    attn = dict(zip(items, attn))
    q_dec = {it: (q_bd[i] * e_b[i]).astype(BF16) for i, it in enumerate(items)}
    k_dec = {it: (k_bd[i] * jnp.exp(bl_col[i] - b_col[i])).astype(BF16) for i, it in enumerate(items)}
    dec = {(bi, d, ci): jnp.exp(head_lanes(refs[d][5][bi, ci * CHUNK:ci * CHUNK + SUBLANE, :],
                                           _gate_lanes(d)[1]))[:1, :] for bi, d, ci in items}

    streams = [(bi, d) for bi in range(nb) for d in range(2)]
    st = {sd: st_ref[i] for i, sd in enumerate(streams)}
    for fwd_bwd in _scan_order(ncb):
        step = [(bi, d, ci) for bi in range(nb) for d, ci in fwd_bwd]
        st16 = [st[it[:2]].astype(BF16) for it in step]
        v_new16 = [(u[it] - _dot_nt(w[it], s16)).astype(BF16) for it, s16 in zip(step, st16)]
        o = [_dot_nt(q_dec[it], s16) + _dot(attn[it], vn) for it, s16, vn in zip(step, st16, v_new16)]
        for it, od, vn in zip(step, o, v_new16):
            bi, d, ci = it
            st[(bi, d)] = dec[it] * st[(bi, d)] + _dot_tn(vn, k_dec[it])
            for h in range(LIN_HEADS):
                refs[d][6][bi, rows(ci), h * LIN_DV:(h + 1) * LIN_DV] = od[h * CHUNK:(h + 1) * CHUNK, :]
    for i, sd in enumerate(streams):
        st_ref[i] = st[sd]


def _gate_lanes(d):
    return GATE_LANE0 + d * LIN_HEADS, GATE_LANE0 + (2 + d) * LIN_HEADS


def gdn_scan(q, k, v, mixed, bcum, btot, batch, seq, tc, nb):
    nblk = seq // tc
    hk, hv = q.shape[1], v.shape[1]
    ins = [a.reshape(batch, seq, a.shape[1]) for a in (q, k, v, mixed, bcum, btot)]
    fwd = lambda bb, n: (bb, n, 0)
    bwd = lambda bb, n: (bb, nblk - 1 - n, 0)
    spec = lambda a, idx: pl.BlockSpec((nb, tc, a.shape[2]), idx)
    o_f, o_b = pl.pallas_call(
        functools.partial(_gdn_kernel, ncb=tc // CHUNK, nb=nb),
        grid=(batch // nb, nblk),
        in_specs=[spec(a, fwd) for a in ins] + [spec(a, bwd) for a in ins],
        out_specs=[pl.BlockSpec((nb, tc, hv), fwd), pl.BlockSpec((nb, tc, hv), bwd)],
        out_shape=[jax.ShapeDtypeStruct((batch, seq, hv), F32)] * 2,
        scratch_shapes=[pltpu.VMEM((2 * nb, LIN_DV, hk), F32)],
        compiler_params=_cparams("parallel", "arbitrary"),
        name="gdn_scan",
    )(*ins, *ins)
    return o_f.reshape(batch * seq, hv), o_b.reshape(batch * seq, hv)


def _rot_half_cols(w):
    half = w.shape[-1] // 2
    return jnp.concatenate([-w[..., half:], w[..., :half]], axis=-1)


def _pad_heads(w, heads, dim, offset=0):
    k = w.shape[0]
    out = jnp.zeros((k, heads, LANE), w.dtype)
    out = out.at[:, :, offset:offset + dim].set(w.reshape(k, heads, dim))
    return out.reshape(k, heads * LANE)


def _layer0_in_weight(w_in):
    aq = A_HEADS * HEAD_DIM
    akv = A_KV_HEADS * HEAD_DIM
    o = 0
    w_q = w_in[:, o:o + aq]; o += aq
    w_k = w_in[:, o:o + akv]; o += akv
    w_v = w_in[:, o:o + akv]; o += akv
    w_cq = w_in[:, o:o + B_Q_RANK]; o += B_Q_RANK
    w_ckv = w_in[:, o:o + B_KV_RANK]; o += B_KV_RANK
    w_pe = w_in[:, o:o + B_ROPE]
    zeros = jnp.zeros((w_in.shape[0], B_NOPE), w_in.dtype)
    v_lo = _pad_heads(w_v, A_KV_HEADS, HEAD_DIM, 0).reshape(-1, A_KV_HEADS, LANE)
    v_hi = _pad_heads(w_v, A_KV_HEADS, HEAD_DIM, HEAD_DIM).reshape(-1, A_KV_HEADS, LANE)
    w_v2 = jnp.stack([v_lo, v_hi], axis=2).reshape(w_in.shape[0], 2 * A_KV_HEADS * LANE)
    w_pe_rot = _rot_half_cols(w_pe)
    pieces = [_pad_heads(w_q, A_HEADS, HEAD_DIM), _pad_heads(w_k, A_KV_HEADS, HEAD_DIM), w_v2,
              w_cq, w_ckv,
              jnp.concatenate([zeros, w_pe, w_pe, zeros, w_pe_rot, w_pe_rot], axis=1)]
    segs, start = [], 0
    for p in pieces:
        segs.append((start, p.shape[1]))
        start += p.shape[1]
    return jnp.concatenate(pieces, axis=1).astype(BF16), segs


def _mla_weights(w_uq, w_ukv):
    r = w_uq.shape[0]
    wq = w_uq.reshape(r, B_HEADS, B_NOPE + B_ROPE)
    pe = wq[:, :, B_NOPE:]
    wq = jnp.concatenate([wq[:, :, :B_NOPE], pe, _rot_half_cols(pe)], axis=-1).reshape(r, B_HEADS * LANE)
    rk = w_ukv.shape[0]
    wkv = w_ukv.reshape(rk, B_HEADS, B_NOPE + B_VDIM)
    wk = _pad_heads(wkv[:, :, :B_NOPE].reshape(rk, B_HEADS * B_NOPE), B_HEADS, B_NOPE)
    wv = _pad_heads(wkv[:, :, B_NOPE:].reshape(rk, B_HEADS * B_VDIM), B_HEADS, B_VDIM)
    return wq.astype(BF16), wk.astype(BF16), wv.astype(BF16)


def _rope_tables(seq):
    half = B_ROPE // 2
    inv = ROPE_THETA ** (-jnp.arange(half, dtype=F32) / half)
    ang = jnp.arange(seq, dtype=F32)[:, None] * inv[None, :]
    cos2 = jnp.tile(jnp.cos(ang), (1, 4))
    sin2 = jnp.tile(jnp.sin(ang), (1, 4))
    zeros = jnp.zeros((seq, B_NOPE), F32)
    qscale = (B_NOPE + B_ROPE) ** -0.5 * math.log2(math.e)
    q_tab = jnp.concatenate([jnp.ones((seq, B_NOPE), F32), cos2[:, :B_ROPE], sin2[:, :B_ROPE]], axis=1) * qscale
    k_cos = jnp.concatenate([zeros, cos2], axis=1)
    k_sin = jnp.concatenate([zeros, sin2], axis=1)
    return q_tab, k_cos, k_sin


def _layer1_in_weight(w_in):
    hk = LIN_HEADS * LIN_DK
    hv = LIN_HEADS * LIN_DV
    conv_ch = 2 * hk + hv
    main = 2 * hk + 2 * hv + 2 * GATE_RANK
    o = 0
    w_main = w_in[:, :2 * hk + 2 * hv]; o = 2 * hk + 2 * hv
    w_gl = w_in[:, o:o + 2 * GATE_RANK]; o += 2 * GATE_RANK
    w_conv = w_in[:, o:o + conv_ch]; o += conv_ch
    w_z = w_in[:, o:o + hv]; o += hv
    w_small = w_in[:, o:o + 4 * LIN_HEADS]
    del main
    pad = jnp.zeros((w_in.shape[0], LANE - 2 * GATE_RANK - 4 * LIN_HEADS), w_in.dtype)
    w = jnp.concatenate([w_main, w_conv, w_z, w_gl, w_small, pad], axis=1).astype(BF16)
    widths = [hk, hk, hv, hv, conv_ch, hv, LANE]
    segs, start = [], 0
    for wd in widths:
        segs.append((start, wd))
        start += wd
    return w, segs


def _pick_tile(n, pref):
    t = min(n, pref)
    while n % t:
        t //= 2
    return t


def kernel(x, att_norm, att_w_in, att_sink, mla_q_norm, mla_w_uq, mla_kv_norm, mla_w_ukv, att_w_out, lin_norm, lin_w_in, gla_w_gate_f, gla_b_gate_f, gla_w_gate_b, gla_b_gate_b, gla_norm, gdn_conv, gdn_a_log_f, gdn_dt_bias_f, gdn_a_log_b, gdn_dt_bias_b, gdn_norm, lin_w_out, mlp_norm, mlp_w1, mlp_w2, final_norm):
    batch, seq, dm = x.shape
    t = batch * seq
    depth = mlp_norm.shape[0]
    tm = _pick_tile(seq, 512)
    xs = x.reshape(t, dm)
    for layer in range(depth):
        i = layer // 2
        if layer % 2 == 0:
            w0, segs = _layer0_in_weight(att_w_in[i])
            wq, wk, wv = _mla_weights(mla_w_uq[i], mla_w_ukv[i])
            q_tab, k_cos, k_sin = _rope_tables(seq)
            qa, ka, va, qb, kb, vb = attn_in_proj(xs, att_norm[i], w0, segs, mla_q_norm[i], wq, mla_kv_norm[i],
                                                  wk, wv, q_tab, k_cos, k_sin, seq, tm)
            o_a = window_attention(qa, ka, va, att_sink[i].astype(F32), batch, seq)
            o_b = mla_flash(qb, kb, vb, batch, seq, _pick_tile(seq, 1024), _pick_tile(seq, 2048), 64)
            mixer_inputs, small_params, w_out = [o_a, o_b], [], att_w_out[i]
        else:
            w1, segs = _layer1_in_weight(lin_w_in[i])
            cq, ck, cv, cg, dqkv, dz, small = norm_proj(xs, lin_norm[i], w1, segs, [BF16] * 6 + [F32], tm)
            hk = LIN_HEADS * LIN_DK
            wg = jnp.zeros((2, LANE, hk), F32)
            wg = wg.at[0, :GATE_RANK].set(gla_w_gate_f[i]).at[1, GATE_RANK:2 * GATE_RANK].set(gla_w_gate_b[i])
            bg = jnp.stack([gla_b_gate_f[i], gla_b_gate_b[i]]).reshape(2, 1, hk).astype(F32)
            tc = _pick_tile(seq, 256)
            o_cf, o_cb = gla_scan(cq, ck, cv, small, wg.astype(BF16), bg, batch, seq, tc)
            gsum = (jnp.arange(hk)[:, None] // LIN_DK == jnp.arange(hk)[None, :] // LIN_DK).astype(BF16)
            nh = LIN_HEADS
            a0 = GATE_LANE0 + 2 * nh
            alog = jnp.zeros((LANE,), F32).at[a0:a0 + nh].set(gdn_a_log_f[i]).at[a0 + nh:a0 + 2 * nh].set(gdn_a_log_b[i])
            dtb = jnp.zeros((LANE,), F32).at[a0:a0 + nh].set(gdn_dt_bias_f[i]).at[a0 + nh:a0 + 2 * nh].set(gdn_dt_bias_b[i])
            dq, dk, dv, mixed, bcum, btot = gdn_prep(dqkv, small, gdn_conv[i].astype(F32), gsum,
                                                     alog.reshape(1, LANE), dtb.reshape(1, LANE), batch, seq, tm)
            o_df, o_db = gdn_scan(dq, dk, dv, mixed, bcum, btot, batch, seq, tc, 2 if batch % 2 == 0 else 1)
            mixer_inputs = [o_cf, o_cb, o_df, o_db, cg, dz]
            small_params = [gla_norm[i].reshape(1, LIN_DV), gdn_norm[i].reshape(1, LIN_DV)]
            w_out = lin_w_out[i]
        last = layer == depth - 1
        xs = mixer_out_mlp(xs, mixer_inputs, small_params, w_out.astype(BF16), mlp_norm[layer],
                           mlp_w1[layer].astype(BF16), mlp_w2[layer].astype(BF16), final_norm, last, tm, 1024)
    return xs.reshape(batch, seq, dm)
```

```python
import functools
import math

import numpy as np
import jax
import jax.numpy as jnp
from jax import lax
from jax.experimental import pallas as pl
from jax.experimental.pallas import tpu as pltpu

F32 = jnp.float32
BF16 = jnp.bfloat16
EPS = 1e-6

LANE = 128
SUBLANE = 8
VMEM_LIMIT_BYTES = 56 * 1024 * 1024

HEAD_DIM = 64
A_HEADS = 8
A_KV_HEADS = 2
A_GROUP = A_HEADS // A_KV_HEADS
WINDOW = 128
BLOCK = 128
B_HEADS = 8
B_NOPE = 64
B_ROPE = 32
B_VDIM = 64
B_Q_RANK = 384
B_KV_RANK = 256
ROPE_THETA = 10000.0
LIN_HEADS = 4
LIN_DK = 64
LIN_DV = 128
GATE_RANK = 16
GATE_NORM = 16.0
CHUNK = 64
CONV_TAPS = 5
STACK = LIN_HEADS * CHUNK
GATE_LANE0 = 2 * GATE_RANK

NT_DIMS = (((1,), (1,)), ((), ()))
TN_DIMS = (((0,), (0,)), ((), ()))


def _cparams(*sem):
    return pltpu.CompilerParams(dimension_semantics=sem, vmem_limit_bytes=VMEM_LIMIT_BYTES)


def _resident(shape):
    nd = len(shape)
    return pl.BlockSpec(shape, lambda *_: (0,) * nd, pipeline_mode=pl.Buffered(1))


def _rms(x, w):
    return x * lax.rsqrt(jnp.mean(x * x, axis=-1, keepdims=True) + EPS) * w


def _sigmoid(x):
    return 1.0 / (1.0 + jnp.exp(-x))


def _softplus(x):
    return jnp.maximum(x, 0.0) + jnp.log1p(jnp.exp(-jnp.abs(x)))


def _dot(a, b):
    return jnp.dot(a, b, preferred_element_type=F32)


def _dot_nt(a, b):
    return lax.dot_general(a, b, NT_DIMS, preferred_element_type=F32)


def _dot_tn(a, b):
    return lax.dot_general(a, b, TN_DIMS, preferred_element_type=F32)


def _split3(x):
    hi = x.astype(BF16)
    r1 = x - hi.astype(F32)
    mid = r1.astype(BF16)
    return hi, mid, (r1 - mid.astype(F32)).astype(BF16)


def _dot_mask_f32(mask16, x):
    hi, mid, lo = _split3(x)
    return _dot(mask16, hi) + _dot(mask16, mid) + _dot(mask16, lo)


def _dot_f32_mask(x, mask16):
    hi, mid, lo = _split3(x)
    return _dot(hi, mask16) + _dot(mid, mask16) + _dot(lo, mask16)


def _norm_proj_kernel(x_ref, nw_ref, w_ref, *out_refs, segs):
    xn = _rms(x_ref[...], nw_ref[...]).astype(BF16)
    for (start, width), o_ref in zip(segs, out_refs):
        o_ref[...] = _dot(xn, w_ref[:, start:start + width]).astype(o_ref.dtype)


def norm_proj(x, norm_w, w, segs, dtypes, tm):
    t, k = x.shape
    n = w.shape[1]
    return pl.pallas_call(
        functools.partial(_norm_proj_kernel, segs=tuple(segs)),
        grid=(t // tm,),
        in_specs=[pl.BlockSpec((tm, k), lambda i: (i, 0)),
                  _resident((1, k)),
                  _resident((k, n))],
        out_specs=[pl.BlockSpec((tm, wd), lambda i: (i, 0)) for _, wd in segs],
        out_shape=[jax.ShapeDtypeStruct((t, wd), dt) for (_, wd), dt in zip(segs, dtypes)],
        compiler_params=_cparams("parallel"),
        name="norm_proj",
    )(x, norm_w.reshape(1, k), w)


def _win_attn_kernel(sink_ref, *refs, nq):
    bias_refs = refs[:nq]
    q_ref, kp_ref, kc_ref, kn_ref, vp_ref, vc_ref, vn_ref, o_ref = refs[nq:]
    kall = jnp.concatenate([kp_ref[...], kc_ref[...], kn_ref[...]], axis=0)
    vall = jnp.concatenate([vp_ref[...], vc_ref[...], vn_ref[...]], axis=0)
    scale = HEAD_DIM ** -0.5
    items = [(j, h) for j in range(nq) for h in range(A_HEADS)]
    qrows = lambda j: slice(j * BLOCK, (j + 1) * BLOCK)
    krows = lambda j: slice(j * BLOCK, (j + 3) * BLOCK)
    kcol = lambda h: slice((h // A_GROUP) * LANE, (h // A_GROUP + 1) * LANE)
    vcol = lambda h: slice((2 * (h // A_GROUP) + h % 2) * LANE, (2 * (h // A_GROUP) + h % 2 + 1) * LANE)
    s = [_dot_nt(q_ref[qrows(j), h * LANE:(h + 1) * LANE], kall[krows(j), kcol(h)]) for j, h in items]
    s = [sv * scale + bias_refs[j][h] for (j, h), sv in zip(items, s)]
    m = [jnp.maximum(jnp.max(sv, axis=-1, keepdims=True), sink_ref[h]) for (_, h), sv in zip(items, s)]
    p = _each(lambda sv, mv: jnp.exp(sv - mv), s, m)
    rden = [1.0 / (jnp.sum(pv_, axis=-1, keepdims=True) + jnp.exp(sink_ref[h] - mv))
            for (_, h), pv_, mv in zip(items, p, m)]
    pv = {it: _dot(pp.astype(BF16), vall[krows(it[0]), vcol(it[1])]) * rd for it, pp, rd in zip(items, p, rden)}
    for j in range(nq):
        for pair in range(A_HEADS // 2):
            o_ref[qrows(j), pair * LANE:(pair + 1) * LANE] = (pv[(j, 2 * pair)] + pv[(j, 2 * pair + 1)]).astype(o_ref.dtype)


def _window_bias():
    qi = np.arange(BLOCK)[:, None]
    kj = np.arange(3 * BLOCK)[None, :]
    dist = np.abs(qi + BLOCK - kj)
    slopes = np.array([2.0 ** (-8.0 * (h + 1) / A_HEADS) for h in range(A_HEADS)], np.float32)
    base = -slopes[:, None, None] * dist[None].astype(np.float32)
    out = []
    for first in (0, 1):
        for last in (0, 1):
            valid = (dist <= WINDOW) & ((kj >= BLOCK) | (first == 0)) & ((kj < 2 * BLOCK) | (last == 0))
            out.append(np.where(valid[None], base, -np.inf))
    return jnp.asarray(np.stack(out), F32)


def window_attention(q, k, v, sink, batch, seq):
    nb = seq // BLOCK
    nq = 2 if nb % 2 == 0 else 1
    ns = nb // nq
    cur = lambda b, n: (b * ns + n, 0)
    prev = lambda b, n: (b * nb + jnp.maximum(n * nq - 1, 0), 0)
    nxt = lambda b, n: (b * nb + jnp.minimum((n + 1) * nq, nb - 1), 0)

    def variant(j):
        first = lambda n: (n == 0).astype(jnp.int32) if j == 0 else 0
        last = lambda n: (n == ns - 1).astype(jnp.int32) if j == nq - 1 else 0
        return lambda b, n: (2 * first(n) + last(n), 0, 0, 0)

    kw, vw = k.shape[1], v.shape[1]
    bias = _window_bias()
    return pl.pallas_call(
        functools.partial(_win_attn_kernel, nq=nq),
        grid=(batch, ns),
        in_specs=([pl.BlockSpec(memory_space=pltpu.SMEM)]
                  + [pl.BlockSpec((None, A_HEADS, BLOCK, 3 * BLOCK), variant(j)) for j in range(nq)]
                  + [pl.BlockSpec((nq * BLOCK, q.shape[1]), cur),
                     pl.BlockSpec((BLOCK, kw), prev), pl.BlockSpec((nq * BLOCK, kw), cur), pl.BlockSpec((BLOCK, kw), nxt),
                     pl.BlockSpec((BLOCK, vw), prev), pl.BlockSpec((nq * BLOCK, vw), cur),
                     pl.BlockSpec((BLOCK, vw), nxt)]),
        out_specs=pl.BlockSpec((nq * BLOCK, A_HEADS * HEAD_DIM), cur),
        out_shape=jax.ShapeDtypeStruct((batch * seq, A_HEADS * HEAD_DIM), BF16),
        compiler_params=_cparams("parallel", "parallel"),
        name="window_attention",
    )(sink, *([bias] * nq), q, k, k, k, v, v, v)


def _attn_in_kernel(x_ref, nw_ref, w_ref, qnw_ref, wq_ref, kvnw_ref, wk_ref, wv_ref, qtab_ref, ctab_ref, stab_ref,
                    qa_ref, ka_ref, va_ref, q_ref, k_ref, v_ref, *, segs):
    xn = _rms(x_ref[...], nw_ref[...]).astype(BF16)
    proj = [_dot(xn, w_ref[:, start:start + width]) for start, width in segs]
    for o_ref, val in zip((qa_ref, ka_ref, va_ref), proj):
        o_ref[...] = val.astype(o_ref.dtype)
    cq, ckv, kab = proj[3:]
    xq = _rms(cq, qnw_ref[...]).astype(BF16)
    tab = qtab_ref[...]
    for h in range(B_HEADS):
        cols = slice(h * LANE, (h + 1) * LANE)
        q_ref[:, cols] = (_dot(xq, wq_ref[:, cols]) * tab).astype(q_ref.dtype)
    xkv = _rms(ckv, kvnw_ref[...]).astype(BF16)
    kr = kab[:, :LANE] * ctab_ref[...] + kab[:, LANE:] * stab_ref[...]
    for h in range(B_HEADS):
        cols = slice(h * LANE, (h + 1) * LANE)
        k_ref[:, cols] = (_dot(xkv, wk_ref[:, cols]) + kr).astype(k_ref.dtype)
    lane = lax.broadcasted_iota(jnp.int32, v_ref.shape, 1) % LANE
    v_ref[...] = jnp.where(lane == B_VDIM, 1.0, _dot(xkv, wv_ref[...])).astype(v_ref.dtype)


def attn_in_proj(x, norm_w, w, segs, q_norm_w, wq, kv_norm_w, wk, wv, q_tab, k_cos, k_sin, seq, tm):
    t, k = x.shape
    nt = seq // tm
    row = lambda i: (i, 0)
    pos = lambda i: (i % nt, 0)
    widths = [wd for _, wd in segs[:3]] + [wq.shape[1], wk.shape[1], wv.shape[1]]
    return pl.pallas_call(
        functools.partial(_attn_in_kernel, segs=tuple(segs)),
        grid=(t // tm,),
        in_specs=[pl.BlockSpec((tm, k), row), _resident((1, k)), _resident(w.shape),
                  _resident((1, wq.shape[0])), _resident(wq.shape),
                  _resident((1, wk.shape[0])), _resident(wk.shape), _resident(wv.shape),
                  pl.BlockSpec((tm, LANE), pos), pl.BlockSpec((tm, LANE), pos), pl.BlockSpec((tm, LANE), pos)],
        out_specs=[pl.BlockSpec((tm, wd), row) for wd in widths],
        out_shape=[jax.ShapeDtypeStruct((t, wd), BF16) for wd in widths],
        compiler_params=_cparams("parallel"),
        name="attn_in_proj",
    )(x, norm_w.reshape(1, k), w, q_norm_w.reshape(1, -1), wq, kv_norm_w.reshape(1, -1), wk, wv,
      q_tab, k_cos, k_sin)


def _mla_flash_kernel(q_ref, k_ref, v_ref, o_ref, s_ref, p_ref, m_ref, a_ref, acc_ref, *, tk, nk, rb):
    tq = q_ref.shape[0]
    heads = range(2)
    m_ref[...] = jnp.full(m_ref.shape, -jnp.inf, F32)
    acc_ref[...] = jnp.zeros(acc_ref.shape, F32)

    def body(j, carry):
        rows = pl.ds(pl.multiple_of(j * tk, tk), tk)
        for e in heads:
            cols = slice(e * LANE, (e + 1) * LANE)
            s_ref[e] = _dot_nt(q_ref[:, cols], k_ref[rows, cols])
        for e in heads:
            cols = slice(e * LANE, (e + 1) * LANE)
            for r in range(tq // rb):
                rs = slice(r * rb, (r + 1) * rb)
                s = s_ref[e, rs, :]
                m_old = m_ref[e, rs, :]
                m_new = jnp.maximum(m_old, jnp.max(s, axis=-1, keepdims=True))
                m_ref[e, rs, :] = m_new
                a_ref[e, rs, :] = jnp.exp2(m_old - m_new)
                p_ref[e, rs, :] = jnp.exp2(s - jnp.concatenate([m_new] * (tk // LANE), axis=1)).astype(BF16)
            acc_ref[e] = a_ref[e] * acc_ref[e] + _dot(p_ref[e], v_ref[rows, cols])
        return carry

    lax.fori_loop(0, nk, body, 0)
    outs = []
    for e in heads:
        acc = acc_ref[e]
        outs.append(acc / acc[:, B_VDIM:B_VDIM + 1])
    lane = lax.broadcasted_iota(jnp.int32, (tq, LANE), 1)
    o_ref[...] = jnp.where(lane < B_VDIM, outs[0], pltpu.roll(outs[1], B_VDIM, axis=1)).astype(o_ref.dtype)


def mla_flash(q, k, v, batch, seq, tq, tk, rb):
    nq = seq // tq
    npair = B_HEADS // 2
    return pl.pallas_call(
        functools.partial(_mla_flash_kernel, tk=tk, nk=seq // tk, rb=rb),
        grid=(batch, npair, nq),
        in_specs=[pl.BlockSpec((tq, 2 * LANE), lambda b, p, i: (b * nq + i, p)),
                  pl.BlockSpec((seq, 2 * LANE), lambda b, p, i: (b, p)),
                  pl.BlockSpec((seq, 2 * LANE), lambda b, p, i: (b, p))],
        out_specs=pl.BlockSpec((tq, LANE), lambda b, p, i: (b * nq + i, p)),
        out_shape=jax.ShapeDtypeStruct((batch * seq, B_HEADS * B_VDIM), BF16),
        scratch_shapes=[pltpu.VMEM((2, tq, tk), F32), pltpu.VMEM((2, tq, tk), BF16), pltpu.VMEM((2, tq, LANE), F32),
                        pltpu.VMEM((2, tq, LANE), F32), pltpu.VMEM((2, tq, LANE), F32)],
        compiler_params=_cparams("parallel", "parallel", "parallel"),
        name="mla_flash",
    )(q, k, v)


def _mlp_tail(x, nw_ref, w1_ref, w2_ref, fw_ref, o_ref, fchunk, final_norm):
    xn = _rms(x, nw_ref[...]).astype(BF16)
    acc = x
    for f in range(w1_ref.shape[1] // fchunk):
        cols = slice(f * fchunk, (f + 1) * fchunk)
        h = jnp.square(jnp.maximum(_dot(xn, w1_ref[:, cols]), 0.0)).astype(BF16)
        acc = acc + _dot(h, w2_ref[cols, :])
    if final_norm:
        acc = _rms(acc, fw_ref[...])
    o_ref[...] = acc


def _attn_out_mlp_kernel(x_ref, a_ref, b_ref, wo_ref, nw_ref, w1_ref, w2_ref, fw_ref, o_ref, *, fchunk, final_norm):
    ka = a_ref.shape[1]
    o_ref[...] = x_ref[...] + _dot(a_ref[...], wo_ref[:ka, :]) + _dot(b_ref[...], wo_ref[ka:, :])
    _mlp_tail(o_ref[...], nw_ref, w1_ref, w2_ref, fw_ref, o_ref, fchunk, final_norm)


def _lin_out_mlp_kernel(x_ref, cf_ref, cb_ref, df_ref, db_ref, gc_ref, gd_ref, cw_ref, dw_ref, wo_ref,
                        nw_ref, w1_ref, w2_ref, fw_ref, o_ref, *, fchunk, final_norm):
    def gated(f_ref, b_ref, gate_ref, gnw_ref):
        o = f_ref[...] + b_ref[...]
        parts = []
        for h in range(LIN_HEADS):
            cols = slice(h * LIN_DV, (h + 1) * LIN_DV)
            gate = gate_ref[:, cols].astype(F32)
            parts.append((_rms(o[:, cols], gnw_ref[...]) * (gate * _sigmoid(gate))).astype(BF16))
        return jnp.concatenate(parts, axis=1)

    kc = cf_ref.shape[1]
    o_ref[...] = (x_ref[...] + _dot(gated(cf_ref, cb_ref, gc_ref, cw_ref), wo_ref[:kc, :])
                  + _dot(gated(df_ref, db_ref, gd_ref, dw_ref), wo_ref[kc:, :]))
    _mlp_tail(o_ref[...], nw_ref, w1_ref, w2_ref, fw_ref, o_ref, fchunk, final_norm)


def mixer_out_mlp(x, mixer_inputs, small_params, w_out, norm_w, w1, w2, final_w, final_norm, tm, fchunk):
    t, d = x.shape
    row = lambda i: (i, 0)
    body = _attn_out_mlp_kernel if len(mixer_inputs) == 2 else _lin_out_mlp_kernel
    return pl.pallas_call(
        functools.partial(body, fchunk=fchunk, final_norm=final_norm),
        grid=(t // tm,),
        in_specs=([pl.BlockSpec((tm, d), row)] + [pl.BlockSpec((tm, a.shape[1]), row) for a in mixer_inputs]
                  + [_resident(p.shape) for p in small_params]
                  + [_resident(w_out.shape), _resident((1, d)), _resident(w1.shape), _resident(w2.shape),
                     _resident((1, d))]),
        out_specs=pl.BlockSpec((tm, d), row),
        out_shape=jax.ShapeDtypeStruct((t, d), F32),
        compiler_params=_cparams("parallel"),
        name="mixer_out_mlp",
    )(x, *mixer_inputs, *small_params, w_out, norm_w.reshape(1, d), w1, w2, final_w.reshape(1, d))


def _direction_masks(d):
    sgn = 1 - 2 * d
    r = lax.broadcasted_iota(jnp.int32, (STACK, STACK), 0)
    c = lax.broadcasted_iota(jnp.int32, (STACK, STACK), 1)
    same = (r & -CHUNK) == (c & -CHUNK)
    order = ((r & (CHUNK - 1)) - (c & (CHUNK - 1))) * sgn
    i = lax.broadcasted_iota(jnp.int32, (CHUNK, CHUNK), 0)
    j = lax.broadcasted_iota(jnp.int32, (CHUNK, CHUNK), 1)
    cum = (((i - j) * sgn) >= 0).astype(F32)
    return same, same & (order >= 0), same & (order > 0), cum


def _block_diag(x, same):
    return jnp.where(same, jnp.concatenate([x] * LIN_HEADS, axis=0), 0.0)


def _block_diag_cols(x, same):
    return jnp.where(same, jnp.concatenate([x] * LIN_HEADS, axis=1), 0.0)


def _stack_heads(v, width):
    return jnp.concatenate([v[:, h * width:(h + 1) * width] for h in range(LIN_HEADS)], axis=0)


def _each(f, *lists):
    return [f(*args) for args in zip(*lists)]


def _scan_order(ncb):
    return [((0, i), (1, ncb - 1 - i)) for i in range(ncb)]


def _gla_kernel(qf_ref, kf_ref, vf_ref, sf_ref, qb_ref, kb_ref, vb_ref, sb_ref, wg_ref, bg_ref,
                of_ref, ob_ref, st_ref, *, ncb, nb):
    @pl.when(pl.program_id(1) == 0)
    def _():
        st_ref[...] = jnp.zeros_like(st_ref)

    refs = ((qf_ref, kf_ref, vf_ref, sf_ref, of_ref), (qb_ref, kb_ref, vb_ref, sb_ref, ob_ref))
    masks = [_direction_masks(d) for d in range(2)]
    items = [(bi, d, ci) for bi in range(nb) for d in range(2) for ci in range(ncb)]
    rows = lambda ci: slice(ci * CHUNK, (ci + 1) * CHUNK)
    chunk = lambda which, it: refs[it[1]][which][it[0], rows(it[2]), :]
    same = [masks[d][0] for _, d, _ in items]
    z = [_dot(chunk(3, it).astype(BF16), wg_ref[it[1]]) + bg_ref[it[1]] for it in items]
    log_a = _each(lambda a: (jnp.minimum(a, 0.0) - jnp.log1p(jnp.exp(-jnp.abs(a)))) * (1.0 / GATE_NORM), z)
    b = [_dot_mask_f32(masks[it[1]][3].astype(BF16), a) for it, a in zip(items, log_a)]
    b_last = _each(lambda a: jnp.sum(a, axis=0, keepdims=True), log_a)
    q_in = [_block_diag(chunk(0, it) * (LIN_DK ** -0.5) * jnp.exp(bb), sm).astype(BF16)
            for it, bb, sm in zip(items, b, same)]
    k = [chunk(1, it) for it in items]
    k_in = _each(lambda a, bb, sm: _block_diag(a * jnp.exp(-bb), sm).astype(BF16), k, b, same)
    k_out = _each(lambda a, bb, bl, sm: _block_diag(a * jnp.exp(bl - bb), sm).astype(BF16), k, b, b_last, same)
    v = [_stack_heads(chunk(2, it), LIN_DV).astype(BF16) for it in items]
    sc = [jnp.where(masks[it[1]][1], _dot_nt(a, bb), 0.0).astype(BF16) for it, a, bb in zip(items, q_in, k_in)]
    o_intra = dict(zip(items, _each(_dot, sc, v)))
    ds = dict(zip(items, _each(_dot_tn, v, k_out)))
    dec = dict(zip(items, _each(jnp.exp, b_last)))
    q_in = dict(zip(items, q_in))

    streams = [(bi, d) for bi in range(nb) for d in range(2)]
    st = {sd: st_ref[i] for i, sd in enumerate(streams)}
    for fwd_bwd in _scan_order(ncb):
        step = [(bi, d, ci) for bi in range(nb) for d, ci in fwd_bwd]
        o = [o_intra[it] + _dot_nt(q_in[it], st[it[:2]].astype(BF16)) for it in step]
        for it, od in zip(step, o):
            bi, d, ci = it
            st[(bi, d)] = dec[it] * st[(bi, d)] + ds[it]
            for h in range(LIN_HEADS):
                refs[d][4][bi, rows(ci), h * LIN_DV:(h + 1) * LIN_DV] = od[h * CHUNK:(h + 1) * CHUNK, :]
    for i, sd in enumerate(streams):
        st_ref[i] = st[sd]


def _bidir_call(body, name, arrays, extra, extra_specs, batch, seq, tc, nb, hk, hv):
    nblk = seq // tc
    ins = [a.reshape(batch, seq, a.shape[1]) for a in arrays]
    fwd = lambda bb, n: (bb, n, 0)
    bwd = lambda bb, n: (bb, nblk - 1 - n, 0)
    spec = lambda a, idx: pl.BlockSpec((nb, tc, a.shape[2]), idx)
    o_f, o_b = pl.pallas_call(
        functools.partial(body, ncb=tc // CHUNK, nb=nb),
        grid=(batch // nb, nblk),
        in_specs=[spec(a, fwd) for a in ins] + [spec(a, bwd) for a in ins] + extra_specs,
        out_specs=[pl.BlockSpec((nb, tc, hv), fwd), pl.BlockSpec((nb, tc, hv), bwd)],
        out_shape=[jax.ShapeDtypeStruct((batch, seq, hv), F32)] * 2,
        scratch_shapes=[pltpu.VMEM((2 * nb, LIN_DV, hk), F32)],
        compiler_params=_cparams("parallel", "arbitrary"),
        name=name,
    )(*ins, *ins, *extra)
    return o_f.reshape(batch * seq, hv), o_b.reshape(batch * seq, hv)


def gla_scan(q, k, v, small, wg, bg, batch, seq, tc, nb):
    return _bidir_call(_gla_kernel, "gla_scan", (q, k, v, small), (wg, bg),
                       [_resident(wg.shape), _resident(bg.shape)], batch, seq, tc, nb, q.shape[1], v.shape[1])


def _gdn_prep_kernel(xp_ref, xc_ref, xn_ref, sm_ref, cw_ref, gsum_ref, alog_ref, dtb_ref,
                     q_ref, k_ref, v_ref, g_ref, b_ref, bt_ref, buf_ref, *, nt):
    i = pl.program_id(1)
    tm = xc_ref.shape[0]
    halo = xp_ref.shape[0]
    buf_ref[:halo, :] = xp_ref[...].astype(F32) * (i > 0).astype(F32)
    buf_ref[halo:halo + tm, :] = xc_ref[...].astype(F32)
    buf_ref[halo + tm:, :] = xn_ref[...].astype(F32) * (i < nt - 1).astype(F32)
    acc = jnp.zeros(xc_ref.shape, F32)
    for j in range(CONV_TAPS):
        off = halo + j - CONV_TAPS // 2
        acc = acc + buf_ref[off:off + tm, :] * cw_ref[j:j + 1, :]
    y = acc * _sigmoid(acc)
    nqk = LIN_HEADS * LIN_DK
    for idx, (o_ref, scale) in enumerate(((q_ref, LIN_DK ** -0.5), (k_ref, 1.0))):
        part = y[:, idx * nqk:(idx + 1) * nqk]
        ssq = _dot_f32_mask(part * part, gsum_ref[...])
        o_ref[...] = (part * lax.rsqrt(ssq + EPS) * scale).astype(o_ref.dtype)
    v_ref[...] = y[:, 2 * nqk:].astype(v_ref.dtype)
    sm = sm_ref[...]
    beta = _sigmoid(sm)
    g = -jnp.exp(alog_ref[...]) * _softplus(sm + dtb_ref[...])
    lane = lax.broadcasted_iota(jnp.int32, sm.shape, 1) - GATE_LANE0
    is_beta = (lane >= 0) & (lane < 2 * LIN_HEADS)
    is_g = (lane >= 2 * LIN_HEADS) & (lane < 4 * LIN_HEADS)
    g = jnp.where(is_g, g, 0.0)
    g_ref[...] = jnp.where(is_beta, beta, g)
    ri = lax.broadcasted_iota(jnp.int32, (CHUNK, CHUNK), 0)
    ci = lax.broadcasted_iota(jnp.int32, (CHUNK, CHUNK), 1)
    tril16 = (ri >= ci).astype(BF16)
    ones16 = jnp.ones((CHUNK, CHUNK), BF16)
    is_bwd = lax.broadcasted_iota(jnp.int32, (CHUNK, LANE), 1) >= GATE_LANE0 + 3 * LIN_HEADS
    for c in range(tm // CHUNK):
        rows = slice(c * CHUNK, (c + 1) * CHUNK)
        gc = g[rows, :]
        prefix = _dot_mask_f32(tril16, gc)
        total = _dot_mask_f32(ones16, gc)
        b_ref[rows, :] = jnp.where(is_bwd, total - prefix + gc, prefix)
        bt_ref[rows, :] = total


def gdn_prep(dqkv, small, conv_w, gsum, alog, dtb, batch, seq, tm):
    nt = seq // tm
    t, c = dqkv.shape
    halo = 2 * SUBLANE
    hb = tm // halo
    nqk = LIN_HEADS * LIN_DK
    row = lambda b, i: (b * nt + i, 0)
    prev = lambda b, i: (jnp.maximum((b * nt + i) * hb - 1, 0), 0)
    nxt = lambda b, i: (jnp.minimum((b * nt + i + 1) * hb, t // halo - 1), 0)
    return pl.pallas_call(
        functools.partial(_gdn_prep_kernel, nt=nt),
        grid=(batch, nt),
        in_specs=[pl.BlockSpec((halo, c), prev), pl.BlockSpec((tm, c), row), pl.BlockSpec((halo, c), nxt),
                  pl.BlockSpec((tm, LANE), row),
                  _resident(conv_w.shape), _resident(gsum.shape), _resident((1, LANE)), _resident((1, LANE))],
        out_specs=[pl.BlockSpec((tm, nqk), row), pl.BlockSpec((tm, nqk), row),
                   pl.BlockSpec((tm, c - 2 * nqk), row)] + [pl.BlockSpec((tm, LANE), row)] * 3,
        out_shape=[jax.ShapeDtypeStruct((t, nqk), BF16), jax.ShapeDtypeStruct((t, nqk), BF16),
                   jax.ShapeDtypeStruct((t, c - 2 * nqk), BF16)] + [jax.ShapeDtypeStruct((t, LANE), F32)] * 3,
        scratch_shapes=[pltpu.VMEM((tm + 2 * halo, c), F32)],
        compiler_params=_cparams("parallel", "parallel"),
        name="gdn_prep",
    )(dqkv, dqkv, dqkv, small, conv_w, gsum, alog, dtb)


def _gdn_kernel(qf_ref, kf_ref, vf_ref, gf_ref, bf_ref, tf_ref, qb_ref, kb_ref, vb_ref, gb_ref, bb_ref, tb_ref,
                of_ref, ob_ref, st_ref, *, ncb, nb):
    @pl.when(pl.program_id(1) == 0)
    def _():
        st_ref[...] = jnp.zeros_like(st_ref)

    refs = ((qf_ref, kf_ref, vf_ref, gf_ref, bf_ref, tf_ref, of_ref),
            (qb_ref, kb_ref, vb_ref, gb_ref, bb_ref, tb_ref, ob_ref))
    masks = [_direction_masks(d) for d in range(2)]
    r = lax.broadcasted_iota(jnp.int32, (STACK, STACK), 0)
    c = lax.broadcasted_iota(jnp.int32, (STACK, STACK), 1)
    diag = r == c
    eye = diag.astype(F32)
    lane_blk = lax.broadcasted_iota(jnp.int32, (SUBLANE, STACK), 1) // CHUNK

    def rows_of(x, lane0):
        return jnp.concatenate(
            [jnp.broadcast_to(x[:, lane0 + h:lane0 + h + 1], (CHUNK, STACK)) for h in range(LIN_HEADS)], axis=0)

    def head_lanes(row, lane0):
        out = jnp.broadcast_to(row[:, lane0:lane0 + 1], (SUBLANE, STACK))
        for h in range(1, LIN_HEADS):
            out = jnp.where(lane_blk == h, jnp.broadcast_to(row[:, lane0 + h:lane0 + h + 1], (SUBLANE, STACK)), out)
        return out

    each = _each
    items = [(bi, d, ci) for bi in range(nb) for d in range(2) for ci in range(ncb)]
    rows = lambda ci: slice(ci * CHUNK, (ci + 1) * CHUNK)
    chunk = lambda which, it: refs[it[1]][which][it[0], rows(it[2]), :]
    same = [masks[d][0] for _, d, _ in items]
    beta_col = [rows_of(chunk(3, it), _gate_lanes(it[1])[0]) for it in items]
    b_col = [rows_of(chunk(4, it), _gate_lanes(it[1])[1]) for it in items]
    bl_col = [rows_of(chunk(5, it), _gate_lanes(it[1])[1]) for it in items]
    k_bd = [_block_diag(chunk(1, it), sm) for it, sm in zip(items, same)]
    q_bd = [_block_diag(chunk(0, it), sm) for it, sm in zip(items, same)]
    kb_bd = each(lambda a, b: a * b, k_bd, beta_col)
    k16 = each(lambda a: a.astype(BF16), k_bd)
    kk = each(lambda a, b: _dot_nt(a.astype(BF16), b), kb_bd, k16)
    qk = each(lambda a, b: _dot_nt(a.astype(BF16), b), q_bd, k16)
    b_row = each(lambda bc: jnp.sum(jnp.where(diag, bc, 0.0), axis=0, keepdims=True), b_col)
    gamma = [jnp.exp(jnp.where(masks[it[1]][1], bc - br, -jnp.inf)) for it, bc, br in zip(items, b_col, b_row)]
    lower = [jnp.where(masks[it[1]][2], a * g, 0.0) for it, a, g in zip(items, kk, gamma)]
    attn = each(lambda a, g: (a * g).astype(BF16), qk, gamma)
    l_hi = each(lambda a: a.astype(BF16), lower)
    pw = [l_hi]
    for _ in range(5):
        pw.append(each(lambda a: _dot(a, a).astype(BF16), pw[-1]))

    def pair(a, b16):
        return a + _dot(a.astype(BF16), b16)

    f01 = each(lambda lo, p1: pair(eye - lo, p1), lower, pw[1])
    f23 = each(lambda p2, p3: pair(eye + p2.astype(F32), p3), pw[2], pw[3])
    f45 = each(lambda p4, p5: pair(eye + p4.astype(F32), p5), pw[4], pw[5])
    f0123 = each(lambda a, b: _dot(a.astype(BF16), b.astype(BF16)).astype(BF16), f01, f23)
    inv16 = each(lambda a, b: _dot(a, b.astype(BF16)).astype(BF16), f0123, f45)
    e_b = each(jnp.exp, b_col)
    vb = [_stack_heads(chunk(2, it), LIN_DV) * bc[:, :LIN_DV] for it, bc in zip(items, beta_col)]
    kbe = [_stack_heads(chunk(1, it), LIN_DK) * (bc * e)[:, :LIN_DK] for it, bc, e in zip(items, beta_col, e_b)]
    rhs = each(lambda v, kb: jnp.concatenate([v, kb], axis=1), vb, kbe)
    x16 = each(lambda m, r_: _dot(m, r_.astype(BF16)).astype(BF16), inv16, rhs)
    l_lo = each(lambda lo, hi: (lo - hi.astype(F32)).astype(BF16), lower, l_hi)
    resid = each(lambda r_, x, hi, lo: r_ - (x.astype(F32) + _dot(hi, x) + _dot(lo, x)), rhs, x16, l_hi, l_lo)
    sol = each(lambda x, m, rs: x.astype(F32) + _dot(m, rs.astype(BF16)), x16, inv16, resid)
    u = {it: sol[i][:, :LIN_DV] for i, it in enumerate(items)}
    w = {it: _block_diag_cols(sol[i][:, LIN_DV:].astype(BF16), same[i])
         for i, it in enumerate(items)}
    attn = dict(zip(items, attn))
    q_dec = {it: (q_bd[i] * e_b[i]).astype(BF16) for i, it in enumerate(items)}
    k_dec = {it: (k_bd[i] * jnp.exp(bl_col[i] - b_col[i])).astype(BF16) for i, it in enumerate(items)}
    dec = {(bi, d, ci): jnp.exp(head_lanes(refs[d][5][bi, ci * CHUNK:ci * CHUNK + SUBLANE, :],
                                           _gate_lanes(d)[1]))[:1, :] for bi, d, ci in items}

    streams = [(bi, d) for bi in range(nb) for d in range(2)]
    st = {sd: st_ref[i] for i, sd in enumerate(streams)}
    for fwd_bwd in _scan_order(ncb):
        step = [(bi, d, ci) for bi in range(nb) for d, ci in fwd_bwd]
        st16 = [st[it[:2]].astype(BF16) for it in step]
        v_new16 = [(u[it] - _dot_nt(w[it], s16)).astype(BF16) for it, s16 in zip(step, st16)]
        o = [_dot_nt(q_dec[it], s16) + _dot(attn[it], vn) for it, s16, vn in zip(step, st16, v_new16)]
        for it, od, vn in zip(step, o, v_new16):
            bi, d, ci = it
            st[(bi, d)] = dec[it] * st[(bi, d)] + _dot_tn(vn, k_dec[it])
            for h in range(LIN_HEADS):
                refs[d][6][bi, rows(ci), h * LIN_DV:(h + 1) * LIN_DV] = od[h * CHUNK:(h + 1) * CHUNK, :]
    for i, sd in enumerate(streams):
        st_ref[i] = st[sd]


def _gate_lanes(d):
    return GATE_LANE0 + d * LIN_HEADS, GATE_LANE0 + (2 + d) * LIN_HEADS


def gdn_scan(q, k, v, mixed, bcum, btot, batch, seq, tc, nb):
    return _bidir_call(_gdn_kernel, "gdn_scan", (q, k, v, mixed, bcum, btot), (), [],
                       batch, seq, tc, nb, q.shape[1], v.shape[1])


def _rot_half_cols(w):
    half = w.shape[-1] // 2
    return jnp.concatenate([-w[..., half:], w[..., :half]], axis=-1)


def _pad_heads(w, heads, dim, offset=0):
    k = w.shape[0]
    out = jnp.zeros((k, heads, LANE), w.dtype)
    out = out.at[:, :, offset:offset + dim].set(w.reshape(k, heads, dim))
    return out.reshape(k, heads * LANE)


def _layer0_in_weight(w_in):
    aq = A_HEADS * HEAD_DIM
    akv = A_KV_HEADS * HEAD_DIM
    o = 0
    w_q = w_in[:, o:o + aq]; o += aq
    w_k = w_in[:, o:o + akv]; o += akv
    w_v = w_in[:, o:o + akv]; o += akv
    w_cq = w_in[:, o:o + B_Q_RANK]; o += B_Q_RANK
    w_ckv = w_in[:, o:o + B_KV_RANK]; o += B_KV_RANK
    w_pe = w_in[:, o:o + B_ROPE]
    zeros = jnp.zeros((w_in.shape[0], B_NOPE), w_in.dtype)
    v_lo = _pad_heads(w_v, A_KV_HEADS, HEAD_DIM, 0).reshape(-1, A_KV_HEADS, LANE)
    v_hi = _pad_heads(w_v, A_KV_HEADS, HEAD_DIM, HEAD_DIM).reshape(-1, A_KV_HEADS, LANE)
    w_v2 = jnp.stack([v_lo, v_hi], axis=2).reshape(w_in.shape[0], 2 * A_KV_HEADS * LANE)
    w_pe_rot = _rot_half_cols(w_pe)
    pieces = [_pad_heads(w_q, A_HEADS, HEAD_DIM), _pad_heads(w_k, A_KV_HEADS, HEAD_DIM), w_v2,
              w_cq, w_ckv,
              jnp.concatenate([zeros, w_pe, w_pe, zeros, w_pe_rot, w_pe_rot], axis=1)]
    segs, start = [], 0
    for p in pieces:
        segs.append((start, p.shape[1]))
        start += p.shape[1]
    return jnp.concatenate(pieces, axis=1).astype(BF16), segs


def _mla_weights(w_uq, w_ukv):
    r = w_uq.shape[0]
    wq = w_uq.reshape(r, B_HEADS, B_NOPE + B_ROPE)
    pe = wq[:, :, B_NOPE:]
    wq = jnp.concatenate([wq[:, :, :B_NOPE], pe, _rot_half_cols(pe)], axis=-1).reshape(r, B_HEADS * LANE)
    rk = w_ukv.shape[0]
    wkv = w_ukv.reshape(rk, B_HEADS, B_NOPE + B_VDIM)
    wk = _pad_heads(wkv[:, :, :B_NOPE].reshape(rk, B_HEADS * B_NOPE), B_HEADS, B_NOPE)
    wv = _pad_heads(wkv[:, :, B_NOPE:].reshape(rk, B_HEADS * B_VDIM), B_HEADS, B_VDIM)
    return wq.astype(BF16), wk.astype(BF16), wv.astype(BF16)


def _rope_tables(seq):
    half = B_ROPE // 2
    inv = ROPE_THETA ** (-jnp.arange(half, dtype=F32) / half)
    ang = jnp.arange(seq, dtype=F32)[:, None] * inv[None, :]
    cos2 = jnp.tile(jnp.cos(ang), (1, 4))
    sin2 = jnp.tile(jnp.sin(ang), (1, 4))
    zeros = jnp.zeros((seq, B_NOPE), F32)
    qscale = (B_NOPE + B_ROPE) ** -0.5 * math.log2(math.e)
    q_tab = jnp.concatenate([jnp.ones((seq, B_NOPE), F32), cos2[:, :B_ROPE], sin2[:, :B_ROPE]], axis=1) * qscale
    k_cos = jnp.concatenate([zeros, cos2], axis=1)
    k_sin = jnp.concatenate([zeros, sin2], axis=1)
    return q_tab, k_cos, k_sin


def _layer1_in_weight(w_in):
    hk = LIN_HEADS * LIN_DK
    hv = LIN_HEADS * LIN_DV
    conv_ch = 2 * hk + hv
    main = 2 * hk + 2 * hv + 2 * GATE_RANK
    o = 0
    w_main = w_in[:, :2 * hk + 2 * hv]; o = 2 * hk + 2 * hv
    w_gl = w_in[:, o:o + 2 * GATE_RANK]; o += 2 * GATE_RANK
    w_conv = w_in[:, o:o + conv_ch]; o += conv_ch
    w_z = w_in[:, o:o + hv]; o += hv
    w_small = w_in[:, o:o + 4 * LIN_HEADS]
    del main
    pad = jnp.zeros((w_in.shape[0], LANE - 2 * GATE_RANK - 4 * LIN_HEADS), w_in.dtype)
    w = jnp.concatenate([w_main, w_conv, w_z, w_gl, w_small, pad], axis=1).astype(BF16)
    widths = [hk, hk, hv, hv, conv_ch, hv, LANE]
    segs, start = [], 0
    for wd in widths:
        segs.append((start, wd))
        start += wd
    return w, segs


def _pick_tile(n, pref):
    t = min(n, pref)
    while n % t:
        t //= 2
    return t


def kernel(x, att_norm, att_w_in, att_sink, mla_q_norm, mla_w_uq, mla_kv_norm, mla_w_ukv, att_w_out, lin_norm, lin_w_in, gla_w_gate_f, gla_b_gate_f, gla_w_gate_b, gla_b_gate_b, gla_norm, gdn_conv, gdn_a_log_f, gdn_dt_bias_f, gdn_a_log_b, gdn_dt_bias_b, gdn_norm, lin_w_out, mlp_norm, mlp_w1, mlp_w2, final_norm):
    batch, seq, dm = x.shape
    t = batch * seq
    depth = mlp_norm.shape[0]
    tm = _pick_tile(seq, 512)
    xs = x.reshape(t, dm)
    for layer in range(depth):
        i = layer // 2
        if layer % 2 == 0:
            w0, segs = _layer0_in_weight(att_w_in[i])
            wq, wk, wv = _mla_weights(mla_w_uq[i], mla_w_ukv[i])
            q_tab, k_cos, k_sin = _rope_tables(seq)
            qa, ka, va, qb, kb, vb = attn_in_proj(xs, att_norm[i], w0, segs, mla_q_norm[i], wq, mla_kv_norm[i],
                                                  wk, wv, q_tab, k_cos, k_sin, seq, tm)
            o_a = window_attention(qa, ka, va, att_sink[i].astype(F32), batch, seq)
            o_b = mla_flash(qb, kb, vb, batch, seq, _pick_tile(seq, 1024), _pick_tile(seq, 2048), 64)
            mixer_inputs, small_params, w_out = [o_a, o_b], [], att_w_out[i]
        else:
            w1, segs = _layer1_in_weight(lin_w_in[i])
            cq, ck, cv, cg, dqkv, dz, small = norm_proj(xs, lin_norm[i], w1, segs, [BF16] * 6 + [F32], tm)
            hk = LIN_HEADS * LIN_DK
            wg = jnp.zeros((2, LANE, hk), F32)
            wg = wg.at[0, :GATE_RANK].set(gla_w_gate_f[i]).at[1, GATE_RANK:2 * GATE_RANK].set(gla_w_gate_b[i])
            bg = jnp.stack([gla_b_gate_f[i], gla_b_gate_b[i]]).reshape(2, 1, hk).astype(F32)
            tc = _pick_tile(seq, 256)
            nb = 2 if batch % 2 == 0 else 1
            o_cf, o_cb = gla_scan(cq, ck, cv, small, wg.astype(BF16), bg, batch, seq, tc, nb)
            gsum = (jnp.arange(hk)[:, None] // LIN_DK == jnp.arange(hk)[None, :] // LIN_DK).astype(BF16)
            nh = LIN_HEADS
            a0 = GATE_LANE0 + 2 * nh
            alog = jnp.zeros((LANE,), F32).at[a0:a0 + nh].set(gdn_a_log_f[i]).at[a0 + nh:a0 + 2 * nh].set(gdn_a_log_b[i])
            dtb = jnp.zeros((LANE,), F32).at[a0:a0 + nh].set(gdn_dt_bias_f[i]).at[a0 + nh:a0 + 2 * nh].set(gdn_dt_bias_b[i])
            dq, dk, dv, mixed, bcum, btot = gdn_prep(dqkv, small, gdn_conv[i].astype(F32), gsum,
                                                     alog.reshape(1, LANE), dtb.reshape(1, LANE), batch, seq, tm)
            o_df, o_db = gdn_scan(dq, dk, dv, mixed, bcum, btot, batch, seq, tc, nb)
            mixer_inputs = [o_cf, o_cb, o_df, o_db, cg, dz]
            small_params = [gla_norm[i].reshape(1, LIN_DV), gdn_norm[i].reshape(1, LIN_DV)]
            w_out = lin_w_out[i]
        last = layer == depth - 1
        xs = mixer_out_mlp(xs, mixer_inputs, small_params, w_out.astype(BF16), mlp_norm[layer],
                           mlp_w1[layer].astype(BF16), mlp_w2[layer].astype(BF16), final_norm, last, tm, 1024)
    return xs.reshape(batch, seq, dm)
```

```python
import functools
import math

import numpy as np
import jax
import jax.numpy as jnp
from jax import lax
from jax.experimental import pallas as pl
from jax.experimental.pallas import tpu as pltpu

F32 = jnp.float32
BF16 = jnp.bfloat16
EPS = 1e-6

LANE = 128
SUBLANE = 8
VMEM_LIMIT_BYTES = 56 * 1024 * 1024

ROW_TILE = 512
MLP_F_CHUNK = 1024
FLASH_Q_TILE = 1024
FLASH_KV_TILE = 2048
FLASH_ROW_BLOCK = 64
SCAN_ROWS = 256

HEAD_DIM = 64
A_HEADS = 8
A_KV_HEADS = 2
A_GROUP = A_HEADS // A_KV_HEADS
WINDOW = 128
BLOCK = 128
B_HEADS = 8
B_NOPE = 64
B_ROPE = 32
B_VDIM = 64
B_Q_RANK = 384
B_KV_RANK = 256
ROPE_THETA = 10000.0
LIN_HEADS = 4
LIN_DK = 64
LIN_DV = 128
GATE_RANK = 16
GATE_NORM = 16.0
CHUNK = 64
CONV_TAPS = 5
STACK = LIN_HEADS * CHUNK
GATE_LANE0 = 2 * GATE_RANK

LOG2E = math.log2(math.e)
WINDOW_Q_SCALE = HEAD_DIM ** -0.5 * LOG2E

NT_DIMS = (((1,), (1,)), ((), ()))
TN_DIMS = (((0,), (0,)), ((), ()))


def _cparams(*sem):
    return pltpu.CompilerParams(dimension_semantics=sem, vmem_limit_bytes=VMEM_LIMIT_BYTES)


def _resident(shape):
    nd = len(shape)
    return pl.BlockSpec(shape, lambda *_: (0,) * nd, pipeline_mode=pl.Buffered(1))


def _rms(x, w):
    return x * lax.rsqrt(jnp.mean(x * x, axis=-1, keepdims=True) + EPS) * w


def _sigmoid(x):
    return 1.0 / (1.0 + jnp.exp(-x))


def _softplus(x):
    return jnp.maximum(x, 0.0) + jnp.log1p(jnp.exp(-jnp.abs(x)))


def _dot(a, b):
    return jnp.dot(a, b, preferred_element_type=F32)


def _dot_nt(a, b):
    return lax.dot_general(a, b, NT_DIMS, preferred_element_type=F32)


def _dot_tn(a, b):
    return lax.dot_general(a, b, TN_DIMS, preferred_element_type=F32)


def _split3(x):
    hi = x.astype(BF16)
    r1 = x - hi.astype(F32)
    mid = r1.astype(BF16)
    return hi, mid, (r1 - mid.astype(F32)).astype(BF16)


def _dot_mask_f32(mask16, x):
    hi, mid, lo = _split3(x)
    return _dot(mask16, hi) + _dot(mask16, mid) + _dot(mask16, lo)


def _dot_f32_mask(x, mask16):
    hi, mid, lo = _split3(x)
    return _dot(hi, mask16) + _dot(mid, mask16) + _dot(lo, mask16)


def _norm_proj_kernel(x_ref, nw_ref, w_ref, *out_refs, segs):
    xn = _rms(x_ref[...], nw_ref[...]).astype(BF16)
    for (start, width), o_ref in zip(segs, out_refs):
        o_ref[...] = _dot(xn, w_ref[:, start:start + width]).astype(o_ref.dtype)


def norm_proj(x, norm_w, w, segs, dtypes, tm):
    t, k = x.shape
    n = w.shape[1]
    return pl.pallas_call(
        functools.partial(_norm_proj_kernel, segs=tuple(segs)),
        grid=(t // tm,),
        in_specs=[pl.BlockSpec((tm, k), lambda i: (i, 0)),
                  _resident((1, k)),
                  _resident((k, n))],
        out_specs=[pl.BlockSpec((tm, wd), lambda i: (i, 0)) for _, wd in segs],
        out_shape=[jax.ShapeDtypeStruct((t, wd), dt) for (_, wd), dt in zip(segs, dtypes)],
        compiler_params=_cparams("parallel"),
        name="norm_proj",
    )(x, norm_w.reshape(1, k), w)


def _win_attn_kernel(sink_ref, *refs, nq):
    bias_refs = refs[:nq]
    q_ref, kp_ref, kc_ref, kn_ref, vp_ref, vc_ref, vn_ref, o_ref = refs[nq:]
    kall = jnp.concatenate([kp_ref[...], kc_ref[...], kn_ref[...]], axis=0)
    vall = jnp.concatenate([vp_ref[...], vc_ref[...], vn_ref[...]], axis=0)
    sink2 = [sink_ref[h] * LOG2E for h in range(A_HEADS)]
    items = [(j, h) for j in range(nq) for h in range(A_HEADS)]
    qrows = lambda j: slice(j * BLOCK, (j + 1) * BLOCK)
    krows = lambda j: slice(j * BLOCK, (j + 3) * BLOCK)
    kcol = lambda h: slice((h // A_GROUP) * LANE, (h // A_GROUP + 1) * LANE)
    vcol = lambda h: slice((2 * (h // A_GROUP) + h % 2) * LANE, (2 * (h // A_GROUP) + h % 2 + 1) * LANE)
    s = [_dot_nt(q_ref[qrows(j), h * LANE:(h + 1) * LANE], kall[krows(j), kcol(h)]) for j, h in items]
    s = [sv + bias_refs[j][h] for (j, h), sv in zip(items, s)]
    m = [jnp.maximum(jnp.max(sv, axis=-1, keepdims=True), sink2[h]) for (_, h), sv in zip(items, s)]
    p = _each(lambda sv, mv: jnp.exp2(sv - mv), s, m)
    rden = [1.0 / (jnp.sum(pv_, axis=-1, keepdims=True) + jnp.exp2(sink2[h] - mv))
            for (_, h), pv_, mv in zip(items, p, m)]
    pv = {it: _dot(pp.astype(BF16), vall[krows(it[0]), vcol(it[1])]) * rd for it, pp, rd in zip(items, p, rden)}
    for j in range(nq):
        for pair in range(A_HEADS // 2):
            o_ref[qrows(j), pair * LANE:(pair + 1) * LANE] = (pv[(j, 2 * pair)] + pv[(j, 2 * pair + 1)]).astype(o_ref.dtype)


def _window_bias():
    qi = np.arange(BLOCK)[:, None]
    kj = np.arange(3 * BLOCK)[None, :]
    dist = np.abs(qi + BLOCK - kj)
    slopes = np.array([2.0 ** (-8.0 * (h + 1) / A_HEADS) for h in range(A_HEADS)], np.float32)
    base = -slopes[:, None, None] * dist[None].astype(np.float32)
    out = []
    for first in (0, 1):
        for last in (0, 1):
            valid = (dist <= WINDOW) & ((kj >= BLOCK) | (first == 0)) & ((kj < 2 * BLOCK) | (last == 0))
            out.append(np.where(valid[None], base, -np.inf))
    return jnp.asarray(np.stack(out) * LOG2E, F32)


def window_attention(q, k, v, sink, batch, seq):
    nb = seq // BLOCK
    nq = 2 if nb % 2 == 0 else 1
    ns = nb // nq
    cur = lambda b, n: (b * ns + n, 0)
    prev = lambda b, n: (b * nb + jnp.maximum(n * nq - 1, 0), 0)
    nxt = lambda b, n: (b * nb + jnp.minimum((n + 1) * nq, nb - 1), 0)

    def variant(j):
        first = lambda n: (n == 0).astype(jnp.int32) if j == 0 else 0
        last = lambda n: (n == ns - 1).astype(jnp.int32) if j == nq - 1 else 0
        return lambda b, n: (2 * first(n) + last(n), 0, 0, 0)

    kw, vw = k.shape[1], v.shape[1]
    bias = _window_bias()
    return pl.pallas_call(
        functools.partial(_win_attn_kernel, nq=nq),
        grid=(batch, ns),
        in_specs=([pl.BlockSpec(memory_space=pltpu.SMEM)]
                  + [pl.BlockSpec((None, A_HEADS, BLOCK, 3 * BLOCK), variant(j)) for j in range(nq)]
                  + [pl.BlockSpec((nq * BLOCK, q.shape[1]), cur),
                     pl.BlockSpec((BLOCK, kw), prev), pl.BlockSpec((nq * BLOCK, kw), cur), pl.BlockSpec((BLOCK, kw), nxt),
                     pl.BlockSpec((BLOCK, vw), prev), pl.BlockSpec((nq * BLOCK, vw), cur),
                     pl.BlockSpec((BLOCK, vw), nxt)]),
        out_specs=pl.BlockSpec((nq * BLOCK, A_HEADS * HEAD_DIM), cur),
        out_shape=jax.ShapeDtypeStruct((batch * seq, A_HEADS * HEAD_DIM), BF16),
        compiler_params=_cparams("parallel", "parallel"),
        name="window_attention",
    )(sink, *([bias] * nq), q, k, k, k, v, v, v)


def _attn_in_kernel(x_ref, nw_ref, w_ref, qnw_ref, wq_ref, kvnw_ref, wk_ref, wv_ref, qtab_ref, ctab_ref, stab_ref,
                    qa_ref, ka_ref, va_ref, q_ref, k_ref, v_ref, *, segs):
    xn = _rms(x_ref[...], nw_ref[...]).astype(BF16)
    proj = [_dot(xn, w_ref[:, start:start + width]) for start, width in segs]
    qa_ref[...] = (proj[0] * WINDOW_Q_SCALE).astype(qa_ref.dtype)
    ka_ref[...] = proj[1].astype(ka_ref.dtype)
    va_ref[...] = proj[2].astype(va_ref.dtype)
    cq, ckv, kab = proj[3:]
    xq = _rms(cq, qnw_ref[...]).astype(BF16)
    tab = qtab_ref[...]
    for h in range(B_HEADS):
        cols = slice(h * LANE, (h + 1) * LANE)
        q_ref[:, cols] = (_dot(xq, wq_ref[:, cols]) * tab).astype(q_ref.dtype)
    xkv = _rms(ckv, kvnw_ref[...]).astype(BF16)
    kr = kab[:, :LANE] * ctab_ref[...] + kab[:, LANE:] * stab_ref[...]
    for h in range(B_HEADS):
        cols = slice(h * LANE, (h + 1) * LANE)
        k_ref[:, cols] = (_dot(xkv, wk_ref[:, cols]) + kr).astype(k_ref.dtype)
    lane = lax.broadcasted_iota(jnp.int32, v_ref.shape, 1) % LANE
    v_ref[...] = jnp.where(lane == B_VDIM, 1.0, _dot(xkv, wv_ref[...])).astype(v_ref.dtype)


def attn_in_proj(x, norm_w, w, segs, q_norm_w, wq, kv_norm_w, wk, wv, q_tab, k_cos, k_sin, seq, tm):
    t, k = x.shape
    nt = seq // tm
    row = lambda i: (i, 0)
    pos = lambda i: (i % nt, 0)
    widths = [wd for _, wd in segs[:3]] + [wq.shape[1], wk.shape[1], wv.shape[1]]
    return pl.pallas_call(
        functools.partial(_attn_in_kernel, segs=tuple(segs)),
        grid=(t // tm,),
        in_specs=[pl.BlockSpec((tm, k), row), _resident((1, k)), _resident(w.shape),
                  _resident((1, wq.shape[0])), _resident(wq.shape),
                  _resident((1, wk.shape[0])), _resident(wk.shape), _resident(wv.shape),
                  pl.BlockSpec((tm, LANE), pos), pl.BlockSpec((tm, LANE), pos), pl.BlockSpec((tm, LANE), pos)],
        out_specs=[pl.BlockSpec((tm, wd), row) for wd in widths],
        out_shape=[jax.ShapeDtypeStruct((t, wd), BF16) for wd in widths],
        compiler_params=_cparams("parallel"),
        name="attn_in_proj",
    )(x, norm_w.reshape(1, k), w, q_norm_w.reshape(1, -1), wq, kv_norm_w.reshape(1, -1), wk, wv,
      q_tab, k_cos, k_sin)


def _mla_flash_kernel(q_ref, k_ref, v_ref, o_ref, s_ref, p_ref, m_ref, a_ref, acc_ref, *, tk, nk, rb):
    tq = q_ref.shape[0]
    heads = range(2)
    m_ref[...] = jnp.full(m_ref.shape, -jnp.inf, F32)
    acc_ref[...] = jnp.zeros(acc_ref.shape, F32)

    def body(j, carry):
        rows = pl.ds(pl.multiple_of(j * tk, tk), tk)
        for e in heads:
            cols = slice(e * LANE, (e + 1) * LANE)
            s_ref[e] = _dot_nt(q_ref[:, cols], k_ref[rows, cols])
        for e in heads:
            cols = slice(e * LANE, (e + 1) * LANE)
            for r in range(tq // rb):
                rs = slice(r * rb, (r + 1) * rb)
                s = s_ref[e, rs, :]
                m_old = m_ref[e, rs, :]
                m_new = jnp.maximum(m_old, jnp.max(s, axis=-1, keepdims=True))
                m_ref[e, rs, :] = m_new
                a_ref[e, rs, :] = jnp.exp2(m_old - m_new)
                p_ref[e, rs, :] = jnp.exp2(s - jnp.concatenate([m_new] * (tk // LANE), axis=1)).astype(BF16)
            acc_ref[e] = a_ref[e] * acc_ref[e] + _dot(p_ref[e], v_ref[rows, cols])
        return carry

    lax.fori_loop(0, nk, body, 0)
    outs = []
    for e in heads:
        acc = acc_ref[e]
        outs.append(acc / acc[:, B_VDIM:B_VDIM + 1])
    lane = lax.broadcasted_iota(jnp.int32, (tq, LANE), 1)
    o_ref[...] = jnp.where(lane < B_VDIM, outs[0], pltpu.roll(outs[1], B_VDIM, axis=1)).astype(o_ref.dtype)


def mla_flash(q, k, v, batch, seq, tq, tk, rb):
    nq = seq // tq
    npair = B_HEADS // 2
    return pl.pallas_call(
        functools.partial(_mla_flash_kernel, tk=tk, nk=seq // tk, rb=rb),
        grid=(batch, npair, nq),
        in_specs=[pl.BlockSpec((tq, 2 * LANE), lambda b, p, i: (b * nq + i, p)),
                  pl.BlockSpec((seq, 2 * LANE), lambda b, p, i: (b, p)),
                  pl.BlockSpec((seq, 2 * LANE), lambda b, p, i: (b, p))],
        out_specs=pl.BlockSpec((tq, LANE), lambda b, p, i: (b * nq + i, p)),
        out_shape=jax.ShapeDtypeStruct((batch * seq, B_HEADS * B_VDIM), BF16),
        scratch_shapes=[pltpu.VMEM((2, tq, tk), F32), pltpu.VMEM((2, tq, tk), BF16), pltpu.VMEM((2, tq, LANE), F32),
                        pltpu.VMEM((2, tq, LANE), F32), pltpu.VMEM((2, tq, LANE), F32)],
        compiler_params=_cparams("parallel", "parallel", "parallel"),
        name="mla_flash",
    )(q, k, v)


def _mlp_tail(x, nw_ref, w1_ref, w2_ref, fw_ref, o_ref, fchunk, final_norm):
    xn = _rms(x, nw_ref[...]).astype(BF16)
    acc = x
    for f in range(w1_ref.shape[1] // fchunk):
        cols = slice(f * fchunk, (f + 1) * fchunk)
        h = jnp.square(jnp.maximum(_dot(xn, w1_ref[:, cols]), 0.0)).astype(BF16)
        acc = acc + _dot(h, w2_ref[cols, :])
    if final_norm:
        acc = _rms(acc, fw_ref[...])
    o_ref[...] = acc


def _attn_out_mlp_kernel(x_ref, a_ref, b_ref, wo_ref, nw_ref, w1_ref, w2_ref, fw_ref, o_ref, *, fchunk, final_norm):
    ka = a_ref.shape[1]
    o_ref[...] = x_ref[...] + _dot(a_ref[...], wo_ref[:ka, :]) + _dot(b_ref[...], wo_ref[ka:, :])
    _mlp_tail(o_ref[...], nw_ref, w1_ref, w2_ref, fw_ref, o_ref, fchunk, final_norm)


def _lin_out_mlp_kernel(x_ref, cf_ref, cb_ref, df_ref, db_ref, gc_ref, gd_ref, cw_ref, dw_ref, wo_ref,
                        nw_ref, w1_ref, w2_ref, fw_ref, o_ref, *, fchunk, final_norm):
    def gated(f_ref, b_ref, gate_ref, gnw_ref):
        o = f_ref[...] + b_ref[...]
        parts = []
        for h in range(LIN_HEADS):
            cols = slice(h * LIN_DV, (h + 1) * LIN_DV)
            gate = gate_ref[:, cols].astype(F32)
            parts.append((_rms(o[:, cols], gnw_ref[...]) * (gate * _sigmoid(gate))).astype(BF16))
        return jnp.concatenate(parts, axis=1)

    kc = cf_ref.shape[1]
    o_ref[...] = (x_ref[...] + _dot(gated(cf_ref, cb_ref, gc_ref, cw_ref), wo_ref[:kc, :])
                  + _dot(gated(df_ref, db_ref, gd_ref, dw_ref), wo_ref[kc:, :]))
    _mlp_tail(o_ref[...], nw_ref, w1_ref, w2_ref, fw_ref, o_ref, fchunk, final_norm)


def mixer_out_mlp(x, mixer_inputs, small_params, w_out, norm_w, w1, w2, final_w, final_norm, tm, fchunk):
    t, d = x.shape
    row = lambda i: (i, 0)
    body = _attn_out_mlp_kernel if len(mixer_inputs) == 2 else _lin_out_mlp_kernel
    return pl.pallas_call(
        functools.partial(body, fchunk=fchunk, final_norm=final_norm),
        grid=(t // tm,),
        in_specs=([pl.BlockSpec((tm, d), row)] + [pl.BlockSpec((tm, a.shape[1]), row) for a in mixer_inputs]
                  + [_resident(p.shape) for p in small_params]
                  + [_resident(w_out.shape), _resident((1, d)), _resident(w1.shape), _resident(w2.shape),
                     _resident((1, d))]),
        out_specs=pl.BlockSpec((tm, d), row),
        out_shape=jax.ShapeDtypeStruct((t, d), F32),
        compiler_params=_cparams("parallel"),
        name="mixer_out_mlp",
    )(x, *mixer_inputs, *small_params, w_out, norm_w.reshape(1, d), w1, w2, final_w.reshape(1, d))


def _direction_masks(d):
    sgn = 1 - 2 * d
    r = lax.broadcasted_iota(jnp.int32, (STACK, STACK), 0)
    c = lax.broadcasted_iota(jnp.int32, (STACK, STACK), 1)
    same = (r & -CHUNK) == (c & -CHUNK)
    order = ((r & (CHUNK - 1)) - (c & (CHUNK - 1))) * sgn
    i = lax.broadcasted_iota(jnp.int32, (CHUNK, CHUNK), 0)
    j = lax.broadcasted_iota(jnp.int32, (CHUNK, CHUNK), 1)
    cum = (((i - j) * sgn) >= 0).astype(F32)
    return same, same & (order >= 0), same & (order > 0), cum


def _block_diag(x, same):
    return jnp.where(same, jnp.concatenate([x] * LIN_HEADS, axis=0), 0.0)


def _block_diag_cols(x, same):
    return jnp.where(same, jnp.concatenate([x] * LIN_HEADS, axis=1), 0.0)


def _stack_heads(v, width):
    return jnp.concatenate([v[:, h * width:(h + 1) * width] for h in range(LIN_HEADS)], axis=0)


def _each(f, *lists):
    return [f(*args) for args in zip(*lists)]


def _scan_order(ncb):
    return [((0, i), (1, ncb - 1 - i)) for i in range(ncb)]


def _gla_kernel(qf_ref, kf_ref, vf_ref, sf_ref, qb_ref, kb_ref, vb_ref, sb_ref, wg_ref, bg_ref,
                of_ref, ob_ref, st_ref, *, ncb, nb):
    @pl.when(pl.program_id(1) == 0)
    def _():
        st_ref[...] = jnp.zeros_like(st_ref)

    refs = ((qf_ref, kf_ref, vf_ref, sf_ref, of_ref), (qb_ref, kb_ref, vb_ref, sb_ref, ob_ref))
    masks = [_direction_masks(d) for d in range(2)]
    items = [(bi, d, ci) for bi in range(nb) for d in range(2) for ci in range(ncb)]
    rows = lambda ci: slice(ci * CHUNK, (ci + 1) * CHUNK)
    chunk = lambda which, it: refs[it[1]][which][it[0], rows(it[2]), :]
    same = [masks[d][0] for _, d, _ in items]
    z = [_dot(chunk(3, it).astype(BF16), wg_ref[it[1]]) + bg_ref[it[1]] for it in items]
    log_a = _each(lambda a: (jnp.minimum(a, 0.0) - jnp.log1p(jnp.exp(-jnp.abs(a)))) * (1.0 / GATE_NORM), z)
    b = [_dot_mask_f32(masks[it[1]][3].astype(BF16), a) for it, a in zip(items, log_a)]
    b_last = _each(lambda a: jnp.sum(a, axis=0, keepdims=True), log_a)
    q_in = [_block_diag(chunk(0, it) * (LIN_DK ** -0.5) * jnp.exp(bb), sm).astype(BF16)
            for it, bb, sm in zip(items, b, same)]
    k = [chunk(1, it) for it in items]
    k_in = _each(lambda a, bb, sm: _block_diag(a * jnp.exp(-bb), sm).astype(BF16), k, b, same)
    k_out = _each(lambda a, bb, bl, sm: _block_diag(a * jnp.exp(bl - bb), sm).astype(BF16), k, b, b_last, same)
    v = [_stack_heads(chunk(2, it), LIN_DV).astype(BF16) for it in items]
    sc = [jnp.where(masks[it[1]][1], _dot_nt(a, bb), 0.0).astype(BF16) for it, a, bb in zip(items, q_in, k_in)]
    o_intra = dict(zip(items, _each(_dot, sc, v)))
    ds = dict(zip(items, _each(_dot_tn, v, k_out)))
    dec = dict(zip(items, _each(jnp.exp, b_last)))
    q_in = dict(zip(items, q_in))

    streams = [(bi, d) for bi in range(nb) for d in range(2)]
    st = {sd: st_ref[i] for i, sd in enumerate(streams)}
    for fwd_bwd in _scan_order(ncb):
        step = [(bi, d, ci) for bi in range(nb) for d, ci in fwd_bwd]
        o = [o_intra[it] + _dot_nt(q_in[it], st[it[:2]].astype(BF16)) for it in step]
        for it, od in zip(step, o):
            bi, d, ci = it
            st[(bi, d)] = dec[it] * st[(bi, d)] + ds[it]
            for h in range(LIN_HEADS):
                refs[d][4][bi, rows(ci), h * LIN_DV:(h + 1) * LIN_DV] = od[h * CHUNK:(h + 1) * CHUNK, :]
    for i, sd in enumerate(streams):
        st_ref[i] = st[sd]


def _bidir_call(body, name, arrays, extra, extra_specs, batch, seq, tc, nb, hk, hv):
    nblk = seq // tc
    ins = [a.reshape(batch, seq, a.shape[1]) for a in arrays]
    fwd = lambda bb, n: (bb, n, 0)
    bwd = lambda bb, n: (bb, nblk - 1 - n, 0)
    spec = lambda a, idx: pl.BlockSpec((nb, tc, a.shape[2]), idx)
    o_f, o_b = pl.pallas_call(
        functools.partial(body, ncb=tc // CHUNK, nb=nb),
        grid=(batch // nb, nblk),
        in_specs=[spec(a, fwd) for a in ins] + [spec(a, bwd) for a in ins] + extra_specs,
        out_specs=[pl.BlockSpec((nb, tc, hv), fwd), pl.BlockSpec((nb, tc, hv), bwd)],
        out_shape=[jax.ShapeDtypeStruct((batch, seq, hv), F32)] * 2,
        scratch_shapes=[pltpu.VMEM((2 * nb, LIN_DV, hk), F32)],
        compiler_params=_cparams("parallel", "arbitrary"),
        name=name,
    )(*ins, *ins, *extra)
    return o_f.reshape(batch * seq, hv), o_b.reshape(batch * seq, hv)


def gla_scan(q, k, v, small, wg, bg, batch, seq, tc, nb):
    return _bidir_call(_gla_kernel, "gla_scan", (q, k, v, small), (wg, bg),
                       [_resident(wg.shape), _resident(bg.shape)], batch, seq, tc, nb, q.shape[1], v.shape[1])


def _gdn_prep_kernel(xp_ref, xc_ref, xn_ref, sm_ref, cw_ref, gsum_ref, alog_ref, dtb_ref,
                     q_ref, k_ref, v_ref, g_ref, b_ref, bt_ref, buf_ref, *, nt):
    i = pl.program_id(1)
    tm = xc_ref.shape[0]
    halo = xp_ref.shape[0]
    buf_ref[:halo, :] = xp_ref[...].astype(F32) * (i > 0).astype(F32)
    buf_ref[halo:halo + tm, :] = xc_ref[...].astype(F32)
    buf_ref[halo + tm:, :] = xn_ref[...].astype(F32) * (i < nt - 1).astype(F32)
    acc = jnp.zeros(xc_ref.shape, F32)
    for j in range(CONV_TAPS):
        off = halo + j - CONV_TAPS // 2
        acc = acc + buf_ref[off:off + tm, :] * cw_ref[j:j + 1, :]
    y = acc * _sigmoid(acc)
    nqk = LIN_HEADS * LIN_DK
    for idx, (o_ref, scale) in enumerate(((q_ref, LIN_DK ** -0.5), (k_ref, 1.0))):
        part = y[:, idx * nqk:(idx + 1) * nqk]
        ssq = _dot_f32_mask(part * part, gsum_ref[...])
        o_ref[...] = (part * lax.rsqrt(ssq + EPS) * scale).astype(o_ref.dtype)
    v_ref[...] = y[:, 2 * nqk:].astype(v_ref.dtype)
    sm = sm_ref[...]
    beta = _sigmoid(sm)
    g = -jnp.exp(alog_ref[...]) * _softplus(sm + dtb_ref[...])
    lane = lax.broadcasted_iota(jnp.int32, sm.shape, 1) - GATE_LANE0
    is_beta = (lane >= 0) & (lane < 2 * LIN_HEADS)
    is_g = (lane >= 2 * LIN_HEADS) & (lane < 4 * LIN_HEADS)
    g = jnp.where(is_g, g, 0.0)
    g_ref[...] = jnp.where(is_beta, beta, g)
    ri = lax.broadcasted_iota(jnp.int32, (CHUNK, CHUNK), 0)
    ci = lax.broadcasted_iota(jnp.int32, (CHUNK, CHUNK), 1)
    tril16 = (ri >= ci).astype(BF16)
    ones16 = jnp.ones((CHUNK, CHUNK), BF16)
    is_bwd = lax.broadcasted_iota(jnp.int32, (CHUNK, LANE), 1) >= GATE_LANE0 + 3 * LIN_HEADS
    for c in range(tm // CHUNK):
        rows = slice(c * CHUNK, (c + 1) * CHUNK)
        gc = g[rows, :]
        prefix = _dot_mask_f32(tril16, gc)
        total = _dot_mask_f32(ones16, gc)
        b_ref[rows, :] = jnp.where(is_bwd, total - prefix + gc, prefix)
        bt_ref[rows, :] = total


def gdn_prep(dqkv, small, conv_w, gsum, alog, dtb, batch, seq, tm):
    nt = seq // tm
    t, c = dqkv.shape
    halo = 2 * SUBLANE
    hb = tm // halo
    nqk = LIN_HEADS * LIN_DK
    row = lambda b, i: (b * nt + i, 0)
    prev = lambda b, i: (jnp.maximum((b * nt + i) * hb - 1, 0), 0)
    nxt = lambda b, i: (jnp.minimum((b * nt + i + 1) * hb, t // halo - 1), 0)
    return pl.pallas_call(
        functools.partial(_gdn_prep_kernel, nt=nt),
        grid=(batch, nt),
        in_specs=[pl.BlockSpec((halo, c), prev), pl.BlockSpec((tm, c), row), pl.BlockSpec((halo, c), nxt),
                  pl.BlockSpec((tm, LANE), row),
                  _resident(conv_w.shape), _resident(gsum.shape), _resident((1, LANE)), _resident((1, LANE))],
        out_specs=[pl.BlockSpec((tm, nqk), row), pl.BlockSpec((tm, nqk), row),
                   pl.BlockSpec((tm, c - 2 * nqk), row)] + [pl.BlockSpec((tm, LANE), row)] * 3,
        out_shape=[jax.ShapeDtypeStruct((t, nqk), BF16), jax.ShapeDtypeStruct((t, nqk), BF16),
                   jax.ShapeDtypeStruct((t, c - 2 * nqk), BF16)] + [jax.ShapeDtypeStruct((t, LANE), F32)] * 3,
        scratch_shapes=[pltpu.VMEM((tm + 2 * halo, c), F32)],
        compiler_params=_cparams("parallel", "parallel"),
        name="gdn_prep",
    )(dqkv, dqkv, dqkv, small, conv_w, gsum, alog, dtb)


def _gdn_kernel(qf_ref, kf_ref, vf_ref, gf_ref, bf_ref, tf_ref, qb_ref, kb_ref, vb_ref, gb_ref, bb_ref, tb_ref,
                of_ref, ob_ref, st_ref, *, ncb, nb):
    @pl.when(pl.program_id(1) == 0)
    def _():
        st_ref[...] = jnp.zeros_like(st_ref)

    refs = ((qf_ref, kf_ref, vf_ref, gf_ref, bf_ref, tf_ref, of_ref),
            (qb_ref, kb_ref, vb_ref, gb_ref, bb_ref, tb_ref, ob_ref))
    masks = [_direction_masks(d) for d in range(2)]
    r = lax.broadcasted_iota(jnp.int32, (STACK, STACK), 0)
    c = lax.broadcasted_iota(jnp.int32, (STACK, STACK), 1)
    diag = r == c
    eye = diag.astype(F32)
    lane_blk = lax.broadcasted_iota(jnp.int32, (SUBLANE, STACK), 1) // CHUNK

    def rows_of(x, lane0):
        return jnp.concatenate(
            [jnp.broadcast_to(x[:, lane0 + h:lane0 + h + 1], (CHUNK, STACK)) for h in range(LIN_HEADS)], axis=0)

    def head_lanes(row, lane0):
        out = jnp.broadcast_to(row[:, lane0:lane0 + 1], (SUBLANE, STACK))
        for h in range(1, LIN_HEADS):
            out = jnp.where(lane_blk == h, jnp.broadcast_to(row[:, lane0 + h:lane0 + h + 1], (SUBLANE, STACK)), out)
        return out

    each = _each
    items = [(bi, d, ci) for bi in range(nb) for d in range(2) for ci in range(ncb)]
    rows = lambda ci: slice(ci * CHUNK, (ci + 1) * CHUNK)
    chunk = lambda which, it: refs[it[1]][which][it[0], rows(it[2]), :]
    same = [masks[d][0] for _, d, _ in items]
    beta_col = [rows_of(chunk(3, it), _gate_lanes(it[1])[0]) for it in items]
    b_col = [rows_of(chunk(4, it), _gate_lanes(it[1])[1]) for it in items]
    bl_col = [rows_of(chunk(5, it), _gate_lanes(it[1])[1]) for it in items]
    k_bd = [_block_diag(chunk(1, it), sm) for it, sm in zip(items, same)]
    q_bd = [_block_diag(chunk(0, it), sm) for it, sm in zip(items, same)]
    kb_bd = each(lambda a, b: a * b, k_bd, beta_col)
    k16 = each(lambda a: a.astype(BF16), k_bd)
    kk = each(lambda a, b: _dot_nt(a.astype(BF16), b), kb_bd, k16)
    qk = each(lambda a, b: _dot_nt(a.astype(BF16), b), q_bd, k16)
    b_row = each(lambda bc: jnp.sum(jnp.where(diag, bc, 0.0), axis=0, keepdims=True), b_col)
    gamma = [jnp.exp(jnp.where(masks[it[1]][1], bc - br, -jnp.inf)) for it, bc, br in zip(items, b_col, b_row)]
    lower = [jnp.where(masks[it[1]][2], a * g, 0.0) for it, a, g in zip(items, kk, gamma)]
    attn = each(lambda a, g: (a * g).astype(BF16), qk, gamma)
    l_hi = each(lambda a: a.astype(BF16), lower)
    pw = [l_hi]
    for _ in range(5):
        pw.append(each(lambda a: _dot(a, a).astype(BF16), pw[-1]))

    def pair(a, b16):
        return a + _dot(a.astype(BF16), b16)

    f01 = each(lambda lo, p1: pair(eye - lo, p1), lower, pw[1])
    f23 = each(lambda p2, p3: pair(eye + p2.astype(F32), p3), pw[2], pw[3])
    f45 = each(lambda p4, p5: pair(eye + p4.astype(F32), p5), pw[4], pw[5])
    f0123 = each(lambda a, b: _dot(a.astype(BF16), b.astype(BF16)).astype(BF16), f01, f23)
    inv16 = each(lambda a, b: _dot(a, b.astype(BF16)).astype(BF16), f0123, f45)
    e_b = each(jnp.exp, b_col)
    vb = [_stack_heads(chunk(2, it), LIN_DV) * bc[:, :LIN_DV] for it, bc in zip(items, beta_col)]
    kbe = [_stack_heads(chunk(1, it), LIN_DK) * (bc * e)[:, :LIN_DK] for it, bc, e in zip(items, beta_col, e_b)]
    rhs = each(lambda v, kb: jnp.concatenate([v, kb], axis=1), vb, kbe)
    x16 = each(lambda m, r_: _dot(m, r_.astype(BF16)).astype(BF16), inv16, rhs)
    l_lo = each(lambda lo, hi: (lo - hi.astype(F32)).astype(BF16), lower, l_hi)
    resid = each(lambda r_, x, hi, lo: r_ - (x.astype(F32) + _dot(hi, x) + _dot(lo, x)), rhs, x16, l_hi, l_lo)
    sol = each(lambda x, m, rs: x.astype(F32) + _dot(m, rs.astype(BF16)), x16, inv16, resid)
    u = {it: sol[i][:, :LIN_DV] for i, it in enumerate(items)}
    w = {it: _block_diag_cols(sol[i][:, LIN_DV:].astype(BF16), same[i])
         for i, it in enumerate(items)}
    attn = dict(zip(items, attn))
    q_dec = {it: (q_bd[i] * e_b[i]).astype(BF16) for i, it in enumerate(items)}
    k_dec = {it: (k_bd[i] * jnp.exp(bl_col[i] - b_col[i])).astype(BF16) for i, it in enumerate(items)}
    dec = {(bi, d, ci): jnp.exp(head_lanes(refs[d][5][bi, ci * CHUNK:ci * CHUNK + SUBLANE, :],
                                           _gate_lanes(d)[1]))[:1, :] for bi, d, ci in items}

    streams = [(bi, d) for bi in range(nb) for d in range(2)]
    st = {sd: st_ref[i] for i, sd in enumerate(streams)}
    for fwd_bwd in _scan_order(ncb):
        step = [(bi, d, ci) for bi in range(nb) for d, ci in fwd_bwd]
        st16 = [st[it[:2]].astype(BF16) for it in step]
        v_new16 = [(u[it] - _dot_nt(w[it], s16)).astype(BF16) for it, s16 in zip(step, st16)]
        o = [_dot_nt(q_dec[it], s16) + _dot(attn[it], vn) for it, s16, vn in zip(step, st16, v_new16)]
        for it, od, vn in zip(step, o, v_new16):
            bi, d, ci = it
            st[(bi, d)] = dec[it] * st[(bi, d)] + _dot_tn(vn, k_dec[it])
            for h in range(LIN_HEADS):
                refs[d][6][bi, rows(ci), h * LIN_DV:(h + 1) * LIN_DV] = od[h * CHUNK:(h + 1) * CHUNK, :]
    for i, sd in enumerate(streams):
        st_ref[i] = st[sd]


def _gate_lanes(d):
    return GATE_LANE0 + d * LIN_HEADS, GATE_LANE0 + (2 + d) * LIN_HEADS


def gdn_scan(q, k, v, mixed, bcum, btot, batch, seq, tc, nb):
    return _bidir_call(_gdn_kernel, "gdn_scan", (q, k, v, mixed, bcum, btot), (), [],
                       batch, seq, tc, nb, q.shape[1], v.shape[1])


def _rot_half_cols(w):
    half = w.shape[-1] // 2
    return jnp.concatenate([-w[..., half:], w[..., :half]], axis=-1)


def _pad_heads(w, heads, dim, offset=0):
    k = w.shape[0]
    out = jnp.zeros((k, heads, LANE), w.dtype)
    out = out.at[:, :, offset:offset + dim].set(w.reshape(k, heads, dim))
    return out.reshape(k, heads * LANE)


def _layer0_in_weight(w_in):
    aq = A_HEADS * HEAD_DIM
    akv = A_KV_HEADS * HEAD_DIM
    o = 0
    w_q = w_in[:, o:o + aq]; o += aq
    w_k = w_in[:, o:o + akv]; o += akv
    w_v = w_in[:, o:o + akv]; o += akv
    w_cq = w_in[:, o:o + B_Q_RANK]; o += B_Q_RANK
    w_ckv = w_in[:, o:o + B_KV_RANK]; o += B_KV_RANK
    w_pe = w_in[:, o:o + B_ROPE]
    zeros = jnp.zeros((w_in.shape[0], B_NOPE), w_in.dtype)
    v_lo = _pad_heads(w_v, A_KV_HEADS, HEAD_DIM, 0).reshape(-1, A_KV_HEADS, LANE)
    v_hi = _pad_heads(w_v, A_KV_HEADS, HEAD_DIM, HEAD_DIM).reshape(-1, A_KV_HEADS, LANE)
    w_v2 = jnp.stack([v_lo, v_hi], axis=2).reshape(w_in.shape[0], 2 * A_KV_HEADS * LANE)
    w_pe_rot = _rot_half_cols(w_pe)
    pieces = [_pad_heads(w_q, A_HEADS, HEAD_DIM), _pad_heads(w_k, A_KV_HEADS, HEAD_DIM), w_v2,
              w_cq, w_ckv,
              jnp.concatenate([zeros, w_pe, w_pe, zeros, w_pe_rot, w_pe_rot], axis=1)]
    segs, start = [], 0
    for p in pieces:
        segs.append((start, p.shape[1]))
        start += p.shape[1]
    return jnp.concatenate(pieces, axis=1).astype(BF16), segs


def _mla_weights(w_uq, w_ukv):
    r = w_uq.shape[0]
    wq = w_uq.reshape(r, B_HEADS, B_NOPE + B_ROPE)
    pe = wq[:, :, B_NOPE:]
    wq = jnp.concatenate([wq[:, :, :B_NOPE], pe, _rot_half_cols(pe)], axis=-1).reshape(r, B_HEADS * LANE)
    rk = w_ukv.shape[0]
    wkv = w_ukv.reshape(rk, B_HEADS, B_NOPE + B_VDIM)
    wk = _pad_heads(wkv[:, :, :B_NOPE].reshape(rk, B_HEADS * B_NOPE), B_HEADS, B_NOPE)
    wv = _pad_heads(wkv[:, :, B_NOPE:].reshape(rk, B_HEADS * B_VDIM), B_HEADS, B_VDIM)
    return wq.astype(BF16), wk.astype(BF16), wv.astype(BF16)


def _rope_tables(seq):
    half = B_ROPE // 2
    inv = ROPE_THETA ** (-jnp.arange(half, dtype=F32) / half)
    ang = jnp.arange(seq, dtype=F32)[:, None] * inv[None, :]
    cos2 = jnp.tile(jnp.cos(ang), (1, 4))
    sin2 = jnp.tile(jnp.sin(ang), (1, 4))
    zeros = jnp.zeros((seq, B_NOPE), F32)
    qscale = (B_NOPE + B_ROPE) ** -0.5 * LOG2E
    q_tab = jnp.concatenate([jnp.ones((seq, B_NOPE), F32), cos2[:, :B_ROPE], sin2[:, :B_ROPE]], axis=1) * qscale
    k_cos = jnp.concatenate([zeros, cos2], axis=1)
    k_sin = jnp.concatenate([zeros, sin2], axis=1)
    return q_tab, k_cos, k_sin


def _layer1_in_weight(w_in):
    hk = LIN_HEADS * LIN_DK
    hv = LIN_HEADS * LIN_DV
    conv_ch = 2 * hk + hv
    o = 2 * hk + 2 * hv
    w_main = w_in[:, :o]
    w_gl = w_in[:, o:o + 2 * GATE_RANK]; o += 2 * GATE_RANK
    w_conv = w_in[:, o:o + conv_ch]; o += conv_ch
    w_z = w_in[:, o:o + hv]; o += hv
    w_small = w_in[:, o:o + 4 * LIN_HEADS]
    pad = jnp.zeros((w_in.shape[0], LANE - 2 * GATE_RANK - 4 * LIN_HEADS), w_in.dtype)
    w = jnp.concatenate([w_main, w_conv, w_z, w_gl, w_small, pad], axis=1).astype(BF16)
    widths = [hk, hk, hv, hv, conv_ch, hv, LANE]
    segs, start = [], 0
    for wd in widths:
        segs.append((start, wd))
        start += wd
    return w, segs


def _pick_tile(n, pref):
    t = min(n, pref)
    while n % t:
        t //= 2
    return t


def kernel(x, att_norm, att_w_in, att_sink, mla_q_norm, mla_w_uq, mla_kv_norm, mla_w_ukv, att_w_out, lin_norm, lin_w_in, gla_w_gate_f, gla_b_gate_f, gla_w_gate_b, gla_b_gate_b, gla_norm, gdn_conv, gdn_a_log_f, gdn_dt_bias_f, gdn_a_log_b, gdn_dt_bias_b, gdn_norm, lin_w_out, mlp_norm, mlp_w1, mlp_w2, final_norm):
    batch, seq, dm = x.shape
    t = batch * seq
    depth = mlp_norm.shape[0]
    tm = _pick_tile(seq, ROW_TILE)
    xs = x.reshape(t, dm)
    for layer in range(depth):
        i = layer // 2
        if layer % 2 == 0:
            w0, segs = _layer0_in_weight(att_w_in[i])
            wq, wk, wv = _mla_weights(mla_w_uq[i], mla_w_ukv[i])
            q_tab, k_cos, k_sin = _rope_tables(seq)
            qa, ka, va, qb, kb, vb = attn_in_proj(xs, att_norm[i], w0, segs, mla_q_norm[i], wq, mla_kv_norm[i],
                                                  wk, wv, q_tab, k_cos, k_sin, seq, tm)
            o_a = window_attention(qa, ka, va, att_sink[i].astype(F32), batch, seq)
            o_b = mla_flash(qb, kb, vb, batch, seq, _pick_tile(seq, FLASH_Q_TILE), _pick_tile(seq, FLASH_KV_TILE),
                            FLASH_ROW_BLOCK)
            mixer_inputs, small_params, w_out = [o_a, o_b], [], att_w_out[i]
        else:
            w1, segs = _layer1_in_weight(lin_w_in[i])
            cq, ck, cv, cg, dqkv, dz, small = norm_proj(xs, lin_norm[i], w1, segs, [BF16] * 6 + [F32], tm)
            hk = LIN_HEADS * LIN_DK
            wg = jnp.zeros((2, LANE, hk), F32)
            wg = wg.at[0, :GATE_RANK].set(gla_w_gate_f[i]).at[1, GATE_RANK:2 * GATE_RANK].set(gla_w_gate_b[i])
            bg = jnp.stack([gla_b_gate_f[i], gla_b_gate_b[i]]).reshape(2, 1, hk).astype(F32)
            tc = _pick_tile(seq, SCAN_ROWS)
            nb = 2 if batch % 2 == 0 else 1
            o_cf, o_cb = gla_scan(cq, ck, cv, small, wg.astype(BF16), bg, batch, seq, tc, nb)
            gsum = (jnp.arange(hk)[:, None] // LIN_DK == jnp.arange(hk)[None, :] // LIN_DK).astype(BF16)
            nh = LIN_HEADS
            a0 = GATE_LANE0 + 2 * nh
            alog = jnp.zeros((LANE,), F32).at[a0:a0 + nh].set(gdn_a_log_f[i]).at[a0 + nh:a0 + 2 * nh].set(gdn_a_log_b[i])
            dtb = jnp.zeros((LANE,), F32).at[a0:a0 + nh].set(gdn_dt_bias_f[i]).at[a0 + nh:a0 + 2 * nh].set(gdn_dt_bias_b[i])
            dq, dk, dv, mixed, bcum, btot = gdn_prep(dqkv, small, gdn_conv[i].astype(F32), gsum,
                                                     alog.reshape(1, LANE), dtb.reshape(1, LANE), batch, seq, tm)
            o_df, o_db = gdn_scan(dq, dk, dv, mixed, bcum, btot, batch, seq, tc, nb)
            mixer_inputs = [o_cf, o_cb, o_df, o_db, cg, dz]
            small_params = [gla_norm[i].reshape(1, LIN_DV), gdn_norm[i].reshape(1, LIN_DV)]
            w_out = lin_w_out[i]
        last = layer == depth - 1
        xs = mixer_out_mlp(xs, mixer_inputs, small_params, w_out.astype(BF16), mlp_norm[layer],
                           mlp_w1[layer].astype(BF16), mlp_w2[layer].astype(BF16), final_norm, last, tm,
                           MLP_F_CHUNK)
    return xs.reshape(batch, seq, dm)
```

```python
import functools
import math

import numpy as np
import jax
import jax.numpy as jnp
from jax import lax
from jax.experimental import pallas as pl
from jax.experimental.pallas import tpu as pltpu

F32 = jnp.float32
BF16 = jnp.bfloat16
EPS = 1e-6

LANE = 128
SUBLANE = 8
VMEM_LIMIT_BYTES = 56 * 1024 * 1024

ROW_TILE = 512
ATTN_MLP_ROW_TILE = 1024
MLP_F_CHUNK = 1024
FLASH_Q_TILE = 1024
FLASH_KV_TILE = 2048
FLASH_ROW_BLOCK = 64
SCAN_ROWS = 256

HEAD_DIM = 64
A_HEADS = 8
A_KV_HEADS = 2
A_GROUP = A_HEADS // A_KV_HEADS
WINDOW = 128
BLOCK = 128
B_HEADS = 8
B_NOPE = 64
B_ROPE = 32
B_VDIM = 64
B_Q_RANK = 384
B_KV_RANK = 256
ROPE_THETA = 10000.0
LIN_HEADS = 4
LIN_DK = 64
LIN_DV = 128
GATE_RANK = 16
GATE_NORM = 16.0
CHUNK = 64
CONV_TAPS = 5
STACK = LIN_HEADS * CHUNK
GATE_LANE0 = 2 * GATE_RANK

LOG2E = math.log2(math.e)
WINDOW_Q_SCALE = HEAD_DIM ** -0.5 * LOG2E

NT_DIMS = (((1,), (1,)), ((), ()))
TN_DIMS = (((0,), (0,)), ((), ()))


def _cparams(*sem):
    return pltpu.CompilerParams(dimension_semantics=sem, vmem_limit_bytes=VMEM_LIMIT_BYTES)


def _resident(shape):
    nd = len(shape)
    return pl.BlockSpec(shape, lambda *_: (0,) * nd, pipeline_mode=pl.Buffered(1))


def _rms(x, w):
    return x * lax.rsqrt(jnp.mean(x * x, axis=-1, keepdims=True) + EPS) * w


def _sigmoid(x):
    return 1.0 / (1.0 + jnp.exp(-x))


def _softplus(x):
    return jnp.maximum(x, 0.0) + jnp.log1p(jnp.exp(-jnp.abs(x)))


def _dot(a, b):
    return jnp.dot(a, b, preferred_element_type=F32)


def _dot_nt(a, b):
    return lax.dot_general(a, b, NT_DIMS, preferred_element_type=F32)


def _dot_tn(a, b):
    return lax.dot_general(a, b, TN_DIMS, preferred_element_type=F32)


def _split3(x):
    hi = x.astype(BF16)
    r1 = x - hi.astype(F32)
    mid = r1.astype(BF16)
    return hi, mid, (r1 - mid.astype(F32)).astype(BF16)


def _dot_mask_f32(mask16, x):
    hi, mid, lo = _split3(x)
    return _dot(mask16, hi) + _dot(mask16, mid) + _dot(mask16, lo)


def _dot_f32_mask(x, mask16):
    hi, mid, lo = _split3(x)
    return _dot(hi, mask16) + _dot(mid, mask16) + _dot(lo, mask16)


def _norm_proj_kernel(x_ref, nw_ref, w_ref, *out_refs, segs):
    xn = _rms(x_ref[...], nw_ref[...]).astype(BF16)
    for (start, width), o_ref in zip(segs, out_refs):
        o_ref[...] = _dot(xn, w_ref[:, start:start + width]).astype(o_ref.dtype)


def norm_proj(x, norm_w, w, segs, dtypes, tm):
    t, k = x.shape
    n = w.shape[1]
    return pl.pallas_call(
        functools.partial(_norm_proj_kernel, segs=tuple(segs)),
        grid=(t // tm,),
        in_specs=[pl.BlockSpec((tm, k), lambda i: (i, 0)),
                  _resident((1, k)),
                  _resident((k, n))],
        out_specs=[pl.BlockSpec((tm, wd), lambda i: (i, 0)) for _, wd in segs],
        out_shape=[jax.ShapeDtypeStruct((t, wd), dt) for (_, wd), dt in zip(segs, dtypes)],
        compiler_params=_cparams("parallel"),
        name="norm_proj",
    )(x, norm_w.reshape(1, k), w)


def _win_attn_kernel(sink_ref, *refs, nq):
    bias_refs = refs[:nq]
    q_ref, kp_ref, kc_ref, kn_ref, vp_ref, vc_ref, vn_ref, o_ref = refs[nq:]
    kall = jnp.concatenate([kp_ref[...], kc_ref[...], kn_ref[...]], axis=0)
    vall = jnp.concatenate([vp_ref[...], vc_ref[...], vn_ref[...]], axis=0)
    sink2 = [sink_ref[h] * LOG2E for h in range(A_HEADS)]
    items = [(j, h) for j in range(nq) for h in range(A_HEADS)]
    qrows = lambda j: slice(j * BLOCK, (j + 1) * BLOCK)
    krows = lambda j: slice(j * BLOCK, (j + 3) * BLOCK)
    kcol = lambda h: slice((h // A_GROUP) * LANE, (h // A_GROUP + 1) * LANE)
    vcol = lambda h: slice((2 * (h // A_GROUP) + h % 2) * LANE, (2 * (h // A_GROUP) + h % 2 + 1) * LANE)
    s = [_dot_nt(q_ref[qrows(j), h * LANE:(h + 1) * LANE], kall[krows(j), kcol(h)]) for j, h in items]
    s = [sv + bias_refs[j][h] for (j, h), sv in zip(items, s)]
    m = [jnp.maximum(jnp.max(sv, axis=-1, keepdims=True), sink2[h]) for (_, h), sv in zip(items, s)]
    p = _each(lambda sv, mv: jnp.exp2(sv - mv), s, m)
    rden = [1.0 / (jnp.sum(pv_, axis=-1, keepdims=True) + jnp.exp2(sink2[h] - mv))
            for (_, h), pv_, mv in zip(items, p, m)]
    pv = {it: _dot(pp.astype(BF16), vall[krows(it[0]), vcol(it[1])]) * rd for it, pp, rd in zip(items, p, rden)}
    for j in range(nq):
        for pair in range(A_HEADS // 2):
            o_ref[qrows(j), pair * LANE:(pair + 1) * LANE] = (pv[(j, 2 * pair)] + pv[(j, 2 * pair + 1)]).astype(o_ref.dtype)


def _window_bias():
    qi = np.arange(BLOCK)[:, None]
    kj = np.arange(3 * BLOCK)[None, :]
    dist = np.abs(qi + BLOCK - kj)
    slopes = np.array([2.0 ** (-8.0 * (h + 1) / A_HEADS) for h in range(A_HEADS)], np.float32)
    base = -slopes[:, None, None] * dist[None].astype(np.float32)
    out = []
    for first in (0, 1):
        for last in (0, 1):
            valid = (dist <= WINDOW) & ((kj >= BLOCK) | (first == 0)) & ((kj < 2 * BLOCK) | (last == 0))
            out.append(np.where(valid[None], base, -np.inf))
    return jnp.asarray(np.stack(out) * LOG2E, F32)


def window_attention(q, k, v, sink, batch, seq):
    nb = seq // BLOCK
    nq = 2 if nb % 2 == 0 else 1
    ns = nb // nq
    cur = lambda b, n: (b * ns + n, 0)
    prev = lambda b, n: (b * nb + jnp.maximum(n * nq - 1, 0), 0)
    nxt = lambda b, n: (b * nb + jnp.minimum((n + 1) * nq, nb - 1), 0)

    def variant(j):
        first = lambda n: (n == 0).astype(jnp.int32) if j == 0 else 0
        last = lambda n: (n == ns - 1).astype(jnp.int32) if j == nq - 1 else 0
        return lambda b, n: (2 * first(n) + last(n), 0, 0, 0)

    kw, vw = k.shape[1], v.shape[1]
    bias = _window_bias()
    return pl.pallas_call(
        functools.partial(_win_attn_kernel, nq=nq),
        grid=(batch, ns),
        in_specs=([pl.BlockSpec(memory_space=pltpu.SMEM)]
                  + [pl.BlockSpec((None, A_HEADS, BLOCK, 3 * BLOCK), variant(j)) for j in range(nq)]
                  + [pl.BlockSpec((nq * BLOCK, q.shape[1]), cur),
                     pl.BlockSpec((BLOCK, kw), prev), pl.BlockSpec((nq * BLOCK, kw), cur), pl.BlockSpec((BLOCK, kw), nxt),
                     pl.BlockSpec((BLOCK, vw), prev), pl.BlockSpec((nq * BLOCK, vw), cur),
                     pl.BlockSpec((BLOCK, vw), nxt)]),
        out_specs=pl.BlockSpec((nq * BLOCK, A_HEADS * HEAD_DIM), cur),
        out_shape=jax.ShapeDtypeStruct((batch * seq, A_HEADS * HEAD_DIM), BF16),
        compiler_params=_cparams("parallel", "parallel"),
        name="window_attention",
    )(sink, *([bias] * nq), q, k, k, k, v, v, v)


def _attn_in_kernel(x_ref, nw_ref, w_ref, qnw_ref, wq_ref, kvnw_ref, wk_ref, wv_ref, qtab_ref, ctab_ref, stab_ref,
                    qa_ref, ka_ref, va_ref, q_ref, k_ref, v_ref, *, segs):
    xn = _rms(x_ref[...], nw_ref[...]).astype(BF16)
    proj = [_dot(xn, w_ref[:, start:start + width]) for start, width in segs]
    qa_ref[...] = (proj[0] * WINDOW_Q_SCALE).astype(qa_ref.dtype)
    ka_ref[...] = proj[1].astype(ka_ref.dtype)
    va_ref[...] = proj[2].astype(va_ref.dtype)
    cq, ckv, kab = proj[3:]
    xq = _rms(cq, qnw_ref[...]).astype(BF16)
    tab = qtab_ref[...]
    for h in range(B_HEADS):
        cols = slice(h * LANE, (h + 1) * LANE)
        q_ref[:, cols] = (_dot(xq, wq_ref[:, cols]) * tab).astype(q_ref.dtype)
    xkv = _rms(ckv, kvnw_ref[...]).astype(BF16)
    kr = kab[:, :LANE] * ctab_ref[...] + kab[:, LANE:] * stab_ref[...]
    for h in range(B_HEADS):
        cols = slice(h * LANE, (h + 1) * LANE)
        k_ref[:, cols] = (_dot(xkv, wk_ref[:, cols]) + kr).astype(k_ref.dtype)
    lane = lax.broadcasted_iota(jnp.int32, v_ref.shape, 1) % LANE
    v_ref[...] = jnp.where(lane == B_VDIM, 1.0, _dot(xkv, wv_ref[...])).astype(v_ref.dtype)


def attn_in_proj(x, norm_w, w, segs, q_norm_w, wq, kv_norm_w, wk, wv, q_tab, k_cos, k_sin, seq, tm):
    t, k = x.shape
    nt = seq // tm
    row = lambda i: (i, 0)
    pos = lambda i: (i % nt, 0)
    widths = [wd for _, wd in segs[:3]] + [wq.shape[1], wk.shape[1], wv.shape[1]]
    return pl.pallas_call(
        functools.partial(_attn_in_kernel, segs=tuple(segs)),
        grid=(t // tm,),
        in_specs=[pl.BlockSpec((tm, k), row), _resident((1, k)), _resident(w.shape),
                  _resident((1, wq.shape[0])), _resident(wq.shape),
                  _resident((1, wk.shape[0])), _resident(wk.shape), _resident(wv.shape),
                  pl.BlockSpec((tm, LANE), pos), pl.BlockSpec((tm, LANE), pos), pl.BlockSpec((tm, LANE), pos)],
        out_specs=[pl.BlockSpec((tm, wd), row) for wd in widths],
        out_shape=[jax.ShapeDtypeStruct((t, wd), BF16) for wd in widths],
        compiler_params=_cparams("parallel"),
        name="attn_in_proj",
    )(x, norm_w.reshape(1, k), w, q_norm_w.reshape(1, -1), wq, kv_norm_w.reshape(1, -1), wk, wv,
      q_tab, k_cos, k_sin)


def _mla_flash_kernel(q_ref, k_ref, v_ref, o_ref, s_ref, p_ref, m_ref, a_ref, acc_ref, *, tk, nk, rb):
    tq = q_ref.shape[0]
    heads = range(2)
    m_ref[...] = jnp.full(m_ref.shape, -jnp.inf, F32)
    acc_ref[...] = jnp.zeros(acc_ref.shape, F32)

    def body(j, carry):
        rows = pl.ds(pl.multiple_of(j * tk, tk), tk)
        for e in heads:
            cols = slice(e * LANE, (e + 1) * LANE)
            s_ref[e] = _dot_nt(q_ref[:, cols], k_ref[rows, cols])
        for e in heads:
            cols = slice(e * LANE, (e + 1) * LANE)
            for r in range(tq // rb):
                rs = slice(r * rb, (r + 1) * rb)
                s = s_ref[e, rs, :]
                m_old = m_ref[e, rs, :]
                m_new = jnp.maximum(m_old, jnp.max(s, axis=-1, keepdims=True))
                m_ref[e, rs, :] = m_new
                a_ref[e, rs, :] = jnp.exp2(m_old - m_new)
                p_ref[e, rs, :] = jnp.exp2(s - jnp.concatenate([m_new] * (tk // LANE), axis=1)).astype(BF16)
            acc_ref[e] = a_ref[e] * acc_ref[e] + _dot(p_ref[e], v_ref[rows, cols])
        return carry

    lax.fori_loop(0, nk, body, 0)
    outs = []
    for e in heads:
        acc = acc_ref[e]
        outs.append(acc / acc[:, B_VDIM:B_VDIM + 1])
    lane = lax.broadcasted_iota(jnp.int32, (tq, LANE), 1)
    o_ref[...] = jnp.where(lane < B_VDIM, outs[0], pltpu.roll(outs[1], B_VDIM, axis=1)).astype(o_ref.dtype)


def mla_flash(q, k, v, batch, seq, tq, tk, rb):
    nq = seq // tq
    npair = B_HEADS // 2
    return pl.pallas_call(
        functools.partial(_mla_flash_kernel, tk=tk, nk=seq // tk, rb=rb),
        grid=(batch, npair, nq),
        in_specs=[pl.BlockSpec((tq, 2 * LANE), lambda b, p, i: (b * nq + i, p)),
                  pl.BlockSpec((seq, 2 * LANE), lambda b, p, i: (b, p)),
                  pl.BlockSpec((seq, 2 * LANE), lambda b, p, i: (b, p))],
        out_specs=pl.BlockSpec((tq, LANE), lambda b, p, i: (b * nq + i, p)),
        out_shape=jax.ShapeDtypeStruct((batch * seq, B_HEADS * B_VDIM), BF16),
        scratch_shapes=[pltpu.VMEM((2, tq, tk), F32), pltpu.VMEM((2, tq, tk), BF16), pltpu.VMEM((2, tq, LANE), F32),
                        pltpu.VMEM((2, tq, LANE), F32), pltpu.VMEM((2, tq, LANE), F32)],
        compiler_params=_cparams("parallel", "parallel", "parallel"),
        name="mla_flash",
    )(q, k, v)


def _mlp_tail(x, nw_ref, w1_ref, w2_ref, fw_ref, o_ref, fchunk, final_norm):
    xn = _rms(x, nw_ref[...]).astype(BF16)
    acc = x
    for f in range(w1_ref.shape[1] // fchunk):
        cols = slice(f * fchunk, (f + 1) * fchunk)
        h = jnp.square(jnp.maximum(_dot(xn, w1_ref[:, cols]), 0.0)).astype(BF16)
        acc = acc + _dot(h, w2_ref[cols, :])
    if final_norm:
        acc = _rms(acc, fw_ref[...])
    o_ref[...] = acc


def _attn_out_mlp_kernel(x_ref, a_ref, b_ref, wo_ref, nw_ref, w1_ref, w2_ref, fw_ref, o_ref, *, fchunk, final_norm):
    ka = a_ref.shape[1]
    o_ref[...] = x_ref[...] + _dot(a_ref[...], wo_ref[:ka, :]) + _dot(b_ref[...], wo_ref[ka:, :])
    _mlp_tail(o_ref[...], nw_ref, w1_ref, w2_ref, fw_ref, o_ref, fchunk, final_norm)


def _lin_out_mlp_kernel(x_ref, cf_ref, cb_ref, df_ref, db_ref, gc_ref, gd_ref, cw_ref, dw_ref, wo_ref,
                        nw_ref, w1_ref, w2_ref, fw_ref, o_ref, *, fchunk, final_norm):
    def gated(f_ref, b_ref, gate_ref, gnw_ref):
        o = f_ref[...] + b_ref[...]
        parts = []
        for h in range(LIN_HEADS):
            cols = slice(h * LIN_DV, (h + 1) * LIN_DV)
            gate = gate_ref[:, cols].astype(F32)
            parts.append((_rms(o[:, cols], gnw_ref[...]) * (gate * _sigmoid(gate))).astype(BF16))
        return jnp.concatenate(parts, axis=1)

    kc = cf_ref.shape[1]
    o_ref[...] = (x_ref[...] + _dot(gated(cf_ref, cb_ref, gc_ref, cw_ref), wo_ref[:kc, :])
                  + _dot(gated(df_ref, db_ref, gd_ref, dw_ref), wo_ref[kc:, :]))
    _mlp_tail(o_ref[...], nw_ref, w1_ref, w2_ref, fw_ref, o_ref, fchunk, final_norm)


def mixer_out_mlp(x, mixer_inputs, small_params, w_out, norm_w, w1, w2, final_w, final_norm, tm, fchunk):
    t, d = x.shape
    row = lambda i: (i, 0)
    body = _attn_out_mlp_kernel if len(mixer_inputs) == 2 else _lin_out_mlp_kernel
    return pl.pallas_call(
        functools.partial(body, fchunk=fchunk, final_norm=final_norm),
        grid=(t // tm,),
        in_specs=([pl.BlockSpec((tm, d), row)] + [pl.BlockSpec((tm, a.shape[1]), row) for a in mixer_inputs]
                  + [_resident(p.shape) for p in small_params]
                  + [_resident(w_out.shape), _resident((1, d)), _resident(w1.shape), _resident(w2.shape),
                     _resident((1, d))]),
        out_specs=pl.BlockSpec((tm, d), row),
        out_shape=jax.ShapeDtypeStruct((t, d), F32),
        compiler_params=_cparams("parallel"),
        name="mixer_out_mlp",
    )(x, *mixer_inputs, *small_params, w_out, norm_w.reshape(1, d), w1, w2, final_w.reshape(1, d))


def _direction_masks(d):
    sgn = 1 - 2 * d
    r = lax.broadcasted_iota(jnp.int32, (STACK, STACK), 0)
    c = lax.broadcasted_iota(jnp.int32, (STACK, STACK), 1)
    same = (r & -CHUNK) == (c & -CHUNK)
    order = ((r & (CHUNK - 1)) - (c & (CHUNK - 1))) * sgn
    i = lax.broadcasted_iota(jnp.int32, (CHUNK, CHUNK), 0)
    j = lax.broadcasted_iota(jnp.int32, (CHUNK, CHUNK), 1)
    cum = (((i - j) * sgn) >= 0).astype(F32)
    return same, same & (order >= 0), same & (order > 0), cum


def _block_diag(x, same):
    return jnp.where(same, jnp.concatenate([x] * LIN_HEADS, axis=0), 0.0)


def _block_diag_cols(x, same):
    return jnp.where(same, jnp.concatenate([x] * LIN_HEADS, axis=1), 0.0)


def _stack_heads(v, width):
    return jnp.concatenate([v[:, h * width:(h + 1) * width] for h in range(LIN_HEADS)], axis=0)


def _each(f, *lists):
    return [f(*args) for args in zip(*lists)]


def _scan_order(ncb):
    return [((0, i), (1, ncb - 1 - i)) for i in range(ncb)]


def _gla_kernel(qf_ref, kf_ref, vf_ref, sf_ref, qb_ref, kb_ref, vb_ref, sb_ref, wg_ref, bg_ref,
                of_ref, ob_ref, st_ref, *, ncb, nb):
    @pl.when(pl.program_id(1) == 0)
    def _():
        st_ref[...] = jnp.zeros_like(st_ref)

    refs = ((qf_ref, kf_ref, vf_ref, sf_ref, of_ref), (qb_ref, kb_ref, vb_ref, sb_ref, ob_ref))
    masks = [_direction_masks(d) for d in range(2)]
    items = [(bi, d, ci) for bi in range(nb) for d in range(2) for ci in range(ncb)]
    rows = lambda ci: slice(ci * CHUNK, (ci + 1) * CHUNK)
    chunk = lambda which, it: refs[it[1]][which][it[0], rows(it[2]), :]
    same = [masks[d][0] for _, d, _ in items]
    z = [_dot(chunk(3, it).astype(BF16), wg_ref[it[1]]) + bg_ref[it[1]] for it in items]
    log_a = _each(lambda a: (jnp.minimum(a, 0.0) - jnp.log1p(jnp.exp(-jnp.abs(a)))) * (1.0 / GATE_NORM), z)
    b = [_dot_mask_f32(masks[it[1]][3].astype(BF16), a) for it, a in zip(items, log_a)]
    b_last = _each(lambda a: jnp.sum(a, axis=0, keepdims=True), log_a)
    q_in = [_block_diag(chunk(0, it) * (LIN_DK ** -0.5) * jnp.exp(bb), sm).astype(BF16)
            for it, bb, sm in zip(items, b, same)]
    k = [chunk(1, it) for it in items]
    k_in = _each(lambda a, bb, sm: _block_diag(a * jnp.exp(-bb), sm).astype(BF16), k, b, same)
    k_out = _each(lambda a, bb, bl, sm: _block_diag(a * jnp.exp(bl - bb), sm).astype(BF16), k, b, b_last, same)
    v = [_stack_heads(chunk(2, it), LIN_DV).astype(BF16) for it in items]
    sc = [jnp.where(masks[it[1]][1], _dot_nt(a, bb), 0.0).astype(BF16) for it, a, bb in zip(items, q_in, k_in)]
    o_intra = dict(zip(items, _each(_dot, sc, v)))
    ds = dict(zip(items, _each(_dot_tn, v, k_out)))
    dec = dict(zip(items, _each(jnp.exp, b_last)))
    q_in = dict(zip(items, q_in))

    streams = [(bi, d) for bi in range(nb) for d in range(2)]
    st = {sd: st_ref[i] for i, sd in enumerate(streams)}
    for fwd_bwd in _scan_order(ncb):
        step = [(bi, d, ci) for bi in range(nb) for d, ci in fwd_bwd]
        o = [o_intra[it] + _dot_nt(q_in[it], st[it[:2]].astype(BF16)) for it in step]
        for it, od in zip(step, o):
            bi, d, ci = it
            st[(bi, d)] = dec[it] * st[(bi, d)] + ds[it]
            for h in range(LIN_HEADS):
                refs[d][4][bi, rows(ci), h * LIN_DV:(h + 1) * LIN_DV] = od[h * CHUNK:(h + 1) * CHUNK, :]
    for i, sd in enumerate(streams):
        st_ref[i] = st[sd]


def _bidir_call(body, name, arrays, extra, extra_specs, batch, seq, tc, nb, hk, hv):
    nblk = seq // tc
    ins = [a.reshape(batch, seq, a.shape[1]) for a in arrays]
    fwd = lambda bb, n: (bb, n, 0)
    bwd = lambda bb, n: (bb, nblk - 1 - n, 0)
    spec = lambda a, idx: pl.BlockSpec((nb, tc, a.shape[2]), idx)
    o_f, o_b = pl.pallas_call(
        functools.partial(body, ncb=tc // CHUNK, nb=nb),
        grid=(batch // nb, nblk),
        in_specs=[spec(a, fwd) for a in ins] + [spec(a, bwd) for a in ins] + extra_specs,
        out_specs=[pl.BlockSpec((nb, tc, hv), fwd), pl.BlockSpec((nb, tc, hv), bwd)],
        out_shape=[jax.ShapeDtypeStruct((batch, seq, hv), F32)] * 2,
        scratch_shapes=[pltpu.VMEM((2 * nb, LIN_DV, hk), F32)],
        compiler_params=_cparams("parallel", "arbitrary"),
        name=name,
    )(*ins, *ins, *extra)
    return o_f.reshape(batch * seq, hv), o_b.reshape(batch * seq, hv)


def gla_scan(q, k, v, small, wg, bg, batch, seq, tc, nb):
    return _bidir_call(_gla_kernel, "gla_scan", (q, k, v, small), (wg, bg),
                       [_resident(wg.shape), _resident(bg.shape)], batch, seq, tc, nb, q.shape[1], v.shape[1])


def _gdn_prep_kernel(xp_ref, xc_ref, xn_ref, sm_ref, cw_ref, gsum_ref, alog_ref, dtb_ref,
                     q_ref, k_ref, v_ref, g_ref, b_ref, bt_ref, buf_ref, *, nt):
    i = pl.program_id(1)
    tm = xc_ref.shape[0]
    halo = xp_ref.shape[0]
    buf_ref[:halo, :] = xp_ref[...].astype(F32) * (i > 0).astype(F32)
    buf_ref[halo:halo + tm, :] = xc_ref[...].astype(F32)
    buf_ref[halo + tm:, :] = xn_ref[...].astype(F32) * (i < nt - 1).astype(F32)
    first = halo - CONV_TAPS // 2
    acc = buf_ref[first:first + tm, :] * cw_ref[0:1, :]
    for j in range(1, CONV_TAPS):
        acc = acc + buf_ref[first + j:first + j + tm, :] * cw_ref[j:j + 1, :]
    y = acc * _sigmoid(acc)
    nqk = LIN_HEADS * LIN_DK
    for idx, (o_ref, scale) in enumerate(((q_ref, LIN_DK ** -0.5), (k_ref, 1.0))):
        part = y[:, idx * nqk:(idx + 1) * nqk]
        ssq = _dot_f32_mask(part * part, gsum_ref[...])
        o_ref[...] = (part * lax.rsqrt(ssq + EPS) * scale).astype(o_ref.dtype)
    v_ref[...] = y[:, 2 * nqk:].astype(v_ref.dtype)
    sm = sm_ref[...]
    beta = _sigmoid(sm)
    g = -jnp.exp(alog_ref[...]) * _softplus(sm + dtb_ref[...])
    lane = lax.broadcasted_iota(jnp.int32, sm.shape, 1) - GATE_LANE0
    is_beta = (lane >= 0) & (lane < 2 * LIN_HEADS)
    is_g = (lane >= 2 * LIN_HEADS) & (lane < 4 * LIN_HEADS)
    g = jnp.where(is_g, g, 0.0)
    g_ref[...] = jnp.where(is_beta, beta, g)
    ri = lax.broadcasted_iota(jnp.int32, (CHUNK, CHUNK), 0)
    ci = lax.broadcasted_iota(jnp.int32, (CHUNK, CHUNK), 1)
    tril16 = (ri >= ci).astype(BF16)
    ones16 = jnp.ones((CHUNK, CHUNK), BF16)
    is_bwd = lax.broadcasted_iota(jnp.int32, (CHUNK, LANE), 1) >= GATE_LANE0 + 3 * LIN_HEADS
    for c in range(tm // CHUNK):
        rows = slice(c * CHUNK, (c + 1) * CHUNK)
        gc = g[rows, :]
        prefix = _dot_mask_f32(tril16, gc)
        total = _dot_mask_f32(ones16, gc)
        b_ref[rows, :] = jnp.where(is_bwd, total - prefix + gc, prefix)
        bt_ref[rows, :] = total


def gdn_prep(dqkv, small, conv_w, gsum, alog, dtb, batch, seq, tm):
    nt = seq // tm
    t, c = dqkv.shape
    halo = 2 * SUBLANE
    hb = tm // halo
    nqk = LIN_HEADS * LIN_DK
    row = lambda b, i: (b * nt + i, 0)
    prev = lambda b, i: (jnp.maximum((b * nt + i) * hb - 1, 0), 0)
    nxt = lambda b, i: (jnp.minimum((b * nt + i + 1) * hb, t // halo - 1), 0)
    return pl.pallas_call(
        functools.partial(_gdn_prep_kernel, nt=nt),
        grid=(batch, nt),
        in_specs=[pl.BlockSpec((halo, c), prev), pl.BlockSpec((tm, c), row), pl.BlockSpec((halo, c), nxt),
                  pl.BlockSpec((tm, LANE), row),
                  _resident(conv_w.shape), _resident(gsum.shape), _resident((1, LANE)), _resident((1, LANE))],
        out_specs=[pl.BlockSpec((tm, nqk), row), pl.BlockSpec((tm, nqk), row),
                   pl.BlockSpec((tm, c - 2 * nqk), row)] + [pl.BlockSpec((tm, LANE), row)] * 3,
        out_shape=[jax.ShapeDtypeStruct((t, nqk), BF16), jax.ShapeDtypeStruct((t, nqk), BF16),
                   jax.ShapeDtypeStruct((t, c - 2 * nqk), BF16)] + [jax.ShapeDtypeStruct((t, LANE), F32)] * 3,
        scratch_shapes=[pltpu.VMEM((tm + 2 * halo, c), F32)],
        compiler_params=_cparams("parallel", "parallel"),
        name="gdn_prep",
    )(dqkv, dqkv, dqkv, small, conv_w, gsum, alog, dtb)


def _gdn_kernel(qf_ref, kf_ref, vf_ref, gf_ref, bf_ref, tf_ref, qb_ref, kb_ref, vb_ref, gb_ref, bb_ref, tb_ref,
                of_ref, ob_ref, st_ref, *, ncb, nb):
    @pl.when(pl.program_id(1) == 0)
    def _():
        st_ref[...] = jnp.zeros_like(st_ref)

    refs = ((qf_ref, kf_ref, vf_ref, gf_ref, bf_ref, tf_ref, of_ref),
            (qb_ref, kb_ref, vb_ref, gb_ref, bb_ref, tb_ref, ob_ref))
    masks = [_direction_masks(d) for d in range(2)]
    r = lax.broadcasted_iota(jnp.int32, (STACK, STACK), 0)
    c = lax.broadcasted_iota(jnp.int32, (STACK, STACK), 1)
    diag = r == c
    eye = diag.astype(F32)
    lane_blk = lax.broadcasted_iota(jnp.int32, (SUBLANE, STACK), 1) // CHUNK

    def rows_of(x, lane0):
        return jnp.concatenate(
            [jnp.broadcast_to(x[:, lane0 + h:lane0 + h + 1], (CHUNK, STACK)) for h in range(LIN_HEADS)], axis=0)

    def head_lanes(row, lane0):
        out = jnp.broadcast_to(row[:, lane0:lane0 + 1], (SUBLANE, STACK))
        for h in range(1, LIN_HEADS):
            out = jnp.where(lane_blk == h, jnp.broadcast_to(row[:, lane0 + h:lane0 + h + 1], (SUBLANE, STACK)), out)
        return out

    each = _each
    items = [(bi, d, ci) for bi in range(nb) for d in range(2) for ci in range(ncb)]
    rows = lambda ci: slice(ci * CHUNK, (ci + 1) * CHUNK)
    chunk = lambda which, it: refs[it[1]][which][it[0], rows(it[2]), :]
    same = [masks[d][0] for _, d, _ in items]
    beta_col = [rows_of(chunk(3, it), _gate_lanes(it[1])[0]) for it in items]
    b_col = [rows_of(chunk(4, it), _gate_lanes(it[1])[1]) for it in items]
    bl_col = [rows_of(chunk(5, it), _gate_lanes(it[1])[1]) for it in items]
    k_bd = [_block_diag(chunk(1, it), sm) for it, sm in zip(items, same)]
    q_bd = [_block_diag(chunk(0, it), sm) for it, sm in zip(items, same)]
    kb_bd = each(lambda a, b: a * b, k_bd, beta_col)
    k16 = each(lambda a: a.astype(BF16), k_bd)
    kk = each(lambda a, b: _dot_nt(a.astype(BF16), b), kb_bd, k16)
    qk = each(lambda a, b: _dot_nt(a.astype(BF16), b), q_bd, k16)
    b_row = each(lambda bc: jnp.sum(jnp.where(diag, bc, 0.0), axis=0, keepdims=True), b_col)
    gamma = [jnp.exp(jnp.where(masks[it[1]][1], bc - br, -jnp.inf)) for it, bc, br in zip(items, b_col, b_row)]
    lower = [jnp.where(masks[it[1]][2], a * g, 0.0) for it, a, g in zip(items, kk, gamma)]
    attn = each(lambda a, g: (a * g).astype(BF16), qk, gamma)
    l_hi = each(lambda a: a.astype(BF16), lower)
    pw = [l_hi]
    for _ in range(5):
        pw.append(each(lambda a: _dot(a, a).astype(BF16), pw[-1]))

    def pair(a, b16):
        return a + _dot(a.astype(BF16), b16)

    f01 = each(lambda lo, p1: pair(eye - lo, p1), lower, pw[1])
    f23 = each(lambda p2, p3: pair(eye + p2.astype(F32), p3), pw[2], pw[3])
    f45 = each(lambda p4, p5: pair(eye + p4.astype(F32), p5), pw[4], pw[5])
    f0123 = each(lambda a, b: _dot(a.astype(BF16), b.astype(BF16)).astype(BF16), f01, f23)
    inv16 = each(lambda a, b: _dot(a, b.astype(BF16)).astype(BF16), f0123, f45)
    e_b = each(jnp.exp, b_col)
    vb = [_stack_heads(chunk(2, it), LIN_DV) * bc[:, :LIN_DV] for it, bc in zip(items, beta_col)]
    kbe = [_stack_heads(chunk(1, it), LIN_DK) * (bc * e)[:, :LIN_DK] for it, bc, e in zip(items, beta_col, e_b)]
    rhs = each(lambda v, kb: jnp.concatenate([v, kb], axis=1), vb, kbe)
    x16 = each(lambda m, r_: _dot(m, r_.astype(BF16)).astype(BF16), inv16, rhs)
    l_lo = each(lambda lo, hi: (lo - hi.astype(F32)).astype(BF16), lower, l_hi)
    resid = each(lambda r_, x, hi, lo: r_ - (x.astype(F32) + _dot(hi, x) + _dot(lo, x)), rhs, x16, l_hi, l_lo)
    sol = each(lambda x, m, rs: x.astype(F32) + _dot(m, rs.astype(BF16)), x16, inv16, resid)
    u = {it: sol[i][:, :LIN_DV] for i, it in enumerate(items)}
    w = {it: _block_diag_cols(sol[i][:, LIN_DV:].astype(BF16), same[i])
         for i, it in enumerate(items)}
    attn = dict(zip(items, attn))
    q_dec = {it: (q_bd[i] * e_b[i]).astype(BF16) for i, it in enumerate(items)}
    k_dec = {it: (k_bd[i] * jnp.exp(bl_col[i] - b_col[i])).astype(BF16) for i, it in enumerate(items)}
    dec = {(bi, d, ci): jnp.exp(head_lanes(refs[d][5][bi, ci * CHUNK:ci * CHUNK + SUBLANE, :],
                                           _gate_lanes(d)[1]))[:1, :] for bi, d, ci in items}

    streams = [(bi, d) for bi in range(nb) for d in range(2)]
    st = {sd: st_ref[i] for i, sd in enumerate(streams)}
    for fwd_bwd in _scan_order(ncb):
        step = [(bi, d, ci) for bi in range(nb) for d, ci in fwd_bwd]
        st16 = [st[it[:2]].astype(BF16) for it in step]
        v_new16 = [(u[it] - _dot_nt(w[it], s16)).astype(BF16) for it, s16 in zip(step, st16)]
        o = [_dot_nt(q_dec[it], s16) + _dot(attn[it], vn) for it, s16, vn in zip(step, st16, v_new16)]
        for it, od, vn in zip(step, o, v_new16):
            bi, d, ci = it
            st[(bi, d)] = dec[it] * st[(bi, d)] + _dot_tn(vn, k_dec[it])
            for h in range(LIN_HEADS):
                refs[d][6][bi, rows(ci), h * LIN_DV:(h + 1) * LIN_DV] = od[h * CHUNK:(h + 1) * CHUNK, :]
    for i, sd in enumerate(streams):
        st_ref[i] = st[sd]


def _gate_lanes(d):
    return GATE_LANE0 + d * LIN_HEADS, GATE_LANE0 + (2 + d) * LIN_HEADS


def gdn_scan(q, k, v, mixed, bcum, btot, batch, seq, tc, nb):
    return _bidir_call(_gdn_kernel, "gdn_scan", (q, k, v, mixed, bcum, btot), (), [],
                       batch, seq, tc, nb, q.shape[1], v.shape[1])


def _rot_half_cols(w):
    half = w.shape[-1] // 2
    return jnp.concatenate([-w[..., half:], w[..., :half]], axis=-1)


def _pad_heads(w, heads, dim, offset=0):
    k = w.shape[0]
    out = jnp.zeros((k, heads, LANE), w.dtype)
    out = out.at[:, :, offset:offset + dim].set(w.reshape(k, heads, dim))
    return out.reshape(k, heads * LANE)


def _layer0_in_weight(w_in):
    aq = A_HEADS * HEAD_DIM
    akv = A_KV_HEADS * HEAD_DIM
    o = 0
    w_q = w_in[:, o:o + aq]; o += aq
    w_k = w_in[:, o:o + akv]; o += akv
    w_v = w_in[:, o:o + akv]; o += akv
    w_cq = w_in[:, o:o + B_Q_RANK]; o += B_Q_RANK
    w_ckv = w_in[:, o:o + B_KV_RANK]; o += B_KV_RANK
    w_pe = w_in[:, o:o + B_ROPE]
    zeros = jnp.zeros((w_in.shape[0], B_NOPE), w_in.dtype)
    v_lo = _pad_heads(w_v, A_KV_HEADS, HEAD_DIM, 0).reshape(-1, A_KV_HEADS, LANE)
    v_hi = _pad_heads(w_v, A_KV_HEADS, HEAD_DIM, HEAD_DIM).reshape(-1, A_KV_HEADS, LANE)
    w_v2 = jnp.stack([v_lo, v_hi], axis=2).reshape(w_in.shape[0], 2 * A_KV_HEADS * LANE)
    w_pe_rot = _rot_half_cols(w_pe)
    pieces = [_pad_heads(w_q, A_HEADS, HEAD_DIM), _pad_heads(w_k, A_KV_HEADS, HEAD_DIM), w_v2,
              w_cq, w_ckv,
              jnp.concatenate([zeros, w_pe, w_pe, zeros, w_pe_rot, w_pe_rot], axis=1)]
    segs, start = [], 0
    for p in pieces:
        segs.append((start, p.shape[1]))
        start += p.shape[1]
    return jnp.concatenate(pieces, axis=1).astype(BF16), segs


def _mla_weights(w_uq, w_ukv):
    r = w_uq.shape[0]
    wq = w_uq.reshape(r, B_HEADS, B_NOPE + B_ROPE)
    pe = wq[:, :, B_NOPE:]
    wq = jnp.concatenate([wq[:, :, :B_NOPE], pe, _rot_half_cols(pe)], axis=-1).reshape(r, B_HEADS * LANE)
    rk = w_ukv.shape[0]
    wkv = w_ukv.reshape(rk, B_HEADS, B_NOPE + B_VDIM)
    wk = _pad_heads(wkv[:, :, :B_NOPE].reshape(rk, B_HEADS * B_NOPE), B_HEADS, B_NOPE)
    wv = _pad_heads(wkv[:, :, B_NOPE:].reshape(rk, B_HEADS * B_VDIM), B_HEADS, B_VDIM)
    return wq.astype(BF16), wk.astype(BF16), wv.astype(BF16)


def _rope_tables(seq):
    half = B_ROPE // 2
    inv = ROPE_THETA ** (-jnp.arange(half, dtype=F32) / half)
    ang = jnp.arange(seq, dtype=F32)[:, None] * inv[None, :]
    cos2 = jnp.tile(jnp.cos(ang), (1, 4))
    sin2 = jnp.tile(jnp.sin(ang), (1, 4))
    zeros = jnp.zeros((seq, B_NOPE), F32)
    qscale = (B_NOPE + B_ROPE) ** -0.5 * LOG2E
    q_tab = jnp.concatenate([jnp.ones((seq, B_NOPE), F32), cos2[:, :B_ROPE], sin2[:, :B_ROPE]], axis=1) * qscale
    k_cos = jnp.concatenate([zeros, cos2], axis=1)
    k_sin = jnp.concatenate([zeros, sin2], axis=1)
    return q_tab, k_cos, k_sin


def _layer1_in_weight(w_in):
    hk = LIN_HEADS * LIN_DK
    hv = LIN_HEADS * LIN_DV
    conv_ch = 2 * hk + hv
    o = 2 * hk + 2 * hv
    w_main = w_in[:, :o]
    w_gl = w_in[:, o:o + 2 * GATE_RANK]; o += 2 * GATE_RANK
    w_conv = w_in[:, o:o + conv_ch]; o += conv_ch
    w_z = w_in[:, o:o + hv]; o += hv
    w_small = w_in[:, o:o + 4 * LIN_HEADS]
    pad = jnp.zeros((w_in.shape[0], LANE - 2 * GATE_RANK - 4 * LIN_HEADS), w_in.dtype)
    w = jnp.concatenate([w_main, w_conv, w_z, w_gl, w_small, pad], axis=1).astype(BF16)
    widths = [hk, hk, hv, hv, conv_ch, hv, LANE]
    segs, start = [], 0
    for wd in widths:
        segs.append((start, wd))
        start += wd
    return w, segs


def _pick_tile(n, pref):
    t = min(n, pref)
    while n % t:
        t //= 2
    return t


def kernel(x, att_norm, att_w_in, att_sink, mla_q_norm, mla_w_uq, mla_kv_norm, mla_w_ukv, att_w_out, lin_norm, lin_w_in, gla_w_gate_f, gla_b_gate_f, gla_w_gate_b, gla_b_gate_b, gla_norm, gdn_conv, gdn_a_log_f, gdn_dt_bias_f, gdn_a_log_b, gdn_dt_bias_b, gdn_norm, lin_w_out, mlp_norm, mlp_w1, mlp_w2, final_norm):
    batch, seq, dm = x.shape
    t = batch * seq
    depth = mlp_norm.shape[0]
    tm = _pick_tile(seq, ROW_TILE)
    xs = x.reshape(t, dm)
    for layer in range(depth):
        i = layer // 2
        if layer % 2 == 0:
            w0, segs = _layer0_in_weight(att_w_in[i])
            wq, wk, wv = _mla_weights(mla_w_uq[i], mla_w_ukv[i])
            q_tab, k_cos, k_sin = _rope_tables(seq)
            qa, ka, va, qb, kb, vb = attn_in_proj(xs, att_norm[i], w0, segs, mla_q_norm[i], wq, mla_kv_norm[i],
                                                  wk, wv, q_tab, k_cos, k_sin, seq, tm)
            o_a = window_attention(qa, ka, va, att_sink[i].astype(F32), batch, seq)
            o_b = mla_flash(qb, kb, vb, batch, seq, _pick_tile(seq, FLASH_Q_TILE), _pick_tile(seq, FLASH_KV_TILE),
                            FLASH_ROW_BLOCK)
            mixer_inputs, small_params, w_out = [o_a, o_b], [], att_w_out[i]
        else:
            w1, segs = _layer1_in_weight(lin_w_in[i])
            cq, ck, cv, cg, dqkv, dz, small = norm_proj(xs, lin_norm[i], w1, segs, [BF16] * 6 + [F32], tm)
            hk = LIN_HEADS * LIN_DK
            wg = jnp.zeros((2, LANE, hk), F32)
            wg = wg.at[0, :GATE_RANK].set(gla_w_gate_f[i]).at[1, GATE_RANK:2 * GATE_RANK].set(gla_w_gate_b[i])
            bg = jnp.stack([gla_b_gate_f[i], gla_b_gate_b[i]]).reshape(2, 1, hk).astype(F32)
            tc = _pick_tile(seq, SCAN_ROWS)
            nb = 2 if batch % 2 == 0 else 1
            o_cf, o_cb = gla_scan(cq, ck, cv, small, wg.astype(BF16), bg, batch, seq, _pick_tile(seq, 2 * SCAN_ROWS), nb)
            gsum = (jnp.arange(hk)[:, None] // LIN_DK == jnp.arange(hk)[None, :] // LIN_DK).astype(BF16)
            nh = LIN_HEADS
            a0 = GATE_LANE0 + 2 * nh
            alog = jnp.zeros((LANE,), F32).at[a0:a0 + nh].set(gdn_a_log_f[i]).at[a0 + nh:a0 + 2 * nh].set(gdn_a_log_b[i])
            dtb = jnp.zeros((LANE,), F32).at[a0:a0 + nh].set(gdn_dt_bias_f[i]).at[a0 + nh:a0 + 2 * nh].set(gdn_dt_bias_b[i])
            dq, dk, dv, mixed, bcum, btot = gdn_prep(dqkv, small, gdn_conv[i].astype(F32), gsum,
                                                     alog.reshape(1, LANE), dtb.reshape(1, LANE), batch, seq, tm)
            o_df, o_db = gdn_scan(dq, dk, dv, mixed, bcum, btot, batch, seq, tc, nb)
            mixer_inputs = [o_cf, o_cb, o_df, o_db, cg, dz]
            small_params = [gla_norm[i].reshape(1, LIN_DV), gdn_norm[i].reshape(1, LIN_DV)]
            w_out = lin_w_out[i]
        last = layer == depth - 1
        tm_mlp = _pick_tile(seq, ATTN_MLP_ROW_TILE) if layer % 2 == 0 else tm
        xs = mixer_out_mlp(xs, mixer_inputs, small_params, w_out.astype(BF16), mlp_norm[layer],
                           mlp_w1[layer].astype(BF16), mlp_w2[layer].astype(BF16), final_norm, last, tm_mlp,
                           MLP_F_CHUNK)
    return xs.reshape(batch, seq, dm)
```

```python
import functools
import math

import numpy as np
import jax
import jax.numpy as jnp
from jax import lax
from jax.experimental import pallas as pl
from jax.experimental.pallas import tpu as pltpu

F32 = jnp.float32
BF16 = jnp.bfloat16
EPS = 1e-6

LANE = 128
SUBLANE = 8
VMEM_LIMIT_BYTES = 56 * 1024 * 1024

ROW_TILE = 512
ATTN_MLP_ROW_TILE = 1024
MLP_F_CHUNK = 1024
FLASH_Q_TILE = 1024
FLASH_KV_TILE = 2048
FLASH_ROW_BLOCK = 64
SCAN_ROWS = 256

HEAD_DIM = 64
A_HEADS = 8
A_KV_HEADS = 2
A_GROUP = A_HEADS // A_KV_HEADS
WINDOW = 128
BLOCK = 128
B_HEADS = 8
B_NOPE = 64
B_ROPE = 32
B_VDIM = 64
B_Q_RANK = 384
B_KV_RANK = 256
ROPE_THETA = 10000.0
LIN_HEADS = 4
LIN_DK = 64
LIN_DV = 128
GATE_RANK = 16
GATE_NORM = 16.0
CHUNK = 64
CONV_TAPS = 5
STACK = LIN_HEADS * CHUNK
GATE_LANE0 = 2 * GATE_RANK

LOG2E = math.log2(math.e)
WINDOW_Q_SCALE = HEAD_DIM ** -0.5 * LOG2E

NT_DIMS = (((1,), (1,)), ((), ()))
TN_DIMS = (((0,), (0,)), ((), ()))


def _cparams(*sem):
    return pltpu.CompilerParams(dimension_semantics=sem, vmem_limit_bytes=VMEM_LIMIT_BYTES)


def _resident(shape):
    nd = len(shape)
    return pl.BlockSpec(shape, lambda *_: (0,) * nd, pipeline_mode=pl.Buffered(1))


def _rms(x, w):
    return x * lax.rsqrt(jnp.mean(x * x, axis=-1, keepdims=True) + EPS) * w


def _sigmoid(x):
    return 1.0 / (1.0 + jnp.exp(-x))


def _softplus(x):
    return jnp.maximum(x, 0.0) + jnp.log1p(jnp.exp(-jnp.abs(x)))


def _dot(a, b):
    return jnp.dot(a, b, preferred_element_type=F32)


def _dot_nt(a, b):
    return lax.dot_general(a, b, NT_DIMS, preferred_element_type=F32)


def _dot_tn(a, b):
    return lax.dot_general(a, b, TN_DIMS, preferred_element_type=F32)


def _split3(x):
    hi = x.astype(BF16)
    r1 = x - hi.astype(F32)
    mid = r1.astype(BF16)
    return hi, mid, (r1 - mid.astype(F32)).astype(BF16)


def _dot_mask_f32(mask16, x):
    hi, mid, lo = _split3(x)
    return _dot(mask16, hi) + _dot(mask16, mid) + _dot(mask16, lo)


def _dot_f32_mask(x, mask16):
    hi, mid, lo = _split3(x)
    return _dot(hi, mask16) + _dot(mid, mask16) + _dot(lo, mask16)


def _norm_proj_kernel(x_ref, nw_ref, w_ref, *out_refs, segs):
    xn = _rms(x_ref[...], nw_ref[...]).astype(BF16)
    for (start, width), o_ref in zip(segs, out_refs):
        o_ref[...] = _dot(xn, w_ref[:, start:start + width]).astype(o_ref.dtype)


def norm_proj(x, norm_w, w, segs, dtypes, tm):
    t, k = x.shape
    n = w.shape[1]
    return pl.pallas_call(
        functools.partial(_norm_proj_kernel, segs=tuple(segs)),
        grid=(t // tm,),
        in_specs=[pl.BlockSpec((tm, k), lambda i: (i, 0)),
                  _resident((1, k)),
                  _resident((k, n))],
        out_specs=[pl.BlockSpec((tm, wd), lambda i: (i, 0)) for _, wd in segs],
        out_shape=[jax.ShapeDtypeStruct((t, wd), dt) for (_, wd), dt in zip(segs, dtypes)],
        compiler_params=_cparams("parallel"),
        name="norm_proj",
    )(x, norm_w.reshape(1, k), w)


def _win_attn_kernel(sink_ref, *refs, nq):
    bias_refs = refs[:nq]
    q_ref, kp_ref, kc_ref, kn_ref, vp_ref, vc_ref, vn_ref, o_ref = refs[nq:]
    kall = jnp.concatenate([kp_ref[...], kc_ref[...], kn_ref[...]], axis=0)
    vall = jnp.concatenate([vp_ref[...], vc_ref[...], vn_ref[...]], axis=0)
    sink2 = [sink_ref[h] * LOG2E for h in range(A_HEADS)]
    items = [(j, h) for j in range(nq) for h in range(A_HEADS)]
    qrows = lambda j: slice(j * BLOCK, (j + 1) * BLOCK)
    krows = lambda j: slice(j * BLOCK, (j + 3) * BLOCK)
    kcol = lambda h: slice((h // A_GROUP) * LANE, (h // A_GROUP + 1) * LANE)
    vcol = lambda h: slice((2 * (h // A_GROUP) + h % 2) * LANE, (2 * (h // A_GROUP) + h % 2 + 1) * LANE)
    s = [_dot_nt(q_ref[qrows(j), h * LANE:(h + 1) * LANE], kall[krows(j), kcol(h)]) for j, h in items]
    s = [sv + bias_refs[j][h] for (j, h), sv in zip(items, s)]
    m = [jnp.maximum(jnp.max(sv, axis=-1, keepdims=True), sink2[h]) for (_, h), sv in zip(items, s)]
    p = _each(lambda sv, mv: jnp.exp2(sv - mv), s, m)
    rden = [1.0 / (jnp.sum(pv_, axis=-1, keepdims=True) + jnp.exp2(sink2[h] - mv))
            for (_, h), pv_, mv in zip(items, p, m)]
    pv = {it: _dot(pp.astype(BF16), vall[krows(it[0]), vcol(it[1])]) * rd for it, pp, rd in zip(items, p, rden)}
    for j in range(nq):
        for pair in range(A_HEADS // 2):
            o_ref[qrows(j), pair * LANE:(pair + 1) * LANE] = (pv[(j, 2 * pair)] + pv[(j, 2 * pair + 1)]).astype(o_ref.dtype)


def _window_bias():
    qi = np.arange(BLOCK)[:, None]
    kj = np.arange(3 * BLOCK)[None, :]
    dist = np.abs(qi + BLOCK - kj)
    slopes = np.array([2.0 ** (-8.0 * (h + 1) / A_HEADS) for h in range(A_HEADS)], np.float32)
    base = -slopes[:, None, None] * dist[None].astype(np.float32)
    out = []
    for first in (0, 1):
        for last in (0, 1):
            valid = (dist <= WINDOW) & ((kj >= BLOCK) | (first == 0)) & ((kj < 2 * BLOCK) | (last == 0))
            out.append(np.where(valid[None], base, -np.inf))
    return jnp.asarray(np.stack(out) * LOG2E, F32)


def window_attention(q, k, v, sink, batch, seq):
    nb = seq // BLOCK
    nq = 2 if nb % 2 == 0 else 1
    ns = nb // nq
    cur = lambda b, n: (b * ns + n, 0)
    prev = lambda b, n: (b * nb + jnp.maximum(n * nq - 1, 0), 0)
    nxt = lambda b, n: (b * nb + jnp.minimum((n + 1) * nq, nb - 1), 0)

    def variant(j):
        first = lambda n: (n == 0).astype(jnp.int32) if j == 0 else 0
        last = lambda n: (n == ns - 1).astype(jnp.int32) if j == nq - 1 else 0
        return lambda b, n: (2 * first(n) + last(n), 0, 0, 0)

    kw, vw = k.shape[1], v.shape[1]
    bias = _window_bias()
    return pl.pallas_call(
        functools.partial(_win_attn_kernel, nq=nq),
        grid=(batch, ns),
        in_specs=([pl.BlockSpec(memory_space=pltpu.SMEM)]
                  + [pl.BlockSpec((None, A_HEADS, BLOCK, 3 * BLOCK), variant(j)) for j in range(nq)]
                  + [pl.BlockSpec((nq * BLOCK, q.shape[1]), cur),
                     pl.BlockSpec((BLOCK, kw), prev), pl.BlockSpec((nq * BLOCK, kw), cur), pl.BlockSpec((BLOCK, kw), nxt),
                     pl.BlockSpec((BLOCK, vw), prev), pl.BlockSpec((nq * BLOCK, vw), cur),
                     pl.BlockSpec((BLOCK, vw), nxt)]),
        out_specs=pl.BlockSpec((nq * BLOCK, A_HEADS * HEAD_DIM), cur),
        out_shape=jax.ShapeDtypeStruct((batch * seq, A_HEADS * HEAD_DIM), BF16),
        compiler_params=_cparams("parallel", "parallel"),
        name="window_attention",
    )(sink, *([bias] * nq), q, k, k, k, v, v, v)


def _attn_in_kernel(x_ref, nw_ref, w_ref, qnw_ref, wq_ref, kvnw_ref, wk_ref, wv_ref, qtab_ref, ctab_ref, stab_ref,
                    qa_ref, ka_ref, va_ref, q_ref, k_ref, v_ref, *, segs):
    xn = _rms(x_ref[...], nw_ref[...]).astype(BF16)
    proj = [_dot(xn, w_ref[:, start:start + width]) for start, width in segs]
    qa_ref[...] = (proj[0] * WINDOW_Q_SCALE).astype(qa_ref.dtype)
    ka_ref[...] = proj[1].astype(ka_ref.dtype)
    va_ref[...] = proj[2].astype(va_ref.dtype)
    cq, ckv, kab = proj[3:]
    xq = _rms(cq, qnw_ref[...]).astype(BF16)
    tab = qtab_ref[...]
    for h in range(B_HEADS):
        cols = slice(h * LANE, (h + 1) * LANE)
        q_ref[:, cols] = (_dot(xq, wq_ref[:, cols]) * tab).astype(q_ref.dtype)
    xkv = _rms(ckv, kvnw_ref[...]).astype(BF16)
    kr = kab[:, :LANE] * ctab_ref[...] + kab[:, LANE:] * stab_ref[...]
    for h in range(B_HEADS):
        cols = slice(h * LANE, (h + 1) * LANE)
        k_ref[:, cols] = (_dot(xkv, wk_ref[:, cols]) + kr).astype(k_ref.dtype)
    lane = lax.broadcasted_iota(jnp.int32, v_ref.shape, 1) % LANE
    v_ref[...] = jnp.where(lane == B_VDIM, 1.0, _dot(xkv, wv_ref[...])).astype(v_ref.dtype)


def attn_in_proj(x, norm_w, w, segs, q_norm_w, wq, kv_norm_w, wk, wv, q_tab, k_cos, k_sin, seq, tm):
    t, k = x.shape
    nt = seq // tm
    row = lambda i: (i, 0)
    pos = lambda i: (i % nt, 0)
    widths = [wd for _, wd in segs[:3]] + [wq.shape[1], wk.shape[1], wv.shape[1]]
    return pl.pallas_call(
        functools.partial(_attn_in_kernel, segs=tuple(segs)),
        grid=(t // tm,),
        in_specs=[pl.BlockSpec((tm, k), row), _resident((1, k)), _resident(w.shape),
                  _resident((1, wq.shape[0])), _resident(wq.shape),
                  _resident((1, wk.shape[0])), _resident(wk.shape), _resident(wv.shape),
                  pl.BlockSpec((tm, LANE), pos), pl.BlockSpec((tm, LANE), pos), pl.BlockSpec((tm, LANE), pos)],
        out_specs=[pl.BlockSpec((tm, wd), row) for wd in widths],
        out_shape=[jax.ShapeDtypeStruct((t, wd), BF16) for wd in widths],
        compiler_params=_cparams("parallel"),
        name="attn_in_proj",
    )(x, norm_w.reshape(1, k), w, q_norm_w.reshape(1, -1), wq, kv_norm_w.reshape(1, -1), wk, wv,
      q_tab, k_cos, k_sin)


def _mla_flash_kernel(q_ref, k_ref, v_ref, o_ref, s_ref, p_ref, m_ref, a_ref, acc_ref, *, tk, nk, rb):
    tq = q_ref.shape[0]
    heads = range(2)
    m_ref[...] = jnp.full(m_ref.shape, -jnp.inf, F32)
    acc_ref[...] = jnp.zeros(acc_ref.shape, F32)

    def body(j, carry):
        rows = pl.ds(pl.multiple_of(j * tk, tk), tk)
        for e in heads:
            cols = slice(e * LANE, (e + 1) * LANE)
            s_ref[e] = _dot_nt(q_ref[:, cols], k_ref[rows, cols])
        for e in heads:
            cols = slice(e * LANE, (e + 1) * LANE)
            for r in range(tq // rb):
                rs = slice(r * rb, (r + 1) * rb)
                s = s_ref[e, rs, :]
                m_old = m_ref[e, rs, :]
                m_new = jnp.maximum(m_old, jnp.max(s, axis=-1, keepdims=True))
                m_ref[e, rs, :] = m_new
                a_ref[e, rs, :] = jnp.exp2(m_old - m_new)
                p_ref[e, rs, :] = jnp.exp2(s - jnp.concatenate([m_new] * (tk // LANE), axis=1)).astype(BF16)
            acc_ref[e] = a_ref[e] * acc_ref[e] + _dot(p_ref[e], v_ref[rows, cols])
        return carry

    lax.fori_loop(0, nk, body, 0, unroll=True)
    outs = []
    for e in heads:
        acc = acc_ref[e]
        outs.append(acc / acc[:, B_VDIM:B_VDIM + 1])
    lane = lax.broadcasted_iota(jnp.int32, (tq, LANE), 1)
    o_ref[...] = jnp.where(lane < B_VDIM, outs[0], pltpu.roll(outs[1], B_VDIM, axis=1)).astype(o_ref.dtype)


def mla_flash(q, k, v, batch, seq, tq, tk, rb):
    nq = seq // tq
    npair = B_HEADS // 2
    return pl.pallas_call(
        functools.partial(_mla_flash_kernel, tk=tk, nk=seq // tk, rb=rb),
        grid=(batch, npair, nq),
        in_specs=[pl.BlockSpec((tq, 2 * LANE), lambda b, p, i: (b * nq + i, p)),
                  pl.BlockSpec((seq, 2 * LANE), lambda b, p, i: (b, p)),
                  pl.BlockSpec((seq, 2 * LANE), lambda b, p, i: (b, p))],
        out_specs=pl.BlockSpec((tq, LANE), lambda b, p, i: (b * nq + i, p)),
        out_shape=jax.ShapeDtypeStruct((batch * seq, B_HEADS * B_VDIM), BF16),
        scratch_shapes=[pltpu.VMEM((2, tq, tk), F32), pltpu.VMEM((2, tq, tk), BF16), pltpu.VMEM((2, tq, LANE), F32),
                        pltpu.VMEM((2, tq, LANE), F32), pltpu.VMEM((2, tq, LANE), F32)],
        compiler_params=_cparams("parallel", "parallel", "parallel"),
        name="mla_flash",
    )(q, k, v)


def _mlp_tail(x, nw_ref, w1_ref, w2_ref, fw_ref, o_ref, fchunk, final_norm):
    xn = _rms(x, nw_ref[...]).astype(BF16)
    acc = x
    for f in range(w1_ref.shape[1] // fchunk):
        cols = slice(f * fchunk, (f + 1) * fchunk)
        h = jnp.square(jnp.maximum(_dot(xn, w1_ref[:, cols]), 0.0)).astype(BF16)
        acc = acc + _dot(h, w2_ref[cols, :])
    if final_norm:
        acc = _rms(acc, fw_ref[...])
    o_ref[...] = acc


def _attn_out_mlp_kernel(x_ref, a_ref, b_ref, wo_ref, nw_ref, w1_ref, w2_ref, fw_ref, o_ref, *, fchunk, final_norm):
    ka = a_ref.shape[1]
    o_ref[...] = x_ref[...] + _dot(a_ref[...], wo_ref[:ka, :]) + _dot(b_ref[...], wo_ref[ka:, :])
    _mlp_tail(o_ref[...], nw_ref, w1_ref, w2_ref, fw_ref, o_ref, fchunk, final_norm)


def _lin_out_mlp_kernel(x_ref, cf_ref, cb_ref, df_ref, db_ref, gc_ref, gd_ref, cw_ref, dw_ref, wo_ref,
                        nw_ref, w1_ref, w2_ref, fw_ref, o_ref, *, fchunk, final_norm):
    def gated(f_ref, b_ref, gate_ref, gnw_ref):
        o = f_ref[...] + b_ref[...]
        parts = []
        for h in range(LIN_HEADS):
            cols = slice(h * LIN_DV, (h + 1) * LIN_DV)
            gate = gate_ref[:, cols].astype(F32)
            parts.append((_rms(o[:, cols], gnw_ref[...]) * (gate * _sigmoid(gate))).astype(BF16))
        return jnp.concatenate(parts, axis=1)

    kc = cf_ref.shape[1]
    o_ref[...] = (x_ref[...] + _dot(gated(cf_ref, cb_ref, gc_ref, cw_ref), wo_ref[:kc, :])
                  + _dot(gated(df_ref, db_ref, gd_ref, dw_ref), wo_ref[kc:, :]))
    _mlp_tail(o_ref[...], nw_ref, w1_ref, w2_ref, fw_ref, o_ref, fchunk, final_norm)


def mixer_out_mlp(x, mixer_inputs, small_params, w_out, norm_w, w1, w2, final_w, final_norm, tm, fchunk):
    t, d = x.shape
    row = lambda i: (i, 0)
    body = _attn_out_mlp_kernel if len(mixer_inputs) == 2 else _lin_out_mlp_kernel
    return pl.pallas_call(
        functools.partial(body, fchunk=fchunk, final_norm=final_norm),
        grid=(t // tm,),
        in_specs=([pl.BlockSpec((tm, d), row)] + [pl.BlockSpec((tm, a.shape[1]), row) for a in mixer_inputs]
                  + [_resident(p.shape) for p in small_params]
                  + [_resident(w_out.shape), _resident((1, d)), _resident(w1.shape), _resident(w2.shape),
                     _resident((1, d))]),
        out_specs=pl.BlockSpec((tm, d), row),
        out_shape=jax.ShapeDtypeStruct((t, d), F32),
        compiler_params=_cparams("parallel"),
        name="mixer_out_mlp",
    )(x, *mixer_inputs, *small_params, w_out, norm_w.reshape(1, d), w1, w2, final_w.reshape(1, d))


def _direction_masks(d):
    sgn = 1 - 2 * d
    r = lax.broadcasted_iota(jnp.int32, (STACK, STACK), 0)
    c = lax.broadcasted_iota(jnp.int32, (STACK, STACK), 1)
    same = (r & -CHUNK) == (c & -CHUNK)
    order = ((r & (CHUNK - 1)) - (c & (CHUNK - 1))) * sgn
    i = lax.broadcasted_iota(jnp.int32, (CHUNK, CHUNK), 0)
    j = lax.broadcasted_iota(jnp.int32, (CHUNK, CHUNK), 1)
    cum = (((i - j) * sgn) >= 0).astype(F32)
    return same, same & (order >= 0), same & (order > 0), cum


def _block_diag(x, same):
    return jnp.where(same, jnp.concatenate([x] * LIN_HEADS, axis=0), 0.0)


def _block_diag_cols(x, same):
    return jnp.where(same, jnp.concatenate([x] * LIN_HEADS, axis=1), 0.0)


def _stack_heads(v, width):
    return jnp.concatenate([v[:, h * width:(h + 1) * width] for h in range(LIN_HEADS)], axis=0)


def _each(f, *lists):
    return [f(*args) for args in zip(*lists)]


def _scan_order(ncb):
    return [((0, i), (1, ncb - 1 - i)) for i in range(ncb)]


def _gla_kernel(qf_ref, kf_ref, vf_ref, sf_ref, qb_ref, kb_ref, vb_ref, sb_ref, wg_ref, bg_ref,
                of_ref, ob_ref, st_ref, *, ncb, nb):
    @pl.when(pl.program_id(1) == 0)
    def _():
        st_ref[...] = jnp.zeros_like(st_ref)

    refs = ((qf_ref, kf_ref, vf_ref, sf_ref, of_ref), (qb_ref, kb_ref, vb_ref, sb_ref, ob_ref))
    masks = [_direction_masks(d) for d in range(2)]
    items = [(bi, d, ci) for bi in range(nb) for d in range(2) for ci in range(ncb)]
    rows = lambda ci: slice(ci * CHUNK, (ci + 1) * CHUNK)
    chunk = lambda which, it: refs[it[1]][which][it[0], rows(it[2]), :]
    same = [masks[d][0] for _, d, _ in items]
    z = [_dot(chunk(3, it).astype(BF16), wg_ref[it[1]]) + bg_ref[it[1]] for it in items]
    log_a = _each(lambda a: (jnp.minimum(a, 0.0) - jnp.log1p(jnp.exp(-jnp.abs(a)))) * (1.0 / GATE_NORM), z)
    b = [_dot_mask_f32(masks[it[1]][3].astype(BF16), a) for it, a in zip(items, log_a)]
    b_last = _each(lambda a: jnp.sum(a, axis=0, keepdims=True), log_a)
    q_in = [_block_diag(chunk(0, it) * (LIN_DK ** -0.5) * jnp.exp(bb), sm).astype(BF16)
            for it, bb, sm in zip(items, b, same)]
    k = [chunk(1, it) for it in items]
    k_in = _each(lambda a, bb, sm: _block_diag(a * jnp.exp(-bb), sm).astype(BF16), k, b, same)
    k_out = _each(lambda a, bb, bl, sm: _block_diag(a * jnp.exp(bl - bb), sm).astype(BF16), k, b, b_last, same)
    v = [_stack_heads(chunk(2, it), LIN_DV).astype(BF16) for it in items]
    sc = [jnp.where(masks[it[1]][1], _dot_nt(a, bb), 0.0).astype(BF16) for it, a, bb in zip(items, q_in, k_in)]
    o_intra = dict(zip(items, _each(_dot, sc, v)))
    ds = dict(zip(items, _each(_dot_tn, v, k_out)))
    dec = dict(zip(items, _each(jnp.exp, b_last)))
    q_in = dict(zip(items, q_in))

    streams = [(bi, d) for bi in range(nb) for d in range(2)]
    st = {sd: st_ref[i] for i, sd in enumerate(streams)}
    for fwd_bwd in _scan_order(ncb):
        step = [(bi, d, ci) for bi in range(nb) for d, ci in fwd_bwd]
        o = [o_intra[it] + _dot_nt(q_in[it], st[it[:2]].astype(BF16)) for it in step]
        for it, od in zip(step, o):
            bi, d, ci = it
            st[(bi, d)] = dec[it] * st[(bi, d)] + ds[it]
            for h in range(LIN_HEADS):
                refs[d][4][bi, rows(ci), h * LIN_DV:(h + 1) * LIN_DV] = od[h * CHUNK:(h + 1) * CHUNK, :]
    for i, sd in enumerate(streams):
        st_ref[i] = st[sd]


def _bidir_call(body, name, arrays, extra, extra_specs, batch, seq, tc, nb, hk, hv):
    nblk = seq // tc
    ins = [a.reshape(batch, seq, a.shape[1]) for a in arrays]
    fwd = lambda bb, n: (bb, n, 0)
    bwd = lambda bb, n: (bb, nblk - 1 - n, 0)
    spec = lambda a, idx: pl.BlockSpec((nb, tc, a.shape[2]), idx)
    o_f, o_b = pl.pallas_call(
        functools.partial(body, ncb=tc // CHUNK, nb=nb),
        grid=(batch // nb, nblk),
        in_specs=[spec(a, fwd) for a in ins] + [spec(a, bwd) for a in ins] + extra_specs,
        out_specs=[pl.BlockSpec((nb, tc, hv), fwd), pl.BlockSpec((nb, tc, hv), bwd)],
        out_shape=[jax.ShapeDtypeStruct((batch, seq, hv), F32)] * 2,
        scratch_shapes=[pltpu.VMEM((2 * nb, LIN_DV, hk), F32)],
        compiler_params=_cparams("parallel", "arbitrary"),
        name=name,
    )(*ins, *ins, *extra)
    return o_f.reshape(batch * seq, hv), o_b.reshape(batch * seq, hv)


def gla_scan(q, k, v, small, wg, bg, batch, seq, tc, nb):
    return _bidir_call(_gla_kernel, "gla_scan", (q, k, v, small), (wg, bg),
                       [_resident(wg.shape), _resident(bg.shape)], batch, seq, tc, nb, q.shape[1], v.shape[1])


def _gdn_prep_kernel(xp_ref, xc_ref, xn_ref, sm_ref, cw_ref, gsum_ref, alog_ref, dtb_ref,
                     q_ref, k_ref, v_ref, g_ref, b_ref, bt_ref, buf_ref, *, nt):
    i = pl.program_id(1)
    tm = xc_ref.shape[0]
    halo = xp_ref.shape[0]
    buf_ref[:halo, :] = xp_ref[...].astype(F32) * (i > 0).astype(F32)
    buf_ref[halo:halo + tm, :] = xc_ref[...].astype(F32)
    buf_ref[halo + tm:, :] = xn_ref[...].astype(F32) * (i < nt - 1).astype(F32)
    first = halo - CONV_TAPS // 2
    acc = buf_ref[first:first + tm, :] * cw_ref[0:1, :]
    for j in range(1, CONV_TAPS):
        acc = acc + buf_ref[first + j:first + j + tm, :] * cw_ref[j:j + 1, :]
    y = acc * _sigmoid(acc)
    nqk = LIN_HEADS * LIN_DK
    for idx, (o_ref, scale) in enumerate(((q_ref, LIN_DK ** -0.5), (k_ref, 1.0))):
        part = y[:, idx * nqk:(idx + 1) * nqk]
        ssq = _dot_f32_mask(part * part, gsum_ref[...])
        o_ref[...] = (part * lax.rsqrt(ssq + EPS) * scale).astype(o_ref.dtype)
    v_ref[...] = y[:, 2 * nqk:].astype(v_ref.dtype)
    sm = sm_ref[...]
    beta = _sigmoid(sm)
    g = -jnp.exp(alog_ref[...]) * _softplus(sm + dtb_ref[...])
    lane = lax.broadcasted_iota(jnp.int32, sm.shape, 1) - GATE_LANE0
    is_beta = (lane >= 0) & (lane < 2 * LIN_HEADS)
    is_g = (lane >= 2 * LIN_HEADS) & (lane < 4 * LIN_HEADS)
    g = jnp.where(is_g, g, 0.0)
    g_ref[...] = jnp.where(is_beta, beta, g)
    ri = lax.broadcasted_iota(jnp.int32, (CHUNK, CHUNK), 0)
    ci = lax.broadcasted_iota(jnp.int32, (CHUNK, CHUNK), 1)
    tril16 = (ri >= ci).astype(BF16)
    ones16 = jnp.ones((CHUNK, CHUNK), BF16)
    is_bwd = lax.broadcasted_iota(jnp.int32, (CHUNK, LANE), 1) >= GATE_LANE0 + 3 * LIN_HEADS
    for c in range(tm // CHUNK):
        rows = slice(c * CHUNK, (c + 1) * CHUNK)
        gc = g[rows, :]
        prefix = _dot_mask_f32(tril16, gc)
        total = _dot_mask_f32(ones16, gc)
        b_ref[rows, :] = jnp.where(is_bwd, total - prefix + gc, prefix)
        bt_ref[rows, :] = total


def gdn_prep(dqkv, small, conv_w, gsum, alog, dtb, batch, seq, tm):
    nt = seq // tm
    t, c = dqkv.shape
    halo = 2 * SUBLANE
    hb = tm // halo
    nqk = LIN_HEADS * LIN_DK
    row = lambda b, i: (b * nt + i, 0)
    prev = lambda b, i: (jnp.maximum((b * nt + i) * hb - 1, 0), 0)
    nxt = lambda b, i: (jnp.minimum((b * nt + i + 1) * hb, t // halo - 1), 0)
    return pl.pallas_call(
        functools.partial(_gdn_prep_kernel, nt=nt),
        grid=(batch, nt),
        in_specs=[pl.BlockSpec((halo, c), prev), pl.BlockSpec((tm, c), row), pl.BlockSpec((halo, c), nxt),
                  pl.BlockSpec((tm, LANE), row),
                  _resident(conv_w.shape), _resident(gsum.shape), _resident((1, LANE)), _resident((1, LANE))],
        out_specs=[pl.BlockSpec((tm, nqk), row), pl.BlockSpec((tm, nqk), row),
                   pl.BlockSpec((tm, c - 2 * nqk), row)] + [pl.BlockSpec((tm, LANE), row)] * 3,
        out_shape=[jax.ShapeDtypeStruct((t, nqk), BF16), jax.ShapeDtypeStruct((t, nqk), BF16),
                   jax.ShapeDtypeStruct((t, c - 2 * nqk), BF16)] + [jax.ShapeDtypeStruct((t, LANE), F32)] * 3,
        scratch_shapes=[pltpu.VMEM((tm + 2 * halo, c), F32)],
        compiler_params=_cparams("parallel", "parallel"),
        name="gdn_prep",
    )(dqkv, dqkv, dqkv, small, conv_w, gsum, alog, dtb)


def _gdn_kernel(qf_ref, kf_ref, vf_ref, gf_ref, bf_ref, tf_ref, qb_ref, kb_ref, vb_ref, gb_ref, bb_ref, tb_ref,
                of_ref, ob_ref, st_ref, *, ncb, nb):
    @pl.when(pl.program_id(1) == 0)
    def _():
        st_ref[...] = jnp.zeros_like(st_ref)

    refs = ((qf_ref, kf_ref, vf_ref, gf_ref, bf_ref, tf_ref, of_ref),
            (qb_ref, kb_ref, vb_ref, gb_ref, bb_ref, tb_ref, ob_ref))
    masks = [_direction_masks(d) for d in range(2)]
    r = lax.broadcasted_iota(jnp.int32, (STACK, STACK), 0)
    c = lax.broadcasted_iota(jnp.int32, (STACK, STACK), 1)
    diag = r == c
    eye = diag.astype(F32)
    lane_blk = lax.broadcasted_iota(jnp.int32, (SUBLANE, STACK), 1) // CHUNK

    def rows_of(x, lane0):
        return jnp.concatenate(
            [jnp.broadcast_to(x[:, lane0 + h:lane0 + h + 1], (CHUNK, STACK)) for h in range(LIN_HEADS)], axis=0)

    def head_lanes(row, lane0):
        out = jnp.broadcast_to(row[:, lane0:lane0 + 1], (SUBLANE, STACK))
        for h in range(1, LIN_HEADS):
            out = jnp.where(lane_blk == h, jnp.broadcast_to(row[:, lane0 + h:lane0 + h + 1], (SUBLANE, STACK)), out)
        return out

    each = _each
    items = [(bi, d, ci) for bi in range(nb) for d in range(2) for ci in range(ncb)]
    rows = lambda ci: slice(ci * CHUNK, (ci + 1) * CHUNK)
    chunk = lambda which, it: refs[it[1]][which][it[0], rows(it[2]), :]
    same = [masks[d][0] for _, d, _ in items]
    beta_col = [rows_of(chunk(3, it), _gate_lanes(it[1])[0]) for it in items]
    b_col = [rows_of(chunk(4, it), _gate_lanes(it[1])[1]) for it in items]
    bl_col = [rows_of(chunk(5, it), _gate_lanes(it[1])[1]) for it in items]
    k_bd = [_block_diag(chunk(1, it), sm) for it, sm in zip(items, same)]
    q_bd = [_block_diag(chunk(0, it), sm) for it, sm in zip(items, same)]
    kb_bd = each(lambda a, b: a * b, k_bd, beta_col)
    k16 = each(lambda a: a.astype(BF16), k_bd)
    kk = each(lambda a, b: _dot_nt(a.astype(BF16), b), kb_bd, k16)
    qk = each(lambda a, b: _dot_nt(a.astype(BF16), b), q_bd, k16)
    b_row = each(lambda bc: jnp.sum(jnp.where(diag, bc, 0.0), axis=0, keepdims=True), b_col)
    gamma = [jnp.exp(jnp.where(masks[it[1]][1], bc - br, -jnp.inf)) for it, bc, br in zip(items, b_col, b_row)]
    lower = [jnp.where(masks[it[1]][2], a * g, 0.0) for it, a, g in zip(items, kk, gamma)]
    attn = each(lambda a, g: (a * g).astype(BF16), qk, gamma)
    l_hi = each(lambda a: a.astype(BF16), lower)
    pw = [l_hi]
    for _ in range(5):
        pw.append(each(lambda a: _dot(a, a).astype(BF16), pw[-1]))

    def pair(a, b16):
        return a + _dot(a.astype(BF16), b16)

    f01 = each(lambda lo, p1: pair(eye - lo, p1), lower, pw[1])
    f23 = each(lambda p2, p3: pair(eye + p2.astype(F32), p3), pw[2], pw[3])
    f45 = each(lambda p4, p5: pair(eye + p4.astype(F32), p5), pw[4], pw[5])
    f0123 = each(lambda a, b: _dot(a.astype(BF16), b.astype(BF16)).astype(BF16), f01, f23)
    inv16 = each(lambda a, b: _dot(a, b.astype(BF16)).astype(BF16), f0123, f45)
    e_b = each(jnp.exp, b_col)
    vb = [_stack_heads(chunk(2, it), LIN_DV) * bc[:, :LIN_DV] for it, bc in zip(items, beta_col)]
    kbe = [_stack_heads(chunk(1, it), LIN_DK) * (bc * e)[:, :LIN_DK] for it, bc, e in zip(items, beta_col, e_b)]
    rhs = each(lambda v, kb: jnp.concatenate([v, kb], axis=1), vb, kbe)
    x16 = each(lambda m, r_: _dot(m, r_.astype(BF16)).astype(BF16), inv16, rhs)
    l_lo = each(lambda lo, hi: (lo - hi.astype(F32)).astype(BF16), lower, l_hi)
    resid = each(lambda r_, x, hi, lo: r_ - (x.astype(F32) + _dot(hi, x) + _dot(lo, x)), rhs, x16, l_hi, l_lo)
    sol = each(lambda x, m, rs: x.astype(F32) + _dot(m, rs.astype(BF16)), x16, inv16, resid)
    u = {it: sol[i][:, :LIN_DV] for i, it in enumerate(items)}
    w = {it: _block_diag_cols(sol[i][:, LIN_DV:].astype(BF16), same[i])
         for i, it in enumerate(items)}
    attn = dict(zip(items, attn))
    q_dec = {it: (q_bd[i] * e_b[i]).astype(BF16) for i, it in enumerate(items)}
    k_dec = {it: (k_bd[i] * jnp.exp(bl_col[i] - b_col[i])).astype(BF16) for i, it in enumerate(items)}
    dec = {(bi, d, ci): jnp.exp(head_lanes(refs[d][5][bi, ci * CHUNK:ci * CHUNK + SUBLANE, :],
                                           _gate_lanes(d)[1]))[:1, :] for bi, d, ci in items}

    streams = [(bi, d) for bi in range(nb) for d in range(2)]
    st = {sd: st_ref[i] for i, sd in enumerate(streams)}
    for fwd_bwd in _scan_order(ncb):
        step = [(bi, d, ci) for bi in range(nb) for d, ci in fwd_bwd]
        st16 = [st[it[:2]].astype(BF16) for it in step]
        v_new16 = [(u[it] - _dot_nt(w[it], s16)).astype(BF16) for it, s16 in zip(step, st16)]
        o = [_dot_nt(q_dec[it], s16) + _dot(attn[it], vn) for it, s16, vn in zip(step, st16, v_new16)]
        for it, od, vn in zip(step, o, v_new16):
            bi, d, ci = it
            st[(bi, d)] = dec[it] * st[(bi, d)] + _dot_tn(vn, k_dec[it])
            for h in range(LIN_HEADS):
                refs[d][6][bi, rows(ci), h * LIN_DV:(h + 1) * LIN_DV] = od[h * CHUNK:(h + 1) * CHUNK, :]
    for i, sd in enumerate(streams):
        st_ref[i] = st[sd]


def _gate_lanes(d):
    return GATE_LANE0 + d * LIN_HEADS, GATE_LANE0 + (2 + d) * LIN_HEADS


def gdn_scan(q, k, v, mixed, bcum, btot, batch, seq, tc, nb):
    return _bidir_call(_gdn_kernel, "gdn_scan", (q, k, v, mixed, bcum, btot), (), [],
                       batch, seq, tc, nb, q.shape[1], v.shape[1])


def _rot_half_cols(w):
    half = w.shape[-1] // 2
    return jnp.concatenate([-w[..., half:], w[..., :half]], axis=-1)


def _pad_heads(w, heads, dim, offset=0):
    k = w.shape[0]
    out = jnp.zeros((k, heads, LANE), w.dtype)
    out = out.at[:, :, offset:offset + dim].set(w.reshape(k, heads, dim))
    return out.reshape(k, heads * LANE)


def _layer0_in_weight(w_in):
    aq = A_HEADS * HEAD_DIM
    akv = A_KV_HEADS * HEAD_DIM
    o = 0
    w_q = w_in[:, o:o + aq]; o += aq
    w_k = w_in[:, o:o + akv]; o += akv
    w_v = w_in[:, o:o + akv]; o += akv
    w_cq = w_in[:, o:o + B_Q_RANK]; o += B_Q_RANK
    w_ckv = w_in[:, o:o + B_KV_RANK]; o += B_KV_RANK
    w_pe = w_in[:, o:o + B_ROPE]
    zeros = jnp.zeros((w_in.shape[0], B_NOPE), w_in.dtype)
    v_lo = _pad_heads(w_v, A_KV_HEADS, HEAD_DIM, 0).reshape(-1, A_KV_HEADS, LANE)
    v_hi = _pad_heads(w_v, A_KV_HEADS, HEAD_DIM, HEAD_DIM).reshape(-1, A_KV_HEADS, LANE)
    w_v2 = jnp.stack([v_lo, v_hi], axis=2).reshape(w_in.shape[0], 2 * A_KV_HEADS * LANE)
    w_pe_rot = _rot_half_cols(w_pe)
    pieces = [_pad_heads(w_q, A_HEADS, HEAD_DIM), _pad_heads(w_k, A_KV_HEADS, HEAD_DIM), w_v2,
              w_cq, w_ckv,
              jnp.concatenate([zeros, w_pe, w_pe, zeros, w_pe_rot, w_pe_rot], axis=1)]
    segs, start = [], 0
    for p in pieces:
        segs.append((start, p.shape[1]))
        start += p.shape[1]
    return jnp.concatenate(pieces, axis=1).astype(BF16), segs


def _mla_weights(w_uq, w_ukv):
    r = w_uq.shape[0]
    wq = w_uq.reshape(r, B_HEADS, B_NOPE + B_ROPE)
    pe = wq[:, :, B_NOPE:]
    wq = jnp.concatenate([wq[:, :, :B_NOPE], pe, _rot_half_cols(pe)], axis=-1).reshape(r, B_HEADS * LANE)
    rk = w_ukv.shape[0]
    wkv = w_ukv.reshape(rk, B_HEADS, B_NOPE + B_VDIM)
    wk = _pad_heads(wkv[:, :, :B_NOPE].reshape(rk, B_HEADS * B_NOPE), B_HEADS, B_NOPE)
    wv = _pad_heads(wkv[:, :, B_NOPE:].reshape(rk, B_HEADS * B_VDIM), B_HEADS, B_VDIM)
    return wq.astype(BF16), wk.astype(BF16), wv.astype(BF16)


def _rope_tables(seq):
    half = B_ROPE // 2
    inv = ROPE_THETA ** (-jnp.arange(half, dtype=F32) / half)
    ang = jnp.arange(seq, dtype=F32)[:, None] * inv[None, :]
    cos2 = jnp.tile(jnp.cos(ang), (1, 4))
    sin2 = jnp.tile(jnp.sin(ang), (1, 4))
    zeros = jnp.zeros((seq, B_NOPE), F32)
    qscale = (B_NOPE + B_ROPE) ** -0.5 * LOG2E
    q_tab = jnp.concatenate([jnp.ones((seq, B_NOPE), F32), cos2[:, :B_ROPE], sin2[:, :B_ROPE]], axis=1) * qscale
    k_cos = jnp.concatenate([zeros, cos2], axis=1)
    k_sin = jnp.concatenate([zeros, sin2], axis=1)
    return q_tab, k_cos, k_sin


def _layer1_in_weight(w_in):
    hk = LIN_HEADS * LIN_DK
    hv = LIN_HEADS * LIN_DV
    conv_ch = 2 * hk + hv
    o = 2 * hk + 2 * hv
    w_main = w_in[:, :o]
    w_gl = w_in[:, o:o + 2 * GATE_RANK]; o += 2 * GATE_RANK
    w_conv = w_in[:, o:o + conv_ch]; o += conv_ch
    w_z = w_in[:, o:o + hv]; o += hv
    w_small = w_in[:, o:o + 4 * LIN_HEADS]
    pad = jnp.zeros((w_in.shape[0], LANE - 2 * GATE_RANK - 4 * LIN_HEADS), w_in.dtype)
    w = jnp.concatenate([w_main, w_conv, w_z, w_gl, w_small, pad], axis=1).astype(BF16)
    widths = [hk, hk, hv, hv, conv_ch, hv, LANE]
    segs, start = [], 0
    for wd in widths:
        segs.append((start, wd))
        start += wd
    return w, segs


def _pick_tile(n, pref):
    t = min(n, pref)
    while n % t:
        t //= 2
    return t


def kernel(x, att_norm, att_w_in, att_sink, mla_q_norm, mla_w_uq, mla_kv_norm, mla_w_ukv, att_w_out, lin_norm, lin_w_in, gla_w_gate_f, gla_b_gate_f, gla_w_gate_b, gla_b_gate_b, gla_norm, gdn_conv, gdn_a_log_f, gdn_dt_bias_f, gdn_a_log_b, gdn_dt_bias_b, gdn_norm, lin_w_out, mlp_norm, mlp_w1, mlp_w2, final_norm):
    batch, seq, dm = x.shape
    t = batch * seq
    depth = mlp_norm.shape[0]
    tm = _pick_tile(seq, ROW_TILE)
    xs = x.reshape(t, dm)
    for layer in range(depth):
        i = layer // 2
        if layer % 2 == 0:
            w0, segs = _layer0_in_weight(att_w_in[i])
            wq, wk, wv = _mla_weights(mla_w_uq[i], mla_w_ukv[i])
            q_tab, k_cos, k_sin = _rope_tables(seq)
            qa, ka, va, qb, kb, vb = attn_in_proj(xs, att_norm[i], w0, segs, mla_q_norm[i], wq, mla_kv_norm[i],
                                                  wk, wv, q_tab, k_cos, k_sin, seq, tm)
            o_a = window_attention(qa, ka, va, att_sink[i].astype(F32), batch, seq)
            o_b = mla_flash(qb, kb, vb, batch, seq, _pick_tile(seq, FLASH_Q_TILE), _pick_tile(seq, FLASH_KV_TILE),
                            FLASH_ROW_BLOCK)
            mixer_inputs, small_params, w_out = [o_a, o_b], [], att_w_out[i]
        else:
            w1, segs = _layer1_in_weight(lin_w_in[i])
            cq, ck, cv, cg, dqkv, dz, small = norm_proj(xs, lin_norm[i], w1, segs, [BF16] * 6 + [F32], tm)
            hk = LIN_HEADS * LIN_DK
            wg = jnp.zeros((2, LANE, hk), F32)
            wg = wg.at[0, :GATE_RANK].set(gla_w_gate_f[i]).at[1, GATE_RANK:2 * GATE_RANK].set(gla_w_gate_b[i])
            bg = jnp.stack([gla_b_gate_f[i], gla_b_gate_b[i]]).reshape(2, 1, hk).astype(F32)
            tc = _pick_tile(seq, SCAN_ROWS)
            nb = 2 if batch % 2 == 0 else 1
            o_cf, o_cb = gla_scan(cq, ck, cv, small, wg.astype(BF16), bg, batch, seq, _pick_tile(seq, 2 * SCAN_ROWS), nb)
            gsum = (jnp.arange(hk)[:, None] // LIN_DK == jnp.arange(hk)[None, :] // LIN_DK).astype(BF16)
            nh = LIN_HEADS
            a0 = GATE_LANE0 + 2 * nh
            alog = jnp.zeros((LANE,), F32).at[a0:a0 + nh].set(gdn_a_log_f[i]).at[a0 + nh:a0 + 2 * nh].set(gdn_a_log_b[i])
            dtb = jnp.zeros((LANE,), F32).at[a0:a0 + nh].set(gdn_dt_bias_f[i]).at[a0 + nh:a0 + 2 * nh].set(gdn_dt_bias_b[i])
            dq, dk, dv, mixed, bcum, btot = gdn_prep(dqkv, small, gdn_conv[i].astype(F32), gsum,
                                                     alog.reshape(1, LANE), dtb.reshape(1, LANE), batch, seq, tm)
            o_df, o_db = gdn_scan(dq, dk, dv, mixed, bcum, btot, batch, seq, tc, nb)
            mixer_inputs = [o_cf, o_cb, o_df, o_db, cg, dz]
            small_params = [gla_norm[i].reshape(1, LIN_DV), gdn_norm[i].reshape(1, LIN_DV)]
            w_out = lin_w_out[i]
        last = layer == depth - 1
        tm_mlp = _pick_tile(seq, ATTN_MLP_ROW_TILE) if layer % 2 == 0 else tm
        xs = mixer_out_mlp(xs, mixer_inputs, small_params, w_out.astype(BF16), mlp_norm[layer],
                           mlp_w1[layer].astype(BF16), mlp_w2[layer].astype(BF16), final_norm, last, tm_mlp,
                           MLP_F_CHUNK)
    return xs.reshape(batch, seq, dm)
```

```python
import functools
import math

import numpy as np
import jax
import jax.numpy as jnp
from jax import lax
from jax.experimental import pallas as pl
from jax.experimental.pallas import tpu as pltpu

F32 = jnp.float32
BF16 = jnp.bfloat16
EPS = 1e-6

LANE = 128
SUBLANE = 8
VMEM_LIMIT_BYTES = 56 * 1024 * 1024

ROW_TILE = 512
PROJ_ROW_TILE = 1024
ATTN_MLP_ROW_TILE = 1024
MLP_F_CHUNK = 1024
FLASH_Q_TILE = 1024
FLASH_KV_TILE = 2048
FLASH_ROW_BLOCK = 64
SCAN_ROWS = 256

HEAD_DIM = 64
A_HEADS = 8
A_KV_HEADS = 2
A_GROUP = A_HEADS // A_KV_HEADS
WINDOW = 128
BLOCK = 128
B_HEADS = 8
B_NOPE = 64
B_ROPE = 32
B_VDIM = 64
B_Q_RANK = 384
B_KV_RANK = 256
ROPE_THETA = 10000.0
LIN_HEADS = 4
LIN_DK = 64
LIN_DV = 128
GATE_RANK = 16
GATE_NORM = 16.0
CHUNK = 64
CONV_TAPS = 5
STACK = LIN_HEADS * CHUNK
GATE_LANE0 = 2 * GATE_RANK

LOG2E = math.log2(math.e)
WINDOW_Q_SCALE = HEAD_DIM ** -0.5 * LOG2E

NT_DIMS = (((1,), (1,)), ((), ()))
TN_DIMS = (((0,), (0,)), ((), ()))


def _cparams(*sem):
    return pltpu.CompilerParams(dimension_semantics=sem, vmem_limit_bytes=VMEM_LIMIT_BYTES)


def _resident(shape):
    nd = len(shape)
    return pl.BlockSpec(shape, lambda *_: (0,) * nd, pipeline_mode=pl.Buffered(1))


def _rms(x, w):
    return x * lax.rsqrt(jnp.mean(x * x, axis=-1, keepdims=True) + EPS) * w


def _sigmoid(x):
    return 1.0 / (1.0 + jnp.exp(-x))


def _softplus(x):
    return jnp.maximum(x, 0.0) + jnp.log1p(jnp.exp(-jnp.abs(x)))


def _dot(a, b):
    return jnp.dot(a, b, preferred_element_type=F32)


def _dot_nt(a, b):
    return lax.dot_general(a, b, NT_DIMS, preferred_element_type=F32)


def _dot_tn(a, b):
    return lax.dot_general(a, b, TN_DIMS, preferred_element_type=F32)


def _split3(x):
    hi = x.astype(BF16)
    r1 = x - hi.astype(F32)
    mid = r1.astype(BF16)
    return hi, mid, (r1 - mid.astype(F32)).astype(BF16)


def _dot_mask_f32(mask16, x):
    hi, mid, lo = _split3(x)
    return _dot(mask16, hi) + _dot(mask16, mid) + _dot(mask16, lo)


def _dot_f32_mask(x, mask16):
    hi, mid, lo = _split3(x)
    return _dot(hi, mask16) + _dot(mid, mask16) + _dot(lo, mask16)


def _norm_proj_kernel(x_ref, nw_ref, w_ref, *out_refs, segs):
    xn = _rms(x_ref[...], nw_ref[...]).astype(BF16)
    for (start, width), o_ref in zip(segs, out_refs):
        o_ref[...] = _dot(xn, w_ref[:, start:start + width]).astype(o_ref.dtype)


def norm_proj(x, norm_w, w, segs, dtypes, tm):
    t, k = x.shape
    n = w.shape[1]
    return pl.pallas_call(
        functools.partial(_norm_proj_kernel, segs=tuple(segs)),
        grid=(t // tm,),
        in_specs=[pl.BlockSpec((tm, k), lambda i: (i, 0)),
                  _resident((1, k)),
                  _resident((k, n))],
        out_specs=[pl.BlockSpec((tm, wd), lambda i: (i, 0)) for _, wd in segs],
        out_shape=[jax.ShapeDtypeStruct((t, wd), dt) for (_, wd), dt in zip(segs, dtypes)],
        compiler_params=_cparams("parallel"),
        name="norm_proj",
    )(x, norm_w.reshape(1, k), w)


def _win_attn_kernel(sink_ref, *refs, nq):
    bias_refs = refs[:nq]
    q_ref, kp_ref, kc_ref, kn_ref, vp_ref, vc_ref, vn_ref, o_ref = refs[nq:]
    kall = jnp.concatenate([kp_ref[...], kc_ref[...], kn_ref[...]], axis=0)
    vall = jnp.concatenate([vp_ref[...], vc_ref[...], vn_ref[...]], axis=0)
    sink2 = [sink_ref[h] * LOG2E for h in range(A_HEADS)]
    items = [(j, h) for j in range(nq) for h in range(A_HEADS)]
    qrows = lambda j: slice(j * BLOCK, (j + 1) * BLOCK)
    krows = lambda j: slice(j * BLOCK, (j + 3) * BLOCK)
    kcol = lambda h: slice((h // A_GROUP) * LANE, (h // A_GROUP + 1) * LANE)
    vcol = lambda h: slice((2 * (h // A_GROUP) + h % 2) * LANE, (2 * (h // A_GROUP) + h % 2 + 1) * LANE)
    s = [_dot_nt(q_ref[qrows(j), h * LANE:(h + 1) * LANE], kall[krows(j), kcol(h)]) for j, h in items]
    s = [sv + bias_refs[j][h] for (j, h), sv in zip(items, s)]
    m = [jnp.maximum(jnp.max(sv, axis=-1, keepdims=True), sink2[h]) for (_, h), sv in zip(items, s)]
    p = _each(lambda sv, mv: jnp.exp2(sv - mv), s, m)
    rden = [1.0 / (jnp.sum(pv_, axis=-1, keepdims=True) + jnp.exp2(sink2[h] - mv))
            for (_, h), pv_, mv in zip(items, p, m)]
    pv = {it: _dot(pp.astype(BF16), vall[krows(it[0]), vcol(it[1])]) * rd for it, pp, rd in zip(items, p, rden)}
    for j in range(nq):
        for pair in range(A_HEADS // 2):
            o_ref[qrows(j), pair * LANE:(pair + 1) * LANE] = (pv[(j, 2 * pair)] + pv[(j, 2 * pair + 1)]).astype(o_ref.dtype)


def _window_bias():
    qi = np.arange(BLOCK)[:, None]
    kj = np.arange(3 * BLOCK)[None, :]
    dist = np.abs(qi + BLOCK - kj)
    slopes = np.array([2.0 ** (-8.0 * (h + 1) / A_HEADS) for h in range(A_HEADS)], np.float32)
    base = -slopes[:, None, None] * dist[None].astype(np.float32)
    out = []
    for first in (0, 1):
        for last in (0, 1):
            valid = (dist <= WINDOW) & ((kj >= BLOCK) | (first == 0)) & ((kj < 2 * BLOCK) | (last == 0))
            out.append(np.where(valid[None], base, -np.inf))
    return jnp.asarray(np.stack(out) * LOG2E, F32)


def window_attention(q, k, v, sink, batch, seq):
    nb = seq // BLOCK
    nq = next(n for n in (4, 2, 1) if nb % n == 0)
    ns = nb // nq
    cur = lambda b, n: (b * ns + n, 0)
    prev = lambda b, n: (b * nb + jnp.maximum(n * nq - 1, 0), 0)
    nxt = lambda b, n: (b * nb + jnp.minimum((n + 1) * nq, nb - 1), 0)

    def variant(j):
        first = lambda n: (n == 0).astype(jnp.int32) if j == 0 else 0
        last = lambda n: (n == ns - 1).astype(jnp.int32) if j == nq - 1 else 0
        return lambda b, n: (2 * first(n) + last(n), 0, 0, 0)

    kw, vw = k.shape[1], v.shape[1]
    bias = _window_bias()
    return pl.pallas_call(
        functools.partial(_win_attn_kernel, nq=nq),
        grid=(batch, ns),
        in_specs=([pl.BlockSpec(memory_space=pltpu.SMEM)]
                  + [pl.BlockSpec((None, A_HEADS, BLOCK, 3 * BLOCK), variant(j)) for j in range(nq)]
                  + [pl.BlockSpec((nq * BLOCK, q.shape[1]), cur),
                     pl.BlockSpec((BLOCK, kw), prev), pl.BlockSpec((nq * BLOCK, kw), cur), pl.BlockSpec((BLOCK, kw), nxt),
                     pl.BlockSpec((BLOCK, vw), prev), pl.BlockSpec((nq * BLOCK, vw), cur),
                     pl.BlockSpec((BLOCK, vw), nxt)]),
        out_specs=pl.BlockSpec((nq * BLOCK, A_HEADS * HEAD_DIM), cur),
        out_shape=jax.ShapeDtypeStruct((batch * seq, A_HEADS * HEAD_DIM), BF16),
        compiler_params=_cparams("parallel", "parallel"),
        name="window_attention",
    )(sink, *([bias] * nq), q, k, k, k, v, v, v)


def _attn_in_kernel(x_ref, nw_ref, w_ref, qnw_ref, wq_ref, kvnw_ref, wk_ref, wv_ref, qtab_ref, ctab_ref, stab_ref,
                    qa_ref, ka_ref, va_ref, q_ref, k_ref, v_ref, *, segs):
    xn = _rms(x_ref[...], nw_ref[...]).astype(BF16)
    proj = [_dot(xn, w_ref[:, start:start + width]) for start, width in segs]
    qa_ref[...] = (proj[0] * WINDOW_Q_SCALE).astype(qa_ref.dtype)
    ka_ref[...] = proj[1].astype(ka_ref.dtype)
    va_ref[...] = proj[2].astype(va_ref.dtype)
    cq, ckv, kab = proj[3:]
    xq = _rms(cq, qnw_ref[...]).astype(BF16)
    tab = qtab_ref[...]
    for h in range(B_HEADS):
        cols = slice(h * LANE, (h + 1) * LANE)
        q_ref[:, cols] = (_dot(xq, wq_ref[:, cols]) * tab).astype(q_ref.dtype)
    xkv = _rms(ckv, kvnw_ref[...]).astype(BF16)
    kr = kab[:, :LANE] * ctab_ref[...] + kab[:, LANE:] * stab_ref[...]
    for h in range(B_HEADS):
        cols = slice(h * LANE, (h + 1) * LANE)
        k_ref[:, cols] = (_dot(xkv, wk_ref[:, cols]) + kr).astype(k_ref.dtype)
    lane = lax.broadcasted_iota(jnp.int32, v_ref.shape, 1) % LANE
    v_ref[...] = jnp.where(lane == B_VDIM, 1.0, _dot(xkv, wv_ref[...])).astype(v_ref.dtype)


def attn_in_proj(x, norm_w, w, segs, q_norm_w, wq, kv_norm_w, wk, wv, q_tab, k_cos, k_sin, seq, tm):
    t, k = x.shape
    nt = seq // tm
    row = lambda i: (i, 0)
    pos = lambda i: (i % nt, 0)
    widths = [wd for _, wd in segs[:3]] + [wq.shape[1], wk.shape[1], wv.shape[1]]
    return pl.pallas_call(
        functools.partial(_attn_in_kernel, segs=tuple(segs)),
        grid=(t // tm,),
        in_specs=[pl.BlockSpec((tm, k), row), _resident((1, k)), _resident(w.shape),
                  _resident((1, wq.shape[0])), _resident(wq.shape),
                  _resident((1, wk.shape[0])), _resident(wk.shape), _resident(wv.shape),
                  pl.BlockSpec((tm, LANE), pos), pl.BlockSpec((tm, LANE), pos), pl.BlockSpec((tm, LANE), pos)],
        out_specs=[pl.BlockSpec((tm, wd), row) for wd in widths],
        out_shape=[jax.ShapeDtypeStruct((t, wd), BF16) for wd in widths],
        compiler_params=_cparams("parallel"),
        name="attn_in_proj",
    )(x, norm_w.reshape(1, k), w, q_norm_w.reshape(1, -1), wq, kv_norm_w.reshape(1, -1), wk, wv,
      q_tab, k_cos, k_sin)


def _mla_flash_kernel(q_ref, k_ref, v_ref, o_ref, s_ref, p_ref, m_ref, a_ref, acc_ref, *, tk, nk, rb):
    tq = q_ref.shape[0]
    heads = range(2)
    m_ref[...] = jnp.full(m_ref.shape, -jnp.inf, F32)
    acc_ref[...] = jnp.zeros(acc_ref.shape, F32)

    def body(j, carry):
        rows = pl.ds(pl.multiple_of(j * tk, tk), tk)
        for e in heads:
            cols = slice(e * LANE, (e + 1) * LANE)
            s_ref[e] = _dot_nt(q_ref[:, cols], k_ref[rows, cols])
        for e in heads:
            cols = slice(e * LANE, (e + 1) * LANE)
            for r in range(tq // rb):
                rs = slice(r * rb, (r + 1) * rb)
                s = s_ref[e, rs, :]
                m_old = m_ref[e, rs, :]
                m_new = jnp.maximum(m_old, jnp.max(s, axis=-1, keepdims=True))
                m_ref[e, rs, :] = m_new
                a_ref[e, rs, :] = jnp.exp2(m_old - m_new)
                p_ref[e, rs, :] = jnp.exp2(s - jnp.concatenate([m_new] * (tk // LANE), axis=1)).astype(BF16)
            acc_ref[e] = a_ref[e] * acc_ref[e] + _dot(p_ref[e], v_ref[rows, cols])
        return carry

    lax.fori_loop(0, nk, body, 0, unroll=True)
    outs = []
    for e in heads:
        acc = acc_ref[e]
        outs.append(acc / acc[:, B_VDIM:B_VDIM + 1])
    lane = lax.broadcasted_iota(jnp.int32, (tq, LANE), 1)
    o_ref[...] = jnp.where(lane < B_VDIM, outs[0], pltpu.roll(outs[1], B_VDIM, axis=1)).astype(o_ref.dtype)


def mla_flash(q, k, v, batch, seq, tq, tk, rb):
    nq = seq // tq
    npair = B_HEADS // 2
    return pl.pallas_call(
        functools.partial(_mla_flash_kernel, tk=tk, nk=seq // tk, rb=rb),
        grid=(batch, npair, nq),
        in_specs=[pl.BlockSpec((tq, 2 * LANE), lambda b, p, i: (b * nq + i, p)),
                  pl.BlockSpec((seq, 2 * LANE), lambda b, p, i: (b, p)),
                  pl.BlockSpec((seq, 2 * LANE), lambda b, p, i: (b, p))],
        out_specs=pl.BlockSpec((tq, LANE), lambda b, p, i: (b * nq + i, p)),
        out_shape=jax.ShapeDtypeStruct((batch * seq, B_HEADS * B_VDIM), BF16),
        scratch_shapes=[pltpu.VMEM((2, tq, tk), F32), pltpu.VMEM((2, tq, tk), BF16), pltpu.VMEM((2, tq, LANE), F32),
                        pltpu.VMEM((2, tq, LANE), F32), pltpu.VMEM((2, tq, LANE), F32)],
        compiler_params=_cparams("parallel", "parallel", "parallel"),
        name="mla_flash",
    )(q, k, v)


def _mlp_tail(x, nw_ref, w1_ref, w2_ref, fw_ref, o_ref, fchunk, final_norm):
    xn = _rms(x, nw_ref[...]).astype(BF16)
    acc = x
    for f in range(w1_ref.shape[1] // fchunk):
        cols = slice(f * fchunk, (f + 1) * fchunk)
        h = jnp.square(jnp.maximum(_dot(xn, w1_ref[:, cols]), 0.0)).astype(BF16)
        acc = acc + _dot(h, w2_ref[cols, :])
    if final_norm:
        acc = _rms(acc, fw_ref[...])
    o_ref[...] = acc


def _attn_out_mlp_kernel(x_ref, a_ref, b_ref, wo_ref, nw_ref, w1_ref, w2_ref, fw_ref, o_ref, *, fchunk, final_norm):
    ka = a_ref.shape[1]
    o_ref[...] = x_ref[...] + _dot(a_ref[...], wo_ref[:ka, :]) + _dot(b_ref[...], wo_ref[ka:, :])
    _mlp_tail(o_ref[...], nw_ref, w1_ref, w2_ref, fw_ref, o_ref, fchunk, final_norm)


def _lin_out_mlp_kernel(x_ref, cf_ref, cb_ref, df_ref, db_ref, gc_ref, gd_ref, cw_ref, dw_ref, wo_ref,
                        nw_ref, w1_ref, w2_ref, fw_ref, o_ref, *, fchunk, final_norm):
    def gated(f_ref, b_ref, gate_ref, gnw_ref):
        o = f_ref[...] + b_ref[...]
        parts = []
        for h in range(LIN_HEADS):
            cols = slice(h * LIN_DV, (h + 1) * LIN_DV)
            gate = gate_ref[:, cols].astype(F32)
            parts.append((_rms(o[:, cols], gnw_ref[...]) * (gate * _sigmoid(gate))).astype(BF16))
        return jnp.concatenate(parts, axis=1)

    kc = cf_ref.shape[1]
    o_ref[...] = (x_ref[...] + _dot(gated(cf_ref, cb_ref, gc_ref, cw_ref), wo_ref[:kc, :])
                  + _dot(gated(df_ref, db_ref, gd_ref, dw_ref), wo_ref[kc:, :]))
    _mlp_tail(o_ref[...], nw_ref, w1_ref, w2_ref, fw_ref, o_ref, fchunk, final_norm)


def mixer_out_mlp(x, mixer_inputs, small_params, w_out, norm_w, w1, w2, final_w, final_norm, tm, fchunk):
    t, d = x.shape
    row = lambda i: (i, 0)
    body = _attn_out_mlp_kernel if len(mixer_inputs) == 2 else _lin_out_mlp_kernel
    return pl.pallas_call(
        functools.partial(body, fchunk=fchunk, final_norm=final_norm),
        grid=(t // tm,),
        in_specs=([pl.BlockSpec((tm, d), row)] + [pl.BlockSpec((tm, a.shape[1]), row) for a in mixer_inputs]
                  + [_resident(p.shape) for p in small_params]
                  + [_resident(w_out.shape), _resident((1, d)), _resident(w1.shape), _resident(w2.shape),
                     _resident((1, d))]),
        out_specs=pl.BlockSpec((tm, d), row),
        out_shape=jax.ShapeDtypeStruct((t, d), F32),
        compiler_params=_cparams("parallel"),
        name="mixer_out_mlp",
    )(x, *mixer_inputs, *small_params, w_out, norm_w.reshape(1, d), w1, w2, final_w.reshape(1, d))


def _direction_masks(d):
    sgn = 1 - 2 * d
    r = lax.broadcasted_iota(jnp.int32, (STACK, STACK), 0)
    c = lax.broadcasted_iota(jnp.int32, (STACK, STACK), 1)
    same = (r & -CHUNK) == (c & -CHUNK)
    order = ((r & (CHUNK - 1)) - (c & (CHUNK - 1))) * sgn
    i = lax.broadcasted_iota(jnp.int32, (CHUNK, CHUNK), 0)
    j = lax.broadcasted_iota(jnp.int32, (CHUNK, CHUNK), 1)
    cum = (((i - j) * sgn) >= 0).astype(F32)
    return same, same & (order >= 0), same & (order > 0), cum


def _block_diag(x, same):
    return jnp.where(same, jnp.concatenate([x] * LIN_HEADS, axis=0), 0.0)


def _block_diag_cols(x, same):
    return jnp.where(same, jnp.concatenate([x] * LIN_HEADS, axis=1), 0.0)


def _stack_heads(v, width):
    return jnp.concatenate([v[:, h * width:(h + 1) * width] for h in range(LIN_HEADS)], axis=0)


def _each(f, *lists):
    return [f(*args) for args in zip(*lists)]


def _scan_order(ncb):
    return [((0, i), (1, ncb - 1 - i)) for i in range(ncb)]


def _gla_kernel(qf_ref, kf_ref, vf_ref, sf_ref, qb_ref, kb_ref, vb_ref, sb_ref, wg_ref, bg_ref,
                of_ref, ob_ref, st_ref, *, ncb, nb):
    @pl.when(pl.program_id(1) == 0)
    def _():
        st_ref[...] = jnp.zeros_like(st_ref)

    refs = ((qf_ref, kf_ref, vf_ref, sf_ref, of_ref), (qb_ref, kb_ref, vb_ref, sb_ref, ob_ref))
    masks = [_direction_masks(d) for d in range(2)]
    items = [(bi, d, ci) for bi in range(nb) for d in range(2) for ci in range(ncb)]
    rows = lambda ci: slice(ci * CHUNK, (ci + 1) * CHUNK)
    chunk = lambda which, it: refs[it[1]][which][it[0], rows(it[2]), :]
    same = [masks[d][0] for _, d, _ in items]
    z = [_dot(chunk(3, it).astype(BF16), wg_ref[it[1]]) + bg_ref[it[1]] for it in items]
    log_a = _each(lambda a: (jnp.minimum(a, 0.0) - jnp.log1p(jnp.exp(-jnp.abs(a)))) * (1.0 / GATE_NORM), z)
    b = [_dot_mask_f32(masks[it[1]][3].astype(BF16), a) for it, a in zip(items, log_a)]
    b_last = _each(lambda a: jnp.sum(a, axis=0, keepdims=True), log_a)
    q_in = [_block_diag(chunk(0, it) * (LIN_DK ** -0.5) * jnp.exp(bb), sm).astype(BF16)
            for it, bb, sm in zip(items, b, same)]
    k = [chunk(1, it) for it in items]
    k_in = _each(lambda a, bb, sm: _block_diag(a * jnp.exp(-bb), sm).astype(BF16), k, b, same)
    k_out = _each(lambda a, bb, bl, sm: _block_diag(a * jnp.exp(bl - bb), sm).astype(BF16), k, b, b_last, same)
    v = [_stack_heads(chunk(2, it), LIN_DV).astype(BF16) for it in items]
    sc = [jnp.where(masks[it[1]][1], _dot_nt(a, bb), 0.0).astype(BF16) for it, a, bb in zip(items, q_in, k_in)]
    o_intra = dict(zip(items, _each(_dot, sc, v)))
    ds = dict(zip(items, _each(_dot_tn, v, k_out)))
    dec = dict(zip(items, _each(jnp.exp, b_last)))
    q_in = dict(zip(items, q_in))

    streams = [(bi, d) for bi in range(nb) for d in range(2)]
    st = {sd: st_ref[i] for i, sd in enumerate(streams)}
    for fwd_bwd in _scan_order(ncb):
        step = [(bi, d, ci) for bi in range(nb) for d, ci in fwd_bwd]
        o = [o_intra[it] + _dot_nt(q_in[it], st[it[:2]].astype(BF16)) for it in step]
        for it, od in zip(step, o):
            bi, d, ci = it
            st[(bi, d)] = dec[it] * st[(bi, d)] + ds[it]
            for h in range(LIN_HEADS):
                refs[d][4][bi, rows(ci), h * LIN_DV:(h + 1) * LIN_DV] = od[h * CHUNK:(h + 1) * CHUNK, :]
    for i, sd in enumerate(streams):
        st_ref[i] = st[sd]


def _bidir_call(body, name, arrays, extra, extra_specs, batch, seq, tc, nb, hk, hv):
    nblk = seq // tc
    ins = [a.reshape(batch, seq, a.shape[1]) for a in arrays]
    fwd = lambda bb, n: (bb, n, 0)
    bwd = lambda bb, n: (bb, nblk - 1 - n, 0)
    spec = lambda a, idx: pl.BlockSpec((nb, tc, a.shape[2]), idx)
    o_f, o_b = pl.pallas_call(
        functools.partial(body, ncb=tc // CHUNK, nb=nb),
        grid=(batch // nb, nblk),
        in_specs=[spec(a, fwd) for a in ins] + [spec(a, bwd) for a in ins] + extra_specs,
        out_specs=[pl.BlockSpec((nb, tc, hv), fwd), pl.BlockSpec((nb, tc, hv), bwd)],
        out_shape=[jax.ShapeDtypeStruct((batch, seq, hv), F32)] * 2,
        scratch_shapes=[pltpu.VMEM((2 * nb, LIN_DV, hk), F32)],
        compiler_params=_cparams("parallel", "arbitrary"),
        name=name,
    )(*ins, *ins, *extra)
    return o_f.reshape(batch * seq, hv), o_b.reshape(batch * seq, hv)


def gla_scan(q, k, v, small, wg, bg, batch, seq, tc, nb):
    return _bidir_call(_gla_kernel, "gla_scan", (q, k, v, small), (wg, bg),
                       [_resident(wg.shape), _resident(bg.shape)], batch, seq, tc, nb, q.shape[1], v.shape[1])


def _gdn_prep_kernel(xp_ref, xc_ref, xn_ref, sm_ref, cw_ref, gsum_ref, alog_ref, dtb_ref,
                     q_ref, k_ref, v_ref, g_ref, b_ref, bt_ref, buf_ref, *, nt):
    i = pl.program_id(1)
    tm = xc_ref.shape[0]
    halo = xp_ref.shape[0]
    buf_ref[:halo, :] = xp_ref[...].astype(F32) * (i > 0).astype(F32)
    buf_ref[halo:halo + tm, :] = xc_ref[...].astype(F32)
    buf_ref[halo + tm:, :] = xn_ref[...].astype(F32) * (i < nt - 1).astype(F32)
    first = halo - CONV_TAPS // 2
    acc = buf_ref[first:first + tm, :] * cw_ref[0:1, :]
    for j in range(1, CONV_TAPS):
        acc = acc + buf_ref[first + j:first + j + tm, :] * cw_ref[j:j + 1, :]
    y = acc * _sigmoid(acc)
    nqk = LIN_HEADS * LIN_DK
    for idx, (o_ref, scale) in enumerate(((q_ref, LIN_DK ** -0.5), (k_ref, 1.0))):
        part = y[:, idx * nqk:(idx + 1) * nqk]
        ssq = _dot_f32_mask(part * part, gsum_ref[...])
        o_ref[...] = (part * lax.rsqrt(ssq + EPS) * scale).astype(o_ref.dtype)
    v_ref[...] = y[:, 2 * nqk:].astype(v_ref.dtype)
    sm = sm_ref[...]
    beta = _sigmoid(sm)
    g = -jnp.exp(alog_ref[...]) * _softplus(sm + dtb_ref[...])
    lane = lax.broadcasted_iota(jnp.int32, sm.shape, 1) - GATE_LANE0
    is_beta = (lane >= 0) & (lane < 2 * LIN_HEADS)
    is_g = (lane >= 2 * LIN_HEADS) & (lane < 4 * LIN_HEADS)
    g = jnp.where(is_g, g, 0.0)
    g_ref[...] = jnp.where(is_beta, beta, g)
    ri = lax.broadcasted_iota(jnp.int32, (CHUNK, CHUNK), 0)
    ci = lax.broadcasted_iota(jnp.int32, (CHUNK, CHUNK), 1)
    tril16 = (ri >= ci).astype(BF16)
    ones16 = jnp.ones((CHUNK, CHUNK), BF16)
    is_bwd = lax.broadcasted_iota(jnp.int32, (CHUNK, LANE), 1) >= GATE_LANE0 + 3 * LIN_HEADS
    for c in range(tm // CHUNK):
        rows = slice(c * CHUNK, (c + 1) * CHUNK)
        gc = g[rows, :]
        prefix = _dot_mask_f32(tril16, gc)
        total = _dot_mask_f32(ones16, gc)
        b_ref[rows, :] = jnp.where(is_bwd, total - prefix + gc, prefix)
        bt_ref[rows, :] = total


def gdn_prep(dqkv, small, conv_w, gsum, alog, dtb, batch, seq, tm):
    nt = seq // tm
    t, c = dqkv.shape
    halo = 2 * SUBLANE
    hb = tm // halo
    nqk = LIN_HEADS * LIN_DK
    row = lambda b, i: (b * nt + i, 0)
    prev = lambda b, i: (jnp.maximum((b * nt + i) * hb - 1, 0), 0)
    nxt = lambda b, i: (jnp.minimum((b * nt + i + 1) * hb, t // halo - 1), 0)
    return pl.pallas_call(
        functools.partial(_gdn_prep_kernel, nt=nt),
        grid=(batch, nt),
        in_specs=[pl.BlockSpec((halo, c), prev), pl.BlockSpec((tm, c), row), pl.BlockSpec((halo, c), nxt),
                  pl.BlockSpec((tm, LANE), row),
                  _resident(conv_w.shape), _resident(gsum.shape), _resident((1, LANE)), _resident((1, LANE))],
        out_specs=[pl.BlockSpec((tm, nqk), row), pl.BlockSpec((tm, nqk), row),
                   pl.BlockSpec((tm, c - 2 * nqk), row)] + [pl.BlockSpec((tm, LANE), row)] * 3,
        out_shape=[jax.ShapeDtypeStruct((t, nqk), BF16), jax.ShapeDtypeStruct((t, nqk), BF16),
                   jax.ShapeDtypeStruct((t, c - 2 * nqk), BF16)] + [jax.ShapeDtypeStruct((t, LANE), F32)] * 3,
        scratch_shapes=[pltpu.VMEM((tm + 2 * halo, c), F32)],
        compiler_params=_cparams("parallel", "parallel"),
        name="gdn_prep",
    )(dqkv, dqkv, dqkv, small, conv_w, gsum, alog, dtb)


def _gdn_kernel(qf_ref, kf_ref, vf_ref, gf_ref, bf_ref, tf_ref, qb_ref, kb_ref, vb_ref, gb_ref, bb_ref, tb_ref,
                of_ref, ob_ref, st_ref, *, ncb, nb):
    @pl.when(pl.program_id(1) == 0)
    def _():
        st_ref[...] = jnp.zeros_like(st_ref)

    refs = ((qf_ref, kf_ref, vf_ref, gf_ref, bf_ref, tf_ref, of_ref),
            (qb_ref, kb_ref, vb_ref, gb_ref, bb_ref, tb_ref, ob_ref))
    masks = [_direction_masks(d) for d in range(2)]
    r = lax.broadcasted_iota(jnp.int32, (STACK, STACK), 0)
    c = lax.broadcasted_iota(jnp.int32, (STACK, STACK), 1)
    diag = r == c
    eye = diag.astype(F32)
    lane_blk = lax.broadcasted_iota(jnp.int32, (SUBLANE, STACK), 1) // CHUNK

    def rows_of(x, lane0):
        return jnp.concatenate(
            [jnp.broadcast_to(x[:, lane0 + h:lane0 + h + 1], (CHUNK, STACK)) for h in range(LIN_HEADS)], axis=0)

    def head_lanes(row, lane0):
        out = jnp.broadcast_to(row[:, lane0:lane0 + 1], (SUBLANE, STACK))
        for h in range(1, LIN_HEADS):
            out = jnp.where(lane_blk == h, jnp.broadcast_to(row[:, lane0 + h:lane0 + h + 1], (SUBLANE, STACK)), out)
        return out

    each = _each
    items = [(bi, d, ci) for bi in range(nb) for d in range(2) for ci in range(ncb)]
    rows = lambda ci: slice(ci * CHUNK, (ci + 1) * CHUNK)
    chunk = lambda which, it: refs[it[1]][which][it[0], rows(it[2]), :]
    same = [masks[d][0] for _, d, _ in items]
    beta_col = [rows_of(chunk(3, it), _gate_lanes(it[1])[0]) for it in items]
    b_col = [rows_of(chunk(4, it), _gate_lanes(it[1])[1]) for it in items]
    bl_col = [rows_of(chunk(5, it), _gate_lanes(it[1])[1]) for it in items]
    k_bd = [_block_diag(chunk(1, it), sm) for it, sm in zip(items, same)]
    q_bd = [_block_diag(chunk(0, it), sm) for it, sm in zip(items, same)]
    kb_bd = each(lambda a, b: a * b, k_bd, beta_col)
    k16 = each(lambda a: a.astype(BF16), k_bd)
    kk = each(lambda a, b: _dot_nt(a.astype(BF16), b), kb_bd, k16)
    qk = each(lambda a, b: _dot_nt(a.astype(BF16), b), q_bd, k16)
    b_row = each(lambda bc: jnp.sum(jnp.where(diag, bc, 0.0), axis=0, keepdims=True), b_col)
    gamma = [jnp.exp(jnp.where(masks[it[1]][1], bc - br, -jnp.inf)) for it, bc, br in zip(items, b_col, b_row)]
    lower = [jnp.where(masks[it[1]][2], a * g, 0.0) for it, a, g in zip(items, kk, gamma)]
    attn = each(lambda a, g: (a * g).astype(BF16), qk, gamma)
    l_hi = each(lambda a: a.astype(BF16), lower)
    pw = [l_hi]
    for _ in range(5):
        pw.append(each(lambda a: _dot(a, a).astype(BF16), pw[-1]))

    def pair(a, b16):
        return a + _dot(a.astype(BF16), b16)

    f01 = each(lambda lo, p1: pair(eye - lo, p1), lower, pw[1])
    f23 = each(lambda p2, p3: pair(eye + p2.astype(F32), p3), pw[2], pw[3])
    f45 = each(lambda p4, p5: pair(eye + p4.astype(F32), p5), pw[4], pw[5])
    f0123 = each(lambda a, b: _dot(a.astype(BF16), b.astype(BF16)).astype(BF16), f01, f23)
    inv16 = each(lambda a, b: _dot(a, b.astype(BF16)).astype(BF16), f0123, f45)
    e_b = each(jnp.exp, b_col)
    vb = [_stack_heads(chunk(2, it), LIN_DV) * bc[:, :LIN_DV] for it, bc in zip(items, beta_col)]
    kbe = [_stack_heads(chunk(1, it), LIN_DK) * (bc * e)[:, :LIN_DK] for it, bc, e in zip(items, beta_col, e_b)]
    rhs = each(lambda v, kb: jnp.concatenate([v, kb], axis=1), vb, kbe)
    x16 = each(lambda m, r_: _dot(m, r_.astype(BF16)).astype(BF16), inv16, rhs)
    l_lo = each(lambda lo, hi: (lo - hi.astype(F32)).astype(BF16), lower, l_hi)
    resid = each(lambda r_, x, hi, lo: r_ - (x.astype(F32) + _dot(hi, x) + _dot(lo, x)), rhs, x16, l_hi, l_lo)
    sol = each(lambda x, m, rs: x.astype(F32) + _dot(m, rs.astype(BF16)), x16, inv16, resid)
    u = {it: sol[i][:, :LIN_DV] for i, it in enumerate(items)}
    w = {it: _block_diag_cols(sol[i][:, LIN_DV:].astype(BF16), same[i])
         for i, it in enumerate(items)}
    attn = dict(zip(items, attn))
    q_dec = {it: (q_bd[i] * e_b[i]).astype(BF16) for i, it in enumerate(items)}
    k_dec = {it: (k_bd[i] * jnp.exp(bl_col[i] - b_col[i])).astype(BF16) for i, it in enumerate(items)}
    dec = {(bi, d, ci): jnp.exp(head_lanes(refs[d][5][bi, ci * CHUNK:ci * CHUNK + SUBLANE, :],
                                           _gate_lanes(d)[1]))[:1, :] for bi, d, ci in items}

    streams = [(bi, d) for bi in range(nb) for d in range(2)]
    st = {sd: st_ref[i] for i, sd in enumerate(streams)}
    for fwd_bwd in _scan_order(ncb):
        step = [(bi, d, ci) for bi in range(nb) for d, ci in fwd_bwd]
        st16 = [st[it[:2]].astype(BF16) for it in step]
        v_new16 = [(u[it] - _dot_nt(w[it], s16)).astype(BF16) for it, s16 in zip(step, st16)]
        o = [_dot_nt(q_dec[it], s16) + _dot(attn[it], vn) for it, s16, vn in zip(step, st16, v_new16)]
        for it, od, vn in zip(step, o, v_new16):
            bi, d, ci = it
            st[(bi, d)] = dec[it] * st[(bi, d)] + _dot_tn(vn, k_dec[it])
            for h in range(LIN_HEADS):
                refs[d][6][bi, rows(ci), h * LIN_DV:(h + 1) * LIN_DV] = od[h * CHUNK:(h + 1) * CHUNK, :]
    for i, sd in enumerate(streams):
        st_ref[i] = st[sd]


def _gate_lanes(d):
    return GATE_LANE0 + d * LIN_HEADS, GATE_LANE0 + (2 + d) * LIN_HEADS


def gdn_scan(q, k, v, mixed, bcum, btot, batch, seq, tc, nb):
    return _bidir_call(_gdn_kernel, "gdn_scan", (q, k, v, mixed, bcum, btot), (), [],
                       batch, seq, tc, nb, q.shape[1], v.shape[1])


def _rot_half_cols(w):
    half = w.shape[-1] // 2
    return jnp.concatenate([-w[..., half:], w[..., :half]], axis=-1)


def _pad_heads(w, heads, dim, offset=0):
    k = w.shape[0]
    out = jnp.zeros((k, heads, LANE), w.dtype)
    out = out.at[:, :, offset:offset + dim].set(w.reshape(k, heads, dim))
    return out.reshape(k, heads * LANE)


def _layer0_in_weight(w_in):
    aq = A_HEADS * HEAD_DIM
    akv = A_KV_HEADS * HEAD_DIM
    o = 0
    w_q = w_in[:, o:o + aq]; o += aq
    w_k = w_in[:, o:o + akv]; o += akv
    w_v = w_in[:, o:o + akv]; o += akv
    w_cq = w_in[:, o:o + B_Q_RANK]; o += B_Q_RANK
    w_ckv = w_in[:, o:o + B_KV_RANK]; o += B_KV_RANK
    w_pe = w_in[:, o:o + B_ROPE]
    zeros = jnp.zeros((w_in.shape[0], B_NOPE), w_in.dtype)
    v_lo = _pad_heads(w_v, A_KV_HEADS, HEAD_DIM, 0).reshape(-1, A_KV_HEADS, LANE)
    v_hi = _pad_heads(w_v, A_KV_HEADS, HEAD_DIM, HEAD_DIM).reshape(-1, A_KV_HEADS, LANE)
    w_v2 = jnp.stack([v_lo, v_hi], axis=2).reshape(w_in.shape[0], 2 * A_KV_HEADS * LANE)
    w_pe_rot = _rot_half_cols(w_pe)
    pieces = [_pad_heads(w_q, A_HEADS, HEAD_DIM), _pad_heads(w_k, A_KV_HEADS, HEAD_DIM), w_v2,
              w_cq, w_ckv,
              jnp.concatenate([zeros, w_pe, w_pe, zeros, w_pe_rot, w_pe_rot], axis=1)]
    segs, start = [], 0
    for p in pieces:
        segs.append((start, p.shape[1]))
        start += p.shape[1]
    return jnp.concatenate(pieces, axis=1).astype(BF16), segs


def _mla_weights(w_uq, w_ukv):
    r = w_uq.shape[0]
    wq = w_uq.reshape(r, B_HEADS, B_NOPE + B_ROPE)
    pe = wq[:, :, B_NOPE:]
    wq = jnp.concatenate([wq[:, :, :B_NOPE], pe, _rot_half_cols(pe)], axis=-1).reshape(r, B_HEADS * LANE)
    rk = w_ukv.shape[0]
    wkv = w_ukv.reshape(rk, B_HEADS, B_NOPE + B_VDIM)
    wk = _pad_heads(wkv[:, :, :B_NOPE].reshape(rk, B_HEADS * B_NOPE), B_HEADS, B_NOPE)
    wv = _pad_heads(wkv[:, :, B_NOPE:].reshape(rk, B_HEADS * B_VDIM), B_HEADS, B_VDIM)
    return wq.astype(BF16), wk.astype(BF16), wv.astype(BF16)


def _rope_tables(seq):
    half = B_ROPE // 2
    inv = ROPE_THETA ** (-jnp.arange(half, dtype=F32) / half)
    ang = jnp.arange(seq, dtype=F32)[:, None] * inv[None, :]
    cos2 = jnp.tile(jnp.cos(ang), (1, 4))
    sin2 = jnp.tile(jnp.sin(ang), (1, 4))
    zeros = jnp.zeros((seq, B_NOPE), F32)
    qscale = (B_NOPE + B_ROPE) ** -0.5 * LOG2E
    q_tab = jnp.concatenate([jnp.ones((seq, B_NOPE), F32), cos2[:, :B_ROPE], sin2[:, :B_ROPE]], axis=1) * qscale
    k_cos = jnp.concatenate([zeros, cos2], axis=1)
    k_sin = jnp.concatenate([zeros, sin2], axis=1)
    return q_tab, k_cos, k_sin


def _layer1_in_weight(w_in):
    hk = LIN_HEADS * LIN_DK
    hv = LIN_HEADS * LIN_DV
    conv_ch = 2 * hk + hv
    o = 2 * hk + 2 * hv
    w_main = w_in[:, :o]
    w_gl = w_in[:, o:o + 2 * GATE_RANK]; o += 2 * GATE_RANK
    w_conv = w_in[:, o:o + conv_ch]; o += conv_ch
    w_z = w_in[:, o:o + hv]; o += hv
    w_small = w_in[:, o:o + 4 * LIN_HEADS]
    pad = jnp.zeros((w_in.shape[0], LANE - 2 * GATE_RANK - 4 * LIN_HEADS), w_in.dtype)
    w = jnp.concatenate([w_main, w_conv, w_z, w_gl, w_small, pad], axis=1).astype(BF16)
    widths = [hk, hk, hv, hv, conv_ch, hv, LANE]
    segs, start = [], 0
    for wd in widths:
        segs.append((start, wd))
        start += wd
    return w, segs


def _pick_tile(n, pref):
    t = min(n, pref)
    while n % t:
        t //= 2
    return t


def kernel(x, att_norm, att_w_in, att_sink, mla_q_norm, mla_w_uq, mla_kv_norm, mla_w_ukv, att_w_out, lin_norm, lin_w_in, gla_w_gate_f, gla_b_gate_f, gla_w_gate_b, gla_b_gate_b, gla_norm, gdn_conv, gdn_a_log_f, gdn_dt_bias_f, gdn_a_log_b, gdn_dt_bias_b, gdn_norm, lin_w_out, mlp_norm, mlp_w1, mlp_w2, final_norm):
    batch, seq, dm = x.shape
    t = batch * seq
    depth = mlp_norm.shape[0]
    tm = _pick_tile(seq, ROW_TILE)
    tp = _pick_tile(seq, PROJ_ROW_TILE)
    xs = x.reshape(t, dm)
    for layer in range(depth):
        i = layer // 2
        if layer % 2 == 0:
            w0, segs = _layer0_in_weight(att_w_in[i])
            wq, wk, wv = _mla_weights(mla_w_uq[i], mla_w_ukv[i])
            q_tab, k_cos, k_sin = _rope_tables(seq)
            qa, ka, va, qb, kb, vb = attn_in_proj(xs, att_norm[i], w0, segs, mla_q_norm[i], wq, mla_kv_norm[i],
                                                  wk, wv, q_tab, k_cos, k_sin, seq, tp)
            o_a = window_attention(qa, ka, va, att_sink[i].astype(F32), batch, seq)
            o_b = mla_flash(qb, kb, vb, batch, seq, _pick_tile(seq, FLASH_Q_TILE), _pick_tile(seq, FLASH_KV_TILE),
                            FLASH_ROW_BLOCK)
            mixer_inputs, small_params, w_out = [o_a, o_b], [], att_w_out[i]
        else:
            w1, segs = _layer1_in_weight(lin_w_in[i])
            cq, ck, cv, cg, dqkv, dz, small = norm_proj(xs, lin_norm[i], w1, segs, [BF16] * 6 + [F32], tp)
            hk = LIN_HEADS * LIN_DK
            wg = jnp.zeros((2, LANE, hk), F32)
            wg = wg.at[0, :GATE_RANK].set(gla_w_gate_f[i]).at[1, GATE_RANK:2 * GATE_RANK].set(gla_w_gate_b[i])
            bg = jnp.stack([gla_b_gate_f[i], gla_b_gate_b[i]]).reshape(2, 1, hk).astype(F32)
            tc = _pick_tile(seq, SCAN_ROWS)
            nb = 2 if batch % 2 == 0 else 1
            o_cf, o_cb = gla_scan(cq, ck, cv, small, wg.astype(BF16), bg, batch, seq, _pick_tile(seq, 2 * SCAN_ROWS), nb)
            gsum = (jnp.arange(hk)[:, None] // LIN_DK == jnp.arange(hk)[None, :] // LIN_DK).astype(BF16)
            nh = LIN_HEADS
            a0 = GATE_LANE0 + 2 * nh
            alog = jnp.zeros((LANE,), F32).at[a0:a0 + nh].set(gdn_a_log_f[i]).at[a0 + nh:a0 + 2 * nh].set(gdn_a_log_b[i])
            dtb = jnp.zeros((LANE,), F32).at[a0:a0 + nh].set(gdn_dt_bias_f[i]).at[a0 + nh:a0 + 2 * nh].set(gdn_dt_bias_b[i])
            dq, dk, dv, mixed, bcum, btot = gdn_prep(dqkv, small, gdn_conv[i].astype(F32), gsum,
                                                     alog.reshape(1, LANE), dtb.reshape(1, LANE), batch, seq, tp)
            o_df, o_db = gdn_scan(dq, dk, dv, mixed, bcum, btot, batch, seq, tc, nb)
            mixer_inputs = [o_cf, o_cb, o_df, o_db, cg, dz]
            small_params = [gla_norm[i].reshape(1, LIN_DV), gdn_norm[i].reshape(1, LIN_DV)]
            w_out = lin_w_out[i]
        last = layer == depth - 1
        tm_mlp = _pick_tile(seq, ATTN_MLP_ROW_TILE) if layer % 2 == 0 else tm
        xs = mixer_out_mlp(xs, mixer_inputs, small_params, w_out.astype(BF16), mlp_norm[layer],
                           mlp_w1[layer].astype(BF16), mlp_w2[layer].astype(BF16), final_norm, last, tm_mlp,
                           MLP_F_CHUNK)
    return xs.reshape(batch, seq, dm)
```

```python
import functools
import math

import numpy as np
import jax
import jax.numpy as jnp
from jax import lax
from jax.experimental import pallas as pl
from jax.experimental.pallas import tpu as pltpu

F32 = jnp.float32
BF16 = jnp.bfloat16
EPS = 1e-6

LANE = 128
SUBLANE = 8
VMEM_LIMIT_BYTES = 56 * 1024 * 1024

ROW_TILE = 512
PROJ_ROW_TILE = 1024
ATTN_MLP_ROW_TILE = 1024
MLP_F_CHUNK = 1024
FLASH_Q_TILE = 1024
FLASH_KV_TILE = 2048
FLASH_ROW_BLOCK = 64
SCAN_ROWS = 256

HEAD_DIM = 64
A_HEADS = 8
A_KV_HEADS = 2
A_GROUP = A_HEADS // A_KV_HEADS
WINDOW = 128
BLOCK = 128
B_HEADS = 8
B_NOPE = 64
B_ROPE = 32
B_VDIM = 64
B_Q_RANK = 384
B_KV_RANK = 256
ROPE_THETA = 10000.0
LIN_HEADS = 4
LIN_DK = 64
LIN_DV = 128
GATE_RANK = 16
GATE_NORM = 16.0
CHUNK = 64
CONV_TAPS = 5
STACK = LIN_HEADS * CHUNK
GATE_LANE0 = 2 * GATE_RANK

LOG2E = math.log2(math.e)
WINDOW_Q_SCALE = HEAD_DIM ** -0.5 * LOG2E

NT_DIMS = (((1,), (1,)), ((), ()))
TN_DIMS = (((0,), (0,)), ((), ()))


def _cparams(*sem):
    return pltpu.CompilerParams(dimension_semantics=sem, vmem_limit_bytes=VMEM_LIMIT_BYTES)


def _resident(shape):
    nd = len(shape)
    return pl.BlockSpec(shape, lambda *_: (0,) * nd, pipeline_mode=pl.Buffered(1))


def _rms(x, w):
    return x * lax.rsqrt(jnp.mean(x * x, axis=-1, keepdims=True) + EPS) * w


def _sigmoid(x):
    return 1.0 / (1.0 + jnp.exp(-x))


def _softplus(x):
    return jnp.maximum(x, 0.0) + jnp.log1p(jnp.exp(-jnp.abs(x)))


def _dot(a, b):
    return jnp.dot(a, b, preferred_element_type=F32)


def _dot_nt(a, b):
    return lax.dot_general(a, b, NT_DIMS, preferred_element_type=F32)


def _dot_tn(a, b):
    return lax.dot_general(a, b, TN_DIMS, preferred_element_type=F32)


def _split3(x):
    hi = x.astype(BF16)
    r1 = x - hi.astype(F32)
    mid = r1.astype(BF16)
    return hi, mid, (r1 - mid.astype(F32)).astype(BF16)


def _dot_mask_f32(mask16, x):
    hi, mid, lo = _split3(x)
    return _dot(mask16, hi) + _dot(mask16, mid) + _dot(mask16, lo)


def _dot_f32_mask(x, mask16):
    hi, mid, lo = _split3(x)
    return _dot(hi, mask16) + _dot(mid, mask16) + _dot(lo, mask16)


def _norm_proj_kernel(x_ref, nw_ref, w_ref, *out_refs, segs):
    xn = _rms(x_ref[...], nw_ref[...]).astype(BF16)
    for (start, width), o_ref in zip(segs, out_refs):
        o_ref[...] = _dot(xn, w_ref[:, start:start + width]).astype(o_ref.dtype)


def norm_proj(x, norm_w, w, segs, dtypes, tm):
    t, k = x.shape
    n = w.shape[1]
    return pl.pallas_call(
        functools.partial(_norm_proj_kernel, segs=tuple(segs)),
        grid=(t // tm,),
        in_specs=[pl.BlockSpec((tm, k), lambda i: (i, 0)),
                  _resident((1, k)),
                  _resident((k, n))],
        out_specs=[pl.BlockSpec((tm, wd), lambda i: (i, 0)) for _, wd in segs],
        out_shape=[jax.ShapeDtypeStruct((t, wd), dt) for (_, wd), dt in zip(segs, dtypes)],
        compiler_params=_cparams("parallel"),
        name="norm_proj",
    )(x, norm_w.reshape(1, k), w)


def _win_attn_kernel(sink_ref, *refs, nq):
    bias_refs = refs[:nq]
    q_ref, kp_ref, kc_ref, kn_ref, vp_ref, vc_ref, vn_ref, o_ref = refs[nq:]
    kall = jnp.concatenate([kp_ref[...], kc_ref[...], kn_ref[...]], axis=0)
    vall = jnp.concatenate([vp_ref[...], vc_ref[...], vn_ref[...]], axis=0)
    sink2 = [sink_ref[h] * LOG2E for h in range(A_HEADS)]
    items = [(j, h) for j in range(nq) for h in range(A_HEADS)]
    qrows = lambda j: slice(j * BLOCK, (j + 1) * BLOCK)
    krows = lambda j: slice(j * BLOCK, (j + 3) * BLOCK)
    kcol = lambda h: slice((h // A_GROUP) * LANE, (h // A_GROUP + 1) * LANE)
    vcol = lambda h: slice((2 * (h // A_GROUP) + h % 2) * LANE, (2 * (h // A_GROUP) + h % 2 + 1) * LANE)
    s = [_dot_nt(q_ref[qrows(j), h * LANE:(h + 1) * LANE], kall[krows(j), kcol(h)]) for j, h in items]
    s = [sv + bias_refs[j][h] for (j, h), sv in zip(items, s)]
    m = [jnp.maximum(jnp.max(sv, axis=-1, keepdims=True), sink2[h]) for (_, h), sv in zip(items, s)]
    p = _each(lambda sv, mv: jnp.exp2(sv - mv), s, m)
    rden = [1.0 / (jnp.sum(pv_, axis=-1, keepdims=True) + jnp.exp2(sink2[h] - mv))
            for (_, h), pv_, mv in zip(items, p, m)]
    pv = {it: _dot(pp.astype(BF16), vall[krows(it[0]), vcol(it[1])]) * rd for it, pp, rd in zip(items, p, rden)}
    for j in range(nq):
        for pair in range(A_HEADS // 2):
            o_ref[qrows(j), pair * LANE:(pair + 1) * LANE] = (pv[(j, 2 * pair)] + pv[(j, 2 * pair + 1)]).astype(o_ref.dtype)


def _window_bias():
    qi = np.arange(BLOCK)[:, None]
    kj = np.arange(3 * BLOCK)[None, :]
    dist = np.abs(qi + BLOCK - kj)
    slopes = np.array([2.0 ** (-8.0 * (h + 1) / A_HEADS) for h in range(A_HEADS)], np.float32)
    base = -slopes[:, None, None] * dist[None].astype(np.float32)
    out = []
    for first in (0, 1):
        for last in (0, 1):
            valid = (dist <= WINDOW) & ((kj >= BLOCK) | (first == 0)) & ((kj < 2 * BLOCK) | (last == 0))
            out.append(np.where(valid[None], base, -np.inf))
    return jnp.asarray(np.stack(out) * LOG2E, F32)


def window_attention(q, k, v, sink, batch, seq):
    nb = seq // BLOCK
    nq = next(n for n in (8, 4, 2, 1) if nb % n == 0)
    ns = nb // nq
    cur = lambda b, n: (b * ns + n, 0)
    prev = lambda b, n: (b * nb + jnp.maximum(n * nq - 1, 0), 0)
    nxt = lambda b, n: (b * nb + jnp.minimum((n + 1) * nq, nb - 1), 0)

    def variant(j):
        first = lambda n: (n == 0).astype(jnp.int32) if j == 0 else 0
        last = lambda n: (n == ns - 1).astype(jnp.int32) if j == nq - 1 else 0
        return lambda b, n: (2 * first(n) + last(n), 0, 0, 0)

    kw, vw = k.shape[1], v.shape[1]
    bias = _window_bias()
    return pl.pallas_call(
        functools.partial(_win_attn_kernel, nq=nq),
        grid=(batch, ns),
        in_specs=([pl.BlockSpec(memory_space=pltpu.SMEM)]
                  + [pl.BlockSpec((None, A_HEADS, BLOCK, 3 * BLOCK), variant(j)) for j in range(nq)]
                  + [pl.BlockSpec((nq * BLOCK, q.shape[1]), cur),
                     pl.BlockSpec((BLOCK, kw), prev), pl.BlockSpec((nq * BLOCK, kw), cur), pl.BlockSpec((BLOCK, kw), nxt),
                     pl.BlockSpec((BLOCK, vw), prev), pl.BlockSpec((nq * BLOCK, vw), cur),
                     pl.BlockSpec((BLOCK, vw), nxt)]),
        out_specs=pl.BlockSpec((nq * BLOCK, A_HEADS * HEAD_DIM), cur),
        out_shape=jax.ShapeDtypeStruct((batch * seq, A_HEADS * HEAD_DIM), BF16),
        compiler_params=_cparams("parallel", "parallel"),
        name="window_attention",
    )(sink, *([bias] * nq), q, k, k, k, v, v, v)


def _attn_in_kernel(x_ref, nw_ref, w_ref, qnw_ref, wq_ref, kvnw_ref, wk_ref, wv_ref, qtab_ref, ctab_ref, stab_ref,
                    qa_ref, ka_ref, va_ref, q_ref, k_ref, v_ref, *, segs):
    xn = _rms(x_ref[...], nw_ref[...]).astype(BF16)
    proj = [_dot(xn, w_ref[:, start:start + width]) for start, width in segs]
    qa_ref[...] = (proj[0] * WINDOW_Q_SCALE).astype(qa_ref.dtype)
    ka_ref[...] = proj[1].astype(ka_ref.dtype)
    va_ref[...] = proj[2].astype(va_ref.dtype)
    cq, ckv, kab = proj[3:]
    xq = _rms(cq, qnw_ref[...]).astype(BF16)
    tab = qtab_ref[...]
    for h in range(B_HEADS):
        cols = slice(h * LANE, (h + 1) * LANE)
        q_ref[:, cols] = (_dot(xq, wq_ref[:, cols]) * tab).astype(q_ref.dtype)
    xkv = _rms(ckv, kvnw_ref[...]).astype(BF16)
    kr = kab[:, :LANE] * ctab_ref[...] + kab[:, LANE:] * stab_ref[...]
    for h in range(B_HEADS):
        cols = slice(h * LANE, (h + 1) * LANE)
        k_ref[:, cols] = (_dot(xkv, wk_ref[:, cols]) + kr).astype(k_ref.dtype)
    lane = lax.broadcasted_iota(jnp.int32, v_ref.shape, 1) % LANE
    v_ref[...] = jnp.where(lane == B_VDIM, 1.0, _dot(xkv, wv_ref[...])).astype(v_ref.dtype)


def attn_in_proj(x, norm_w, w, segs, q_norm_w, wq, kv_norm_w, wk, wv, q_tab, k_cos, k_sin, seq, tm):
    t, k = x.shape
    nt = seq // tm
    row = lambda i: (i, 0)
    pos = lambda i: (i % nt, 0)
    widths = [wd for _, wd in segs[:3]] + [wq.shape[1], wk.shape[1], wv.shape[1]]
    return pl.pallas_call(
        functools.partial(_attn_in_kernel, segs=tuple(segs)),
        grid=(t // tm,),
        in_specs=[pl.BlockSpec((tm, k), row), _resident((1, k)), _resident(w.shape),
                  _resident((1, wq.shape[0])), _resident(wq.shape),
                  _resident((1, wk.shape[0])), _resident(wk.shape), _resident(wv.shape),
                  pl.BlockSpec((tm, LANE), pos), pl.BlockSpec((tm, LANE), pos), pl.BlockSpec((tm, LANE), pos)],
        out_specs=[pl.BlockSpec((tm, wd), row) for wd in widths],
        out_shape=[jax.ShapeDtypeStruct((t, wd), BF16) for wd in widths],
        compiler_params=_cparams("parallel"),
        name="attn_in_proj",
    )(x, norm_w.reshape(1, k), w, q_norm_w.reshape(1, -1), wq, kv_norm_w.reshape(1, -1), wk, wv,
      q_tab, k_cos, k_sin)


def _mla_flash_kernel(q_ref, k_ref, v_ref, o_ref, s_ref, p_ref, m_ref, a_ref, acc_ref, *, tk, nk, rb):
    tq = q_ref.shape[0]
    heads = range(2)
    m_ref[...] = jnp.full(m_ref.shape, -jnp.inf, F32)
    acc_ref[...] = jnp.zeros(acc_ref.shape, F32)

    def body(j, carry):
        rows = pl.ds(pl.multiple_of(j * tk, tk), tk)
        for e in heads:
            cols = slice(e * LANE, (e + 1) * LANE)
            s_ref[e] = _dot_nt(q_ref[:, cols], k_ref[rows, cols])
        for e in heads:
            cols = slice(e * LANE, (e + 1) * LANE)
            for r in range(tq // rb):
                rs = slice(r * rb, (r + 1) * rb)
                s = s_ref[e, rs, :]
                m_old = m_ref[e, rs, :]
                m_new = jnp.maximum(m_old, jnp.max(s, axis=-1, keepdims=True))
                m_ref[e, rs, :] = m_new
                a_ref[e, rs, :] = jnp.exp2(m_old - m_new)
                p_ref[e, rs, :] = jnp.exp2(s - jnp.concatenate([m_new] * (tk // LANE), axis=1)).astype(BF16)
            acc_ref[e] = a_ref[e] * acc_ref[e] + _dot(p_ref[e], v_ref[rows, cols])
        return carry

    lax.fori_loop(0, nk, body, 0, unroll=True)
    outs = []
    for e in heads:
        acc = acc_ref[e]
        outs.append(acc / acc[:, B_VDIM:B_VDIM + 1])
    lane = lax.broadcasted_iota(jnp.int32, (tq, LANE), 1)
    o_ref[...] = jnp.where(lane < B_VDIM, outs[0], pltpu.roll(outs[1], B_VDIM, axis=1)).astype(o_ref.dtype)


def mla_flash(q, k, v, batch, seq, tq, tk, rb):
    nq = seq // tq
    npair = B_HEADS // 2
    return pl.pallas_call(
        functools.partial(_mla_flash_kernel, tk=tk, nk=seq // tk, rb=rb),
        grid=(batch, npair, nq),
        in_specs=[pl.BlockSpec((tq, 2 * LANE), lambda b, p, i: (b * nq + i, p)),
                  pl.BlockSpec((seq, 2 * LANE), lambda b, p, i: (b, p)),
                  pl.BlockSpec((seq, 2 * LANE), lambda b, p, i: (b, p))],
        out_specs=pl.BlockSpec((tq, LANE), lambda b, p, i: (b * nq + i, p)),
        out_shape=jax.ShapeDtypeStruct((batch * seq, B_HEADS * B_VDIM), BF16),
        scratch_shapes=[pltpu.VMEM((2, tq, tk), F32), pltpu.VMEM((2, tq, tk), BF16), pltpu.VMEM((2, tq, LANE), F32),
                        pltpu.VMEM((2, tq, LANE), F32), pltpu.VMEM((2, tq, LANE), F32)],
        compiler_params=_cparams("parallel", "parallel", "parallel"),
        name="mla_flash",
    )(q, k, v)


def _mlp_tail(x, nw_ref, w1_ref, w2_ref, fw_ref, o_ref, fchunk, final_norm):
    xn = _rms(x, nw_ref[...]).astype(BF16)
    acc = x
    for f in range(w1_ref.shape[1] // fchunk):
        cols = slice(f * fchunk, (f + 1) * fchunk)
        h = jnp.square(jnp.maximum(_dot(xn, w1_ref[:, cols]), 0.0)).astype(BF16)
        acc = acc + _dot(h, w2_ref[cols, :])
    if final_norm:
        acc = _rms(acc, fw_ref[...])
    o_ref[...] = acc


def _attn_out_mlp_kernel(x_ref, a_ref, b_ref, wo_ref, nw_ref, w1_ref, w2_ref, fw_ref, o_ref, *, fchunk, final_norm):
    ka = a_ref.shape[1]
    o_ref[...] = x_ref[...] + _dot(a_ref[...], wo_ref[:ka, :]) + _dot(b_ref[...], wo_ref[ka:, :])
    _mlp_tail(o_ref[...], nw_ref, w1_ref, w2_ref, fw_ref, o_ref, fchunk, final_norm)


def _lin_out_mlp_kernel(x_ref, cf_ref, cb_ref, df_ref, db_ref, gc_ref, gd_ref, cw_ref, dw_ref, wo_ref,
                        nw_ref, w1_ref, w2_ref, fw_ref, o_ref, *, fchunk, final_norm):
    def gated(f_ref, b_ref, gate_ref, gnw_ref):
        o = f_ref[...] + b_ref[...]
        parts = []
        for h in range(LIN_HEADS):
            cols = slice(h * LIN_DV, (h + 1) * LIN_DV)
            gate = gate_ref[:, cols].astype(F32)
            parts.append((_rms(o[:, cols], gnw_ref[...]) * (gate * _sigmoid(gate))).astype(BF16))
        return jnp.concatenate(parts, axis=1)

    kc = cf_ref.shape[1]
    o_ref[...] = (x_ref[...] + _dot(gated(cf_ref, cb_ref, gc_ref, cw_ref), wo_ref[:kc, :])
                  + _dot(gated(df_ref, db_ref, gd_ref, dw_ref), wo_ref[kc:, :]))
    _mlp_tail(o_ref[...], nw_ref, w1_ref, w2_ref, fw_ref, o_ref, fchunk, final_norm)


def mixer_out_mlp(x, mixer_inputs, small_params, w_out, norm_w, w1, w2, final_w, final_norm, tm, fchunk):
    t, d = x.shape
    row = lambda i: (i, 0)
    body = _attn_out_mlp_kernel if len(mixer_inputs) == 2 else _lin_out_mlp_kernel
    return pl.pallas_call(
        functools.partial(body, fchunk=fchunk, final_norm=final_norm),
        grid=(t // tm,),
        in_specs=([pl.BlockSpec((tm, d), row)] + [pl.BlockSpec((tm, a.shape[1]), row) for a in mixer_inputs]
                  + [_resident(p.shape) for p in small_params]
                  + [_resident(w_out.shape), _resident((1, d)), _resident(w1.shape), _resident(w2.shape),
                     _resident((1, d))]),
        out_specs=pl.BlockSpec((tm, d), row),
        out_shape=jax.ShapeDtypeStruct((t, d), F32),
        compiler_params=_cparams("parallel"),
        name="mixer_out_mlp",
    )(x, *mixer_inputs, *small_params, w_out, norm_w.reshape(1, d), w1, w2, final_w.reshape(1, d))


def _direction_masks(d):
    sgn = 1 - 2 * d
    r = lax.broadcasted_iota(jnp.int32, (STACK, STACK), 0)
    c = lax.broadcasted_iota(jnp.int32, (STACK, STACK), 1)
    same = (r & -CHUNK) == (c & -CHUNK)
    order = ((r & (CHUNK - 1)) - (c & (CHUNK - 1))) * sgn
    i = lax.broadcasted_iota(jnp.int32, (CHUNK, CHUNK), 0)
    j = lax.broadcasted_iota(jnp.int32, (CHUNK, CHUNK), 1)
    cum = (((i - j) * sgn) >= 0).astype(F32)
    return same, same & (order >= 0), same & (order > 0), cum


def _block_diag(x, same):
    return jnp.where(same, jnp.concatenate([x] * LIN_HEADS, axis=0), 0.0)


def _block_diag_cols(x, same):
    return jnp.where(same, jnp.concatenate([x] * LIN_HEADS, axis=1), 0.0)


def _stack_heads(v, width):
    return jnp.concatenate([v[:, h * width:(h + 1) * width] for h in range(LIN_HEADS)], axis=0)


def _each(f, *lists):
    return [f(*args) for args in zip(*lists)]


def _scan_order(ncb):
    return [((0, i), (1, ncb - 1 - i)) for i in range(ncb)]


def _gla_kernel(qf_ref, kf_ref, vf_ref, sf_ref, qb_ref, kb_ref, vb_ref, sb_ref, wg_ref, bg_ref,
                of_ref, ob_ref, st_ref, *, ncb, nb):
    @pl.when(pl.program_id(1) == 0)
    def _():
        st_ref[...] = jnp.zeros_like(st_ref)

    refs = ((qf_ref, kf_ref, vf_ref, sf_ref, of_ref), (qb_ref, kb_ref, vb_ref, sb_ref, ob_ref))
    masks = [_direction_masks(d) for d in range(2)]
    items = [(bi, d, ci) for bi in range(nb) for d in range(2) for ci in range(ncb)]
    rows = lambda ci: slice(ci * CHUNK, (ci + 1) * CHUNK)
    chunk = lambda which, it: refs[it[1]][which][it[0], rows(it[2]), :]
    same = [masks[d][0] for _, d, _ in items]
    z = [_dot(chunk(3, it).astype(BF16), wg_ref[it[1]]) + bg_ref[it[1]] for it in items]
    log_a = _each(lambda a: (jnp.minimum(a, 0.0) - jnp.log1p(jnp.exp(-jnp.abs(a)))) * (1.0 / GATE_NORM), z)
    b = [_dot_mask_f32(masks[it[1]][3].astype(BF16), a) for it, a in zip(items, log_a)]
    b_last = _each(lambda a: jnp.sum(a, axis=0, keepdims=True), log_a)
    q_in = [_block_diag(chunk(0, it) * (LIN_DK ** -0.5) * jnp.exp(bb), sm).astype(BF16)
            for it, bb, sm in zip(items, b, same)]
    k = [chunk(1, it) for it in items]
    k_in = _each(lambda a, bb, sm: _block_diag(a * jnp.exp(-bb), sm).astype(BF16), k, b, same)
    k_out = _each(lambda a, bb, bl, sm: _block_diag(a * jnp.exp(bl - bb), sm).astype(BF16), k, b, b_last, same)
    v = [_stack_heads(chunk(2, it), LIN_DV).astype(BF16) for it in items]
    sc = [jnp.where(masks[it[1]][1], _dot_nt(a, bb), 0.0).astype(BF16) for it, a, bb in zip(items, q_in, k_in)]
    o_intra = dict(zip(items, _each(_dot, sc, v)))
    ds = dict(zip(items, _each(_dot_tn, v, k_out)))
    dec = dict(zip(items, _each(jnp.exp, b_last)))
    q_in = dict(zip(items, q_in))

    streams = [(bi, d) for bi in range(nb) for d in range(2)]
    st = {sd: st_ref[i] for i, sd in enumerate(streams)}
    for fwd_bwd in _scan_order(ncb):
        step = [(bi, d, ci) for bi in range(nb) for d, ci in fwd_bwd]
        o = [o_intra[it] + _dot_nt(q_in[it], st[it[:2]].astype(BF16)) for it in step]
        for it, od in zip(step, o):
            bi, d, ci = it
            st[(bi, d)] = dec[it] * st[(bi, d)] + ds[it]
            for h in range(LIN_HEADS):
                refs[d][4][bi, rows(ci), h * LIN_DV:(h + 1) * LIN_DV] = od[h * CHUNK:(h + 1) * CHUNK, :]
    for i, sd in enumerate(streams):
        st_ref[i] = st[sd]


def _bidir_call(body, name, arrays, extra, extra_specs, batch, seq, tc, nb, hk, hv):
    nblk = seq // tc
    ins = [a.reshape(batch, seq, a.shape[1]) for a in arrays]
    fwd = lambda bb, n: (bb, n, 0)
    bwd = lambda bb, n: (bb, nblk - 1 - n, 0)
    spec = lambda a, idx: pl.BlockSpec((nb, tc, a.shape[2]), idx)
    o_f, o_b = pl.pallas_call(
        functools.partial(body, ncb=tc // CHUNK, nb=nb),
        grid=(batch // nb, nblk),
        in_specs=[spec(a, fwd) for a in ins] + [spec(a, bwd) for a in ins] + extra_specs,
        out_specs=[pl.BlockSpec((nb, tc, hv), fwd), pl.BlockSpec((nb, tc, hv), bwd)],
        out_shape=[jax.ShapeDtypeStruct((batch, seq, hv), F32)] * 2,
        scratch_shapes=[pltpu.VMEM((2 * nb, LIN_DV, hk), F32)],
        compiler_params=_cparams("parallel", "arbitrary"),
        name=name,
    )(*ins, *ins, *extra)
    return o_f.reshape(batch * seq, hv), o_b.reshape(batch * seq, hv)


def gla_scan(q, k, v, small, wg, bg, batch, seq, tc, nb):
    return _bidir_call(_gla_kernel, "gla_scan", (q, k, v, small), (wg, bg),
                       [_resident(wg.shape), _resident(bg.shape)], batch, seq, tc, nb, q.shape[1], v.shape[1])


def _gdn_prep_kernel(xp_ref, xc_ref, xn_ref, sm_ref, cw_ref, gsum_ref, alog_ref, dtb_ref,
                     q_ref, k_ref, v_ref, g_ref, b_ref, bt_ref, buf_ref, *, nt):
    i = pl.program_id(1)
    tm = xc_ref.shape[0]
    halo = xp_ref.shape[0]
    buf_ref[:halo, :] = xp_ref[...].astype(F32) * (i > 0).astype(F32)
    buf_ref[halo:halo + tm, :] = xc_ref[...].astype(F32)
    buf_ref[halo + tm:, :] = xn_ref[...].astype(F32) * (i < nt - 1).astype(F32)
    first = halo - CONV_TAPS // 2
    acc = buf_ref[first:first + tm, :] * cw_ref[0:1, :]
    for j in range(1, CONV_TAPS):
        acc = acc + buf_ref[first + j:first + j + tm, :] * cw_ref[j:j + 1, :]
    y = acc * _sigmoid(acc)
    nqk = LIN_HEADS * LIN_DK
    for idx, (o_ref, scale) in enumerate(((q_ref, LIN_DK ** -0.5), (k_ref, 1.0))):
        part = y[:, idx * nqk:(idx + 1) * nqk]
        ssq = _dot_f32_mask(part * part, gsum_ref[...])
        o_ref[...] = (part * lax.rsqrt(ssq + EPS) * scale).astype(o_ref.dtype)
    v_ref[...] = y[:, 2 * nqk:].astype(v_ref.dtype)
    sm = sm_ref[...]
    beta = _sigmoid(sm)
    g = -jnp.exp(alog_ref[...]) * _softplus(sm + dtb_ref[...])
    lane = lax.broadcasted_iota(jnp.int32, sm.shape, 1) - GATE_LANE0
    is_beta = (lane >= 0) & (lane < 2 * LIN_HEADS)
    is_g = (lane >= 2 * LIN_HEADS) & (lane < 4 * LIN_HEADS)
    g = jnp.where(is_g, g, 0.0)
    g_ref[...] = jnp.where(is_beta, beta, g)
    ri = lax.broadcasted_iota(jnp.int32, (CHUNK, CHUNK), 0)
    ci = lax.broadcasted_iota(jnp.int32, (CHUNK, CHUNK), 1)
    tril16 = (ri >= ci).astype(BF16)
    ones16 = jnp.ones((CHUNK, CHUNK), BF16)
    is_bwd = lax.broadcasted_iota(jnp.int32, (CHUNK, LANE), 1) >= GATE_LANE0 + 3 * LIN_HEADS
    for c in range(tm // CHUNK):
        rows = slice(c * CHUNK, (c + 1) * CHUNK)
        gc = g[rows, :]
        prefix = _dot_mask_f32(tril16, gc)
        total = _dot_mask_f32(ones16, gc)
        b_ref[rows, :] = jnp.where(is_bwd, total - prefix + gc, prefix)
        bt_ref[rows, :] = total


def gdn_prep(dqkv, small, conv_w, gsum, alog, dtb, batch, seq, tm):
    nt = seq // tm
    t, c = dqkv.shape
    halo = 2 * SUBLANE
    hb = tm // halo
    nqk = LIN_HEADS * LIN_DK
    row = lambda b, i: (b * nt + i, 0)
    prev = lambda b, i: (jnp.maximum((b * nt + i) * hb - 1, 0), 0)
    nxt = lambda b, i: (jnp.minimum((b * nt + i + 1) * hb, t // halo - 1), 0)
    return pl.pallas_call(
        functools.partial(_gdn_prep_kernel, nt=nt),
        grid=(batch, nt),
        in_specs=[pl.BlockSpec((halo, c), prev), pl.BlockSpec((tm, c), row), pl.BlockSpec((halo, c), nxt),
                  pl.BlockSpec((tm, LANE), row),
                  _resident(conv_w.shape), _resident(gsum.shape), _resident((1, LANE)), _resident((1, LANE))],
        out_specs=[pl.BlockSpec((tm, nqk), row), pl.BlockSpec((tm, nqk), row),
                   pl.BlockSpec((tm, c - 2 * nqk), row)] + [pl.BlockSpec((tm, LANE), row)] * 3,
        out_shape=[jax.ShapeDtypeStruct((t, nqk), BF16), jax.ShapeDtypeStruct((t, nqk), BF16),
                   jax.ShapeDtypeStruct((t, c - 2 * nqk), BF16)] + [jax.ShapeDtypeStruct((t, LANE), F32)] * 3,
        scratch_shapes=[pltpu.VMEM((tm + 2 * halo, c), F32)],
        compiler_params=_cparams("parallel", "parallel"),
        name="gdn_prep",
    )(dqkv, dqkv, dqkv, small, conv_w, gsum, alog, dtb)


def _gdn_kernel(qf_ref, kf_ref, vf_ref, gf_ref, bf_ref, tf_ref, qb_ref, kb_ref, vb_ref, gb_ref, bb_ref, tb_ref,
                of_ref, ob_ref, st_ref, *, ncb, nb):
    @pl.when(pl.program_id(1) == 0)
    def _():
        st_ref[...] = jnp.zeros_like(st_ref)

    refs = ((qf_ref, kf_ref, vf_ref, gf_ref, bf_ref, tf_ref, of_ref),
            (qb_ref, kb_ref, vb_ref, gb_ref, bb_ref, tb_ref, ob_ref))
    masks = [_direction_masks(d) for d in range(2)]
    r = lax.broadcasted_iota(jnp.int32, (STACK, STACK), 0)
    c = lax.broadcasted_iota(jnp.int32, (STACK, STACK), 1)
    diag = r == c
    eye = diag.astype(F32)
    lane_blk = lax.broadcasted_iota(jnp.int32, (SUBLANE, STACK), 1) // CHUNK

    def rows_of(x, lane0):
        return jnp.concatenate(
            [jnp.broadcast_to(x[:, lane0 + h:lane0 + h + 1], (CHUNK, STACK)) for h in range(LIN_HEADS)], axis=0)

    def head_lanes(row, lane0):
        out = jnp.broadcast_to(row[:, lane0:lane0 + 1], (SUBLANE, STACK))
        for h in range(1, LIN_HEADS):
            out = jnp.where(lane_blk == h, jnp.broadcast_to(row[:, lane0 + h:lane0 + h + 1], (SUBLANE, STACK)), out)
        return out

    each = _each
    items = [(bi, d, ci) for bi in range(nb) for d in range(2) for ci in range(ncb)]
    rows = lambda ci: slice(ci * CHUNK, (ci + 1) * CHUNK)
    chunk = lambda which, it: refs[it[1]][which][it[0], rows(it[2]), :]
    same = [masks[d][0] for _, d, _ in items]
    beta_col = [rows_of(chunk(3, it), _gate_lanes(it[1])[0]) for it in items]
    b_col = [rows_of(chunk(4, it), _gate_lanes(it[1])[1]) for it in items]
    bl_col = [rows_of(chunk(5, it), _gate_lanes(it[1])[1]) for it in items]
    k_bd = [_block_diag(chunk(1, it), sm) for it, sm in zip(items, same)]
    q_bd = [_block_diag(chunk(0, it), sm) for it, sm in zip(items, same)]
    kb_bd = each(lambda a, b: a * b, k_bd, beta_col)
    k16 = each(lambda a: a.astype(BF16), k_bd)
    kk = each(lambda a, b: _dot_nt(a.astype(BF16), b), kb_bd, k16)
    qk = each(lambda a, b: _dot_nt(a.astype(BF16), b), q_bd, k16)
    b_row = each(lambda bc: jnp.sum(jnp.where(diag, bc, 0.0), axis=0, keepdims=True), b_col)
    gamma = [jnp.exp(jnp.where(masks[it[1]][1], bc - br, -jnp.inf)) for it, bc, br in zip(items, b_col, b_row)]
    lower = [jnp.where(masks[it[1]][2], a * g, 0.0) for it, a, g in zip(items, kk, gamma)]
    attn = each(lambda a, g: (a * g).astype(BF16), qk, gamma)
    l_hi = each(lambda a: a.astype(BF16), lower)
    pw = [l_hi]
    for _ in range(5):
        pw.append(each(lambda a: _dot(a, a).astype(BF16), pw[-1]))

    def pair(a, b16):
        return a + _dot(a.astype(BF16), b16)

    f01 = each(lambda lo, p1: pair(eye - lo, p1), lower, pw[1])
    f23 = each(lambda p2, p3: pair(eye + p2.astype(F32), p3), pw[2], pw[3])
    f45 = each(lambda p4, p5: pair(eye + p4.astype(F32), p5), pw[4], pw[5])
    f0123 = each(lambda a, b: _dot(a.astype(BF16), b.astype(BF16)).astype(BF16), f01, f23)
    inv16 = each(lambda a, b: _dot(a, b.astype(BF16)).astype(BF16), f0123, f45)
    e_b = each(jnp.exp, b_col)
    vb = [_stack_heads(chunk(2, it), LIN_DV) * bc[:, :LIN_DV] for it, bc in zip(items, beta_col)]
    kbe = [_stack_heads(chunk(1, it), LIN_DK) * (bc * e)[:, :LIN_DK] for it, bc, e in zip(items, beta_col, e_b)]
    rhs = each(lambda v, kb: jnp.concatenate([v, kb], axis=1), vb, kbe)
    x16 = each(lambda m, r_: _dot(m, r_.astype(BF16)).astype(BF16), inv16, rhs)
    l_lo = each(lambda lo, hi: (lo - hi.astype(F32)).astype(BF16), lower, l_hi)
    resid = each(lambda r_, x, hi, lo: r_ - (x.astype(F32) + _dot(hi, x) + _dot(lo, x)), rhs, x16, l_hi, l_lo)
    sol = each(lambda x, m, rs: x.astype(F32) + _dot(m, rs.astype(BF16)), x16, inv16, resid)
    u = {it: sol[i][:, :LIN_DV] for i, it in enumerate(items)}
    w = {it: _block_diag_cols(sol[i][:, LIN_DV:].astype(BF16), same[i])
         for i, it in enumerate(items)}
    attn = dict(zip(items, attn))
    q_dec = {it: (q_bd[i] * e_b[i]).astype(BF16) for i, it in enumerate(items)}
    k_dec = {it: (k_bd[i] * jnp.exp(bl_col[i] - b_col[i])).astype(BF16) for i, it in enumerate(items)}
    dec = {(bi, d, ci): jnp.exp(head_lanes(refs[d][5][bi, ci * CHUNK:ci * CHUNK + SUBLANE, :],
                                           _gate_lanes(d)[1]))[:1, :] for bi, d, ci in items}

    streams = [(bi, d) for bi in range(nb) for d in range(2)]
    st = {sd: st_ref[i] for i, sd in enumerate(streams)}
    for fwd_bwd in _scan_order(ncb):
        step = [(bi, d, ci) for bi in range(nb) for d, ci in fwd_bwd]
        st16 = [st[it[:2]].astype(BF16) for it in step]
        v_new16 = [(u[it] - _dot_nt(w[it], s16)).astype(BF16) for it, s16 in zip(step, st16)]
        o = [_dot_nt(q_dec[it], s16) + _dot(attn[it], vn) for it, s16, vn in zip(step, st16, v_new16)]
        for it, od, vn in zip(step, o, v_new16):
            bi, d, ci = it
            st[(bi, d)] = dec[it] * st[(bi, d)] + _dot_tn(vn, k_dec[it])
            for h in range(LIN_HEADS):
                refs[d][6][bi, rows(ci), h * LIN_DV:(h + 1) * LIN_DV] = od[h * CHUNK:(h + 1) * CHUNK, :]
    for i, sd in enumerate(streams):
        st_ref[i] = st[sd]


def _gate_lanes(d):
    return GATE_LANE0 + d * LIN_HEADS, GATE_LANE0 + (2 + d) * LIN_HEADS


def gdn_scan(q, k, v, mixed, bcum, btot, batch, seq, tc, nb):
    return _bidir_call(_gdn_kernel, "gdn_scan", (q, k, v, mixed, bcum, btot), (), [],
                       batch, seq, tc, nb, q.shape[1], v.shape[1])


def _rot_half_cols(w):
    half = w.shape[-1] // 2
    return jnp.concatenate([-w[..., half:], w[..., :half]], axis=-1)


def _pad_heads(w, heads, dim, offset=0):
    k = w.shape[0]
    out = jnp.zeros((k, heads, LANE), w.dtype)
    out = out.at[:, :, offset:offset + dim].set(w.reshape(k, heads, dim))
    return out.reshape(k, heads * LANE)


def _layer0_in_weight(w_in):
    aq = A_HEADS * HEAD_DIM
    akv = A_KV_HEADS * HEAD_DIM
    o = 0
    w_q = w_in[:, o:o + aq]; o += aq
    w_k = w_in[:, o:o + akv]; o += akv
    w_v = w_in[:, o:o + akv]; o += akv
    w_cq = w_in[:, o:o + B_Q_RANK]; o += B_Q_RANK
    w_ckv = w_in[:, o:o + B_KV_RANK]; o += B_KV_RANK
    w_pe = w_in[:, o:o + B_ROPE]
    zeros = jnp.zeros((w_in.shape[0], B_NOPE), w_in.dtype)
    v_lo = _pad_heads(w_v, A_KV_HEADS, HEAD_DIM, 0).reshape(-1, A_KV_HEADS, LANE)
    v_hi = _pad_heads(w_v, A_KV_HEADS, HEAD_DIM, HEAD_DIM).reshape(-1, A_KV_HEADS, LANE)
    w_v2 = jnp.stack([v_lo, v_hi], axis=2).reshape(w_in.shape[0], 2 * A_KV_HEADS * LANE)
    w_pe_rot = _rot_half_cols(w_pe)
    pieces = [_pad_heads(w_q, A_HEADS, HEAD_DIM), _pad_heads(w_k, A_KV_HEADS, HEAD_DIM), w_v2,
              w_cq, w_ckv,
              jnp.concatenate([zeros, w_pe, w_pe, zeros, w_pe_rot, w_pe_rot], axis=1)]
    segs, start = [], 0
    for p in pieces:
        segs.append((start, p.shape[1]))
        start += p.shape[1]
    return jnp.concatenate(pieces, axis=1).astype(BF16), segs


def _mla_weights(w_uq, w_ukv):
    r = w_uq.shape[0]
    wq = w_uq.reshape(r, B_HEADS, B_NOPE + B_ROPE)
    pe = wq[:, :, B_NOPE:]
    wq = jnp.concatenate([wq[:, :, :B_NOPE], pe, _rot_half_cols(pe)], axis=-1).reshape(r, B_HEADS * LANE)
    rk = w_ukv.shape[0]
    wkv = w_ukv.reshape(rk, B_HEADS, B_NOPE + B_VDIM)
    wk = _pad_heads(wkv[:, :, :B_NOPE].reshape(rk, B_HEADS * B_NOPE), B_HEADS, B_NOPE)
    wv = _pad_heads(wkv[:, :, B_NOPE:].reshape(rk, B_HEADS * B_VDIM), B_HEADS, B_VDIM)
    return wq.astype(BF16), wk.astype(BF16), wv.astype(BF16)


def _rope_tables(seq):
    half = B_ROPE // 2
    inv = ROPE_THETA ** (-jnp.arange(half, dtype=F32) / half)
    ang = jnp.arange(seq, dtype=F32)[:, None] * inv[None, :]
    cos2 = jnp.tile(jnp.cos(ang), (1, 4))
    sin2 = jnp.tile(jnp.sin(ang), (1, 4))
    zeros = jnp.zeros((seq, B_NOPE), F32)
    qscale = (B_NOPE + B_ROPE) ** -0.5 * LOG2E
    q_tab = jnp.concatenate([jnp.ones((seq, B_NOPE), F32), cos2[:, :B_ROPE], sin2[:, :B_ROPE]], axis=1) * qscale
    k_cos = jnp.concatenate([zeros, cos2], axis=1)
    k_sin = jnp.concatenate([zeros, sin2], axis=1)
    return q_tab, k_cos, k_sin


def _layer1_in_weight(w_in):
    hk = LIN_HEADS * LIN_DK
    hv = LIN_HEADS * LIN_DV
    conv_ch = 2 * hk + hv
    o = 2 * hk + 2 * hv
    w_main = w_in[:, :o]
    w_gl = w_in[:, o:o + 2 * GATE_RANK]; o += 2 * GATE_RANK
    w_conv = w_in[:, o:o + conv_ch]; o += conv_ch
    w_z = w_in[:, o:o + hv]; o += hv
    w_small = w_in[:, o:o + 4 * LIN_HEADS]
    pad = jnp.zeros((w_in.shape[0], LANE - 2 * GATE_RANK - 4 * LIN_HEADS), w_in.dtype)
    w = jnp.concatenate([w_main, w_conv, w_z, w_gl, w_small, pad], axis=1).astype(BF16)
    widths = [hk, hk, hv, hv, conv_ch, hv, LANE]
    segs, start = [], 0
    for wd in widths:
        segs.append((start, wd))
        start += wd
    return w, segs


def _pick_tile(n, pref):
    t = min(n, pref)
    while n % t:
        t //= 2
    return t


def kernel(x, att_norm, att_w_in, att_sink, mla_q_norm, mla_w_uq, mla_kv_norm, mla_w_ukv, att_w_out, lin_norm, lin_w_in, gla_w_gate_f, gla_b_gate_f, gla_w_gate_b, gla_b_gate_b, gla_norm, gdn_conv, gdn_a_log_f, gdn_dt_bias_f, gdn_a_log_b, gdn_dt_bias_b, gdn_norm, lin_w_out, mlp_norm, mlp_w1, mlp_w2, final_norm):
    batch, seq, dm = x.shape
    t = batch * seq
    depth = mlp_norm.shape[0]
    tm = _pick_tile(seq, ROW_TILE)
    tp = _pick_tile(seq, PROJ_ROW_TILE)
    xs = x.reshape(t, dm)
    for layer in range(depth):
        i = layer // 2
        if layer % 2 == 0:
            w0, segs = _layer0_in_weight(att_w_in[i])
            wq, wk, wv = _mla_weights(mla_w_uq[i], mla_w_ukv[i])
            q_tab, k_cos, k_sin = _rope_tables(seq)
            qa, ka, va, qb, kb, vb = attn_in_proj(xs, att_norm[i], w0, segs, mla_q_norm[i], wq, mla_kv_norm[i],
                                                  wk, wv, q_tab, k_cos, k_sin, seq, tp)
            o_a = window_attention(qa, ka, va, att_sink[i].astype(F32), batch, seq)
            o_b = mla_flash(qb, kb, vb, batch, seq, _pick_tile(seq, FLASH_Q_TILE), _pick_tile(seq, FLASH_KV_TILE),
                            FLASH_ROW_BLOCK)
            mixer_inputs, small_params, w_out = [o_a, o_b], [], att_w_out[i]
        else:
            w1, segs = _layer1_in_weight(lin_w_in[i])
            cq, ck, cv, cg, dqkv, dz, small = norm_proj(xs, lin_norm[i], w1, segs, [BF16] * 6 + [F32], tp)
            hk = LIN_HEADS * LIN_DK
            wg = jnp.zeros((2, LANE, hk), F32)
            wg = wg.at[0, :GATE_RANK].set(gla_w_gate_f[i]).at[1, GATE_RANK:2 * GATE_RANK].set(gla_w_gate_b[i])
            bg = jnp.stack([gla_b_gate_f[i], gla_b_gate_b[i]]).reshape(2, 1, hk).astype(F32)
            tc = _pick_tile(seq, SCAN_ROWS)
            nb = 2 if batch % 2 == 0 else 1
            o_cf, o_cb = gla_scan(cq, ck, cv, small, wg.astype(BF16), bg, batch, seq, _pick_tile(seq, 2 * SCAN_ROWS), nb)
            gsum = (jnp.arange(hk)[:, None] // LIN_DK == jnp.arange(hk)[None, :] // LIN_DK).astype(BF16)
            nh = LIN_HEADS
            a0 = GATE_LANE0 + 2 * nh
            alog = jnp.zeros((LANE,), F32).at[a0:a0 + nh].set(gdn_a_log_f[i]).at[a0 + nh:a0 + 2 * nh].set(gdn_a_log_b[i])
            dtb = jnp.zeros((LANE,), F32).at[a0:a0 + nh].set(gdn_dt_bias_f[i]).at[a0 + nh:a0 + 2 * nh].set(gdn_dt_bias_b[i])
            dq, dk, dv, mixed, bcum, btot = gdn_prep(dqkv, small, gdn_conv[i].astype(F32), gsum,
                                                     alog.reshape(1, LANE), dtb.reshape(1, LANE), batch, seq, tp)
            o_df, o_db = gdn_scan(dq, dk, dv, mixed, bcum, btot, batch, seq, tc, nb)
            mixer_inputs = [o_cf, o_cb, o_df, o_db, cg, dz]
            small_params = [gla_norm[i].reshape(1, LIN_DV), gdn_norm[i].reshape(1, LIN_DV)]
            w_out = lin_w_out[i]
        last = layer == depth - 1
        tm_mlp = _pick_tile(seq, ATTN_MLP_ROW_TILE) if layer % 2 == 0 else tm
        xs = mixer_out_mlp(xs, mixer_inputs, small_params, w_out.astype(BF16), mlp_norm[layer],
                           mlp_w1[layer].astype(BF16), mlp_w2[layer].astype(BF16), final_norm, last, tm_mlp,
                           MLP_F_CHUNK)
    return xs.reshape(batch, seq, dm)
```
